```python
import math
import jax, jax.numpy as jnp
from jax import lax
import numpy as np

D_MODEL = 1024
BATCH = 16
SEQ = 2048
DEPTH = 1

CHUNK = 64
Q_BLOCK = 128

FOX_HEADS = 8
FOX_HEAD_DIM = 64
FOX_WIDTH = FOX_HEADS * FOX_HEAD_DIM

GLA_HEADS = 4
GLA_KEY_DIM = 128
GLA_VAL_DIM = 256
GLA_QK_WIDTH = GLA_HEADS * GLA_KEY_DIM
GLA_V_WIDTH = GLA_HEADS * GLA_VAL_DIM
GLA_GATE_RANK = 16
GLA_GATE_TEMP = 16.0
GLA_NORM_EPS = 1e-5

PEER_HEADS = 8
PEER_N_KEYS = 128
PEER_N_EXPERTS = PEER_N_KEYS * PEER_N_KEYS
PEER_QUERY_DIM = 256
PEER_HALF = PEER_QUERY_DIM // 2
PEER_TOPK = 16
PEER_TOKEN_BLOCK = 128

ALPHA = (2.0 * DEPTH) ** 0.25
BETA = (8.0 * DEPTH) ** -0.25
LN_EPS = 1e-5

IN_SPLIT_SIZES = (FOX_WIDTH, FOX_WIDTH, FOX_WIDTH, FOX_HEADS,
                  GLA_QK_WIDTH, GLA_QK_WIDTH, GLA_V_WIDTH, GLA_V_WIDTH, GLA_GATE_RANK,
                  D_MODEL, D_MODEL)
IN_WIDTH = sum(IN_SPLIT_SIZES)
IN_SPLIT_POINTS = tuple(int(v) for v in np.cumsum(IN_SPLIT_SIZES)[:-1])

kernel_name = "hybrid_fox_gla_peer_block"


def layer_norm(x, gain, bias):
    xf = x.astype(jnp.float32)
    mu = jnp.mean(xf, axis=-1, keepdims=True)
    var = jnp.mean(jnp.square(xf - mu), axis=-1, keepdims=True)
    y = (xf - mu) * lax.rsqrt(var + LN_EPS)
    return (y * gain.astype(jnp.float32) + bias.astype(jnp.float32)).astype(x.dtype)


def fox_attention(q, k, v, f_logit):
    B, S, H, dh = q.shape
    log_f = jax.nn.log_sigmoid(f_logit.astype(jnp.float32))
    c = jnp.cumsum(log_f, axis=1).transpose(0, 2, 1)
    scale = dh ** -0.5
    outs = []
    for i in range(S // Q_BLOCK):
        q0, q1 = i * Q_BLOCK, (i + 1) * Q_BLOCK
        qb, kb, vb = q[:, q0:q1], k[:, :q1], v[:, :q1]
        logits = jnp.einsum('bqhd,bkhd->bhqk', qb, kb).astype(jnp.float32) * scale
        logits = logits + c[:, :, q0:q1, None] - c[:, :, None, :q1]
        q_pos = jnp.arange(q0, q1)[:, None]
        k_pos = jnp.arange(q1)[None, :]
        logits = jnp.where(k_pos <= q_pos, logits, -jnp.inf)
        probs = jax.nn.softmax(logits, axis=-1).astype(v.dtype)
        outs.append(jnp.einsum('bhqk,bkhd->bqhd', probs, vb))
    return jnp.concatenate(outs, axis=1)


def gla_chunk_attention(q, k, v, log_a):
    B, S, H, dk = q.shape
    dv = v.shape[-1]
    nc = S // CHUNK

    def to_chunks(t):
        return t.reshape(B, nc, CHUNK, *t.shape[2:]).swapaxes(0, 1)

    la = to_chunks(log_a)
    cum = jnp.cumsum(la, axis=2)
    total = cum[:, :, -1]
    k_dec = to_chunks(k).astype(jnp.float32) * jnp.exp(total[:, :, None] - cum)
    qc = to_chunks(q).astype(jnp.float32)
    vc = to_chunks(v).astype(jnp.float32)

    def step(state, inp):
        q_c, k_c, v_c, tot = inp
        state = state * jnp.exp(tot)[..., None] + jnp.einsum('bchk,bchv->bhkv', k_c, v_c)
        o_c = jnp.einsum('bchk,bhkv->bchv', q_c, state)
        return state, o_c

    init = jnp.zeros((B, H, dk, dv), jnp.float32)
    _, o = lax.scan(step, init, (qc, k_dec, vc, total))
    return o.swapaxes(0, 1).reshape(B, S, H, dv)


def peer_layer(h, w_query, sub_keys, expert_u, expert_v):
    B, S, D = h.shape
    T = B * S
    ht = h.reshape(T, D)
    q = jnp.einsum('td,dhq->thq', ht, w_query).reshape(T, PEER_HEADS, 2, PEER_HALF)
    scores = jnp.einsum('thpc,hpnc->thpn', q, sub_keys).astype(jnp.float32)
    s_top, i_top = lax.top_k(scores, PEER_TOPK)
    cand = (s_top[:, :, 0, :, None] + s_top[:, :, 1, None, :]).reshape(T, PEER_HEADS, PEER_TOPK * PEER_TOPK)
    cand_idx = (i_top[:, :, 0, :, None] * PEER_N_KEYS + i_top[:, :, 1, None, :]).reshape(T, PEER_HEADS, PEER_TOPK * PEER_TOPK)
    best, pos = lax.top_k(cand, PEER_TOPK)
    expert_idx = jnp.take_along_axis(cand_idx, pos, axis=-1)
    gates = jax.nn.softmax(best, axis=-1).astype(h.dtype)

    nb = T // PEER_TOKEN_BLOCK
    hb = ht.reshape(nb, PEER_TOKEN_BLOCK, D)
    ib = expert_idx.reshape(nb, PEER_TOKEN_BLOCK, PEER_HEADS * PEER_TOPK)
    gb = gates.reshape(nb, PEER_TOKEN_BLOCK, PEER_HEADS * PEER_TOPK)

    def block(args):
        h_blk, i_blk, g_blk = args
        u = expert_u[i_blk]
        vv = expert_v[i_blk]
        act = jax.nn.gelu(jnp.einsum('td,tkd->tk', h_blk, u), approximate=False)
        return jnp.einsum('tk,tkd->td', act * g_blk, vv)

    out = lax.map(block, (hb, ib, gb))
    return out.reshape(B, S, D)


def setup_inputs(seed: int = 0) -> dict:
    key = jax.random.key(seed)
    ks = jax.random.split(key, 17)
    f32 = jnp.float32
    nrm = lambda k, shape: jax.random.normal(k, shape, f32)
    L = DEPTH
    return {
        "x": nrm(ks[0], (BATCH, SEQ, D_MODEL)),
        "w_in": nrm(ks[1], (L, D_MODEL, IN_WIDTH)) * D_MODEL ** -0.5,
        "fox_f_bias": 3.0 + 0.5 * nrm(ks[2], (L, FOX_HEADS)),
        "gla_gate_up": nrm(ks[3], (L, GLA_GATE_RANK, GLA_QK_WIDTH)) * GLA_GATE_RANK ** -0.5,
        "gla_gate_bias": 0.1 * nrm(ks[4], (L, GLA_QK_WIDTH)),
        "gla_norm_gain": 1.0 + 0.02 * nrm(ks[5], (L, GLA_HEADS, GLA_VAL_DIM)),
        "w_out_fox": nrm(ks[6], (L, FOX_WIDTH, D_MODEL)) * FOX_WIDTH ** -0.5,
        "w_out_gla": nrm(ks[7], (L, GLA_V_WIDTH, D_MODEL)) * GLA_V_WIDTH ** -0.5,
        "w_out": nrm(ks[8], (L, D_MODEL, D_MODEL)) * (D_MODEL ** -0.5 * BETA),
        "ln1_gain": 1.0 + 0.02 * nrm(ks[9], (L, D_MODEL)),
        "ln1_bias": 0.02 * nrm(ks[10], (L, D_MODEL)),
        "peer_w_query": nrm(ks[11], (L, D_MODEL, PEER_HEADS, PEER_QUERY_DIM)) * D_MODEL ** -0.5,
        "peer_sub_keys": nrm(ks[12], (L, PEER_HEADS, 2, PEER_N_KEYS, PEER_HALF)) * PEER_HALF ** -0.5,
        "peer_expert_u": nrm(ks[13], (L, PEER_N_EXPERTS, D_MODEL)) * D_MODEL ** -0.5,
        "peer_expert_v": nrm(ks[14], (L, PEER_N_EXPERTS, D_MODEL)) * BETA,
        "ln2_gain": 1.0 + 0.02 * nrm(ks[15], (L, D_MODEL)),
        "ln2_bias": 0.02 * nrm(ks[16], (L, D_MODEL)),
    }


def reference(x, w_in, fox_f_bias, gla_gate_up, gla_gate_bias, gla_norm_gain, w_out_fox, w_out_gla,
              w_out, ln1_gain, ln1_bias, peer_w_query, peer_sub_keys, peer_expert_u, peer_expert_v,
              ln2_gain, ln2_bias):
    B, S, D = x.shape
    for l in range(DEPTH):
        proj = jnp.einsum('bsd,de->bse', x, w_in[l])
        (fq, fk, fv, ff, gq, gk, gv, gg, glr, mga, mgb) = jnp.split(proj, IN_SPLIT_POINTS, axis=-1)

        y_fox = fox_attention(fq.reshape(B, S, FOX_HEADS, FOX_HEAD_DIM),
                              fk.reshape(B, S, FOX_HEADS, FOX_HEAD_DIM),
                              fv.reshape(B, S, FOX_HEADS, FOX_HEAD_DIM),
                              ff + fox_f_bias[l])
        y_fox = jnp.einsum('bsf,fd->bsd', y_fox.reshape(B, S, FOX_WIDTH), w_out_fox[l])

        log_a = jax.nn.log_sigmoid((glr @ gla_gate_up[l] + gla_gate_bias[l]).astype(jnp.float32)) / GLA_GATE_TEMP
        o = gla_chunk_attention((gq * GLA_KEY_DIM ** -0.5).reshape(B, S, GLA_HEADS, GLA_KEY_DIM),
                                gk.reshape(B, S, GLA_HEADS, GLA_KEY_DIM),
                                gv.reshape(B, S, GLA_HEADS, GLA_VAL_DIM),
                                log_a.reshape(B, S, GLA_HEADS, GLA_KEY_DIM))
        o = o * lax.rsqrt(jnp.mean(jnp.square(o), axis=-1, keepdims=True) + GLA_NORM_EPS) * gla_norm_gain[l].astype(jnp.float32)
        o = o.reshape(B, S, GLA_V_WIDTH).astype(x.dtype) * jax.nn.silu(gg)
        y_gla = jnp.einsum('bsv,vd->bsd', o, w_out_gla[l])

        merged = jax.nn.sigmoid(mga) * y_fox + jax.nn.sigmoid(mgb) * y_gla
        mix = jnp.einsum('bsd,de->bse', merged, w_out[l])
        x = layer_norm(ALPHA * x + mix, ln1_gain[l], ln1_bias[l])

        ffn = peer_layer(x, peer_w_query[l], peer_sub_keys[l], peer_expert_u[l], peer_expert_v[l])
        x = layer_norm(ALPHA * x + ffn, ln2_gain[l], ln2_bias[l])
    return x
```

```python
import functools
import math

import jax
import jax.numpy as jnp
from jax import lax
from jax.experimental import pallas as pl
from jax.experimental.pallas import tpu as pltpu

F32 = jnp.float32
BF16 = jnp.bfloat16

D_MODEL = 1024
FOX_HEADS = 8
FOX_HEAD_DIM = 64
FOX_WIDTH = FOX_HEADS * FOX_HEAD_DIM
GLA_HEADS = 4
GLA_KEY_DIM = 128
GLA_VAL_DIM = 256
GLA_QK_WIDTH = GLA_HEADS * GLA_KEY_DIM
GLA_V_WIDTH = GLA_HEADS * GLA_VAL_DIM
GLA_GATE_RANK = 16
GLA_GATE_TEMP = 16.0
GLA_NORM_EPS = 1e-5
GLA_CHUNK = 64
PEER_HEADS = 8
PEER_N_KEYS = 128
PEER_HALF = 128
PEER_TOPK = 16
PEER_PICKS = PEER_HEADS * PEER_TOPK
DEPTH = 1
ALPHA = (2.0 * DEPTH) ** 0.25
LN_EPS = 1e-5

LANES = 128
SUBLANES = 8
VMEM_LIMIT = 52 * 1024 * 1024

IN_SPLIT_SIZES = (FOX_WIDTH, FOX_WIDTH, FOX_WIDTH, FOX_HEADS,
                  GLA_QK_WIDTH, GLA_QK_WIDTH, GLA_V_WIDTH, GLA_V_WIDTH, GLA_GATE_RANK,
                  D_MODEL, D_MODEL)
FF_COL = 0
GLR_COL = FOX_HEADS


def _dot(a, b, **kw):
    return jnp.dot(a, b, preferred_element_type=F32, **kw)


def _dot_nt(a, b):
    return lax.dot_general(a, b, (((1,), (1,)), ((), ())), preferred_element_type=F32)


def _dot_tn(a, b):
    return lax.dot_general(a, b, (((0,), (0,)), ((), ())), preferred_element_type=F32)


def _layer_norm(y, gain, bias):
    mu = jnp.mean(y, axis=-1, keepdims=True)
    yc = y - mu
    var = jnp.mean(yc * yc, axis=-1, keepdims=True)
    return yc * lax.rsqrt(var + LN_EPS) * gain + bias


def _in_proj_kernel(x_ref, wf_ref, wgqk_ref, wgv_ref, wgg_ref, wm_ref, ws_ref,
                    f_ref, gqk_ref, gv_ref, gg_ref, m_ref, s_ref):
    xb = x_ref[...].astype(BF16)
    f_ref[...] = _dot(xb, wf_ref[...]).astype(BF16)
    gqk_ref[...] = _dot(xb, wgqk_ref[...]).astype(BF16)
    gv_ref[...] = _dot(xb, wgv_ref[...]).astype(BF16)
    gg_ref[...] = _dot(xb, wgg_ref[...])
    m_ref[...] = _dot(xb, wm_ref[...])
    s_ref[...] = _dot(xb, ws_ref[...])


def _in_proj(x2, w_in):
    T = x2.shape[0]
    tm = 256
    pts = [0]
    for s in IN_SPLIT_SIZES:
        pts.append(pts[-1] + s)
    col = lambda i, j: w_in[:, pts[i]:pts[j]]
    wf = col(0, 3).astype(BF16)
    wgqk = col(4, 6).astype(BF16)
    wgv = col(6, 7).astype(BF16)
    wgg = col(7, 8).astype(BF16)
    wm = col(9, 11).astype(BF16)
    ws = jnp.concatenate([col(3, 4), col(8, 9)], axis=1)
    ws = jnp.pad(ws, ((0, 0), (0, LANES - ws.shape[1]))).astype(BF16)
    ws_list = [wf, wgqk, wgv, wgg, wm, ws]
    out_dtypes = [BF16, BF16, BF16, F32, F32, F32]
    const = lambda w: pl.BlockSpec(w.shape, lambda i: (0, 0))
    return pl.pallas_call(
        _in_proj_kernel,
        grid=(T // tm,),
        in_specs=[pl.BlockSpec((tm, D_MODEL), lambda i: (i, 0))] + [const(w) for w in ws_list],
        out_specs=[pl.BlockSpec((tm, w.shape[1]), lambda i: (i, 0)) for w in ws_list],
        out_shape=[jax.ShapeDtypeStruct((T, w.shape[1]), dt) for w, dt in zip(ws_list, out_dtypes)],
        compiler_params=pltpu.CompilerParams(dimension_semantics=("arbitrary",), vmem_limit_bytes=VMEM_LIMIT),
        name="in_proj",
    )(x2, *ws_list)


def _fox_gate_kernel(s_ref, bias_ref, c_ref):
    S = s_ref.shape[0]
    ff_t = s_ref[...].T[FF_COL:FF_COL + FOX_HEADS, :]
    log_f = jax.nn.log_sigmoid(ff_t + bias_ref[...])
    r = lax.broadcasted_iota(jnp.int32, (LANES, LANES), 0)
    c = lax.broadcasted_iota(jnp.int32, (LANES, LANES), 1)
    tri = (r <= c).astype(F32)
    carry = jnp.zeros((FOX_HEADS, 1), F32)
    for j in range(S // LANES):
        blk = log_f[:, j * LANES:(j + 1) * LANES]
        cs = _dot(blk, tri, precision=lax.Precision.HIGHEST) + carry
        c_ref[0, :, j * LANES:(j + 1) * LANES] = cs
        carry = cs[:, LANES - 1:LANES]


def _fox_gate(small, fox_f_bias, B, S):
    return pl.pallas_call(
        _fox_gate_kernel,
        grid=(B,),
        in_specs=[pl.BlockSpec((S, LANES), lambda b: (b, 0)),
                  pl.BlockSpec((FOX_HEADS, 1), lambda b: (0, 0))],
        out_specs=pl.BlockSpec((1, FOX_HEADS, S), lambda b: (b, 0, 0)),
        out_shape=jax.ShapeDtypeStruct((B, FOX_HEADS, S), F32),
        compiler_params=pltpu.CompilerParams(dimension_semantics=("arbitrary",)),
        name="fox_gate",
    )(small, fox_f_bias.reshape(FOX_HEADS, 1))


FOX_TQ = 128
FOX_TK = 128


def _fox_attn_kernel(q_ref, k_ref, v_ref, c_ref, o_ref):
    qi = pl.program_id(2)
    tq, tk, dh = FOX_TQ, FOX_TK, FOX_HEAD_DIM
    q_pos = qi * tq + lax.broadcasted_iota(jnp.int32, (tq, tk), 0)
    k_off = lax.broadcasted_iota(jnp.int32, (tq, tk), 1)
    n_kv = (qi * tq) // tk + tq // tk
    for hh in range(LANES // dh):
        q = q_ref[:, hh * dh:(hh + 1) * dh]

        def body(j, carry):
            m, l, acc = carry
            ks = pl.ds(pl.multiple_of(j * tk, tk), tk)
            k = k_ref[ks, hh * dh:(hh + 1) * dh]
            v = v_ref[ks, hh * dh:(hh + 1) * dh]
            s = _dot_nt(q, k) * (dh ** -0.5) - c_ref[hh, :, ks]
            s = jnp.where(j * tk + k_off <= q_pos, s, -jnp.inf)
            m_new = jnp.maximum(m, jnp.max(s, axis=1, keepdims=True))
            p = jnp.exp(s - m_new)
            a = jnp.exp(m - m_new)
            l = a * l + jnp.sum(p, axis=1, keepdims=True)
            acc = a * acc + _dot(p.astype(BF16), v)
            return m_new, l, acc

        init = (jnp.full((tq, 1), -jnp.inf, F32), jnp.zeros((tq, 1), F32), jnp.zeros((tq, dh), F32))
        _, l, acc = lax.fori_loop(0, n_kv, body, init)
        o_ref[:, hh * dh:(hh + 1) * dh] = (acc / l).astype(BF16)


def _fox_attn(fqkv, c_t, B, S):
    T = B * S
    nq = S // FOX_TQ
    n_hp = FOX_WIDTH // LANES
    return pl.pallas_call(
        _fox_attn_kernel,
        grid=(B, n_hp, nq),
        in_specs=[pl.BlockSpec((FOX_TQ, LANES), lambda b, h, i: (b * nq + i, h)),
                  pl.BlockSpec((S, LANES), lambda b, h, i: (b, n_hp + h)),
                  pl.BlockSpec((S, LANES), lambda b, h, i: (b, 2 * n_hp + h)),
                  pl.BlockSpec((LANES // FOX_HEAD_DIM, 1, S), lambda b, h, i: (b * n_hp + h, 0, 0))],
        out_specs=pl.BlockSpec((FOX_TQ, LANES), lambda b, h, i: (b * nq + i, h)),
        out_shape=jax.ShapeDtypeStruct((T, FOX_WIDTH), BF16),
        compiler_params=pltpu.CompilerParams(dimension_semantics=("arbitrary", "arbitrary", "arbitrary")),
        name="fox_attn",
    )(fqkv, fqkv, fqkv, c_t.reshape(B * FOX_HEADS, 1, S))


def _gla_kernel(q_ref, k_ref, v_ref, s_ref, up_ref, gb_ref, gg_ref, gain_ref, o_ref):
    S = q_ref.shape[0]
    C = GLA_CHUNK
    r = lax.broadcasted_iota(jnp.int32, (C, C), 0)
    c = lax.broadcasted_iota(jnp.int32, (C, C), 1)
    tril = (r >= c).astype(F32)

    def body(ci, st_t):
        rows = pl.ds(pl.multiple_of(ci * C, C), C)
        z = _dot(s_ref[rows, :].astype(BF16), up_ref[...]) + gb_ref[...]
        la = jax.nn.log_sigmoid(z) * (1.0 / GLA_GATE_TEMP)
        cum = _dot(tril, la, precision=lax.Precision.HIGHEST)
        tot = cum[C - 1:C, :]
        kd = (k_ref[rows, :].astype(F32) * jnp.exp(tot - cum)).astype(BF16)
        st_t = st_t * jnp.exp(tot) + _dot_tn(v_ref[rows, :], kd)
        o = _dot_nt(q_ref[rows, :], st_t.astype(BF16)) * (GLA_KEY_DIM ** -0.5)
        o = o * lax.rsqrt(jnp.mean(o * o, axis=-1, keepdims=True) + GLA_NORM_EPS) * gain_ref[...]
        o = o * jax.nn.silu(gg_ref[rows, :])
        o_ref[rows, :] = o.astype(BF16)
        return st_t

    lax.fori_loop(0, S // C, body, jnp.zeros((GLA_VAL_DIM, GLA_KEY_DIM), F32))


def _gla(gqk, gv, small, gg, gla_gate_up, gla_gate_bias, gla_norm_gain, B, S):
    T = B * S
    up = jnp.zeros((LANES, GLA_QK_WIDTH), F32).at[GLR_COL:GLR_COL + GLA_GATE_RANK].set(gla_gate_up).astype(BF16)
    gb = gla_gate_bias.reshape(1, GLA_QK_WIDTH)
    gain = gla_norm_gain.reshape(1, GLA_V_WIDTH)
    H = GLA_HEADS
    return pl.pallas_call(
        _gla_kernel,
        grid=(B, H),
        in_specs=[pl.BlockSpec((S, GLA_KEY_DIM), lambda b, h: (b, h)),
                  pl.BlockSpec((S, GLA_KEY_DIM), lambda b, h: (b, H + h)),
                  pl.BlockSpec((S, GLA_VAL_DIM), lambda b, h: (b, h)),
                  pl.BlockSpec((S, LANES), lambda b, h: (b, 0)),
                  pl.BlockSpec((LANES, GLA_KEY_DIM), lambda b, h: (0, h)),
                  pl.BlockSpec((1, GLA_KEY_DIM), lambda b, h: (0, h)),
                  pl.BlockSpec((S, GLA_VAL_DIM), lambda b, h: (b, h)),
                  pl.BlockSpec((1, GLA_VAL_DIM), lambda b, h: (0, h))],
        out_specs=pl.BlockSpec((S, GLA_VAL_DIM), lambda b, h: (b, h)),
        out_shape=jax.ShapeDtypeStruct((T, GLA_V_WIDTH), BF16),
        compiler_params=pltpu.CompilerParams(dimension_semantics=("arbitrary", "arbitrary")),
        name="gla",
    )(gqk, gqk, gv, small, up, gb, gg, gain)


def _merge_kernel(a_ref, og_ref, m_ref, x_ref, wf_ref, wg_ref, wo_ref, g_ref, b_ref, o_ref):
    y_fox = _dot(a_ref[...], wf_ref[...])
    y_gla = _dot(og_ref[...], wg_ref[...])
    merged = (jax.nn.sigmoid(m_ref[:, :D_MODEL]) * y_fox + jax.nn.sigmoid(m_ref[:, D_MODEL:]) * y_gla)
    mix = _dot(merged.astype(BF16), wo_ref[...])
    o_ref[...] = _layer_norm(ALPHA * x_ref[...] + mix, g_ref[...], b_ref[...])


def _merge(attn, og, mg, x2, w_out_fox, w_out_gla, w_out, ln_gain, ln_bias):
    T = x2.shape[0]
    tm = 256
    row = lambda w: pl.BlockSpec((tm, w), lambda i: (i, 0))
    const = lambda a: pl.BlockSpec(a.shape, lambda i: (0, 0))
    ws = [w_out_fox.astype(BF16), w_out_gla.astype(BF16), w_out.astype(BF16),
          ln_gain.reshape(1, D_MODEL), ln_bias.reshape(1, D_MODEL)]
    return pl.pallas_call(
        _merge_kernel,
        grid=(T // tm,),
        in_specs=[row(FOX_WIDTH), row(GLA_V_WIDTH), row(2 * D_MODEL), row(D_MODEL)] + [const(w) for w in ws],
        out_specs=row(D_MODEL),
        out_shape=jax.ShapeDtypeStruct((T, D_MODEL), F32),
        compiler_params=pltpu.CompilerParams(dimension_semantics=("arbitrary",), vmem_limit_bytes=VMEM_LIMIT),
        name="merge",
    )(attn, og, mg, x2, *ws)


ROUTE_TM = 256
CAND_BLOCKS = 10


def _route_kernel(x_ref, wq_ref, keys_ref, idx_ref, gate_ref, q_scr, st_scr, it_scr, best_scr):
    tm = ROUTE_TM
    K = PEER_TOPK
    q_scr[...] = _dot(x_ref[...].astype(BF16), wq_ref[...])
    key_iota = lax.broadcasted_iota(jnp.int32, (PEER_N_KEYS, tm), 0)

    def stage1(hp, _):
        q = q_scr[:, pl.ds(pl.multiple_of(hp * PEER_HALF, PEER_HALF), PEER_HALF)].astype(BF16)
        s = _dot_nt(keys_ref[hp], q)
        for i in range(K):
            m = jnp.max(s, axis=0, keepdims=True)
            idx = jnp.min(jnp.where(s == m, key_iota, PEER_N_KEYS), axis=0, keepdims=True)
            st_scr[hp, i:i + 1, :] = m
            it_scr[hp, i:i + 1, :] = idx
            s = jnp.where(key_iota == idx, -jnp.inf, s)
        return 0

    lax.fori_loop(0, 2 * PEER_HEADS, stage1, 0)

    n_cand = CAND_BLOCKS * SUBLANES
    cand_iota = lax.broadcasted_iota(jnp.int32, (n_cand, tm), 0)

    def stage2(h, _):
        s0, s1 = st_scr[2 * h], st_scr[2 * h + 1]
        i0, i1 = it_scr[2 * h], it_scr[2 * h + 1]
        lo, hi = slice(0, SUBLANES), slice(SUBLANES, 2 * SUBLANES)
        cs = [s0[0:1] + s1[lo], s0[0:1] + s1[hi]]
        ci = [i0[0:1] * PEER_N_KEYS + i1[lo], i0[0:1] * PEER_N_KEYS + i1[hi]]
        for a in range(1, SUBLANES):
            cs.append(s0[a:a + 1] + s1[lo])
            ci.append(i0[a:a + 1] * PEER_N_KEYS + i1[lo])
        cs.append(s0[hi] + s1[0:1])
        ci.append(i0[hi] * PEER_N_KEYS + i1[0:1])
        cand = jnp.concatenate(cs, axis=0)
        cidx = jnp.concatenate(ci, axis=0)
        for i in range(K):
            m = jnp.max(cand, axis=0, keepdims=True)
            pos = jnp.min(jnp.where(cand == m, cand_iota, n_cand), axis=0, keepdims=True)
            sel = cand_iota == pos
            best_scr[i:i + 1, :] = m
            idx_ref[pl.ds(h * K + i, 1), :] = jnp.max(jnp.where(sel, cidx, -1), axis=0, keepdims=True)
            cand = jnp.where(sel, -jnp.inf, cand)
        best = best_scr[...]
        e = jnp.exp(best - best[0:1])
        gate_ref[pl.ds(pl.multiple_of(h * K, K), K), :] = e / jnp.sum(e, axis=0, keepdims=True)
        return 0

    lax.fori_loop(0, PEER_HEADS, stage2, 0)


def _route(x1, peer_w_query, peer_sub_keys):
    T = x1.shape[0]
    tm = ROUTE_TM
    wq = peer_w_query.reshape(D_MODEL, 2 * PEER_HEADS * PEER_HALF).astype(BF16)
    keys = peer_sub_keys.reshape(2 * PEER_HEADS, PEER_N_KEYS, PEER_HALF).astype(BF16)
    return pl.pallas_call(
        _route_kernel,
        grid=(T // tm,),
        in_specs=[pl.BlockSpec((tm, D_MODEL), lambda i: (i, 0)),
                  pl.BlockSpec(wq.shape, lambda i: (0, 0)),
                  pl.BlockSpec(keys.shape, lambda i: (0, 0, 0))],
        out_specs=[pl.BlockSpec((PEER_PICKS, tm), lambda i: (0, i)),
                   pl.BlockSpec((PEER_PICKS, tm), lambda i: (0, i))],
        out_shape=[jax.ShapeDtypeStruct((PEER_PICKS, T), jnp.int32),
                   jax.ShapeDtypeStruct((PEER_PICKS, T), F32)],
        scratch_shapes=[pltpu.VMEM((tm, 2 * PEER_HEADS * PEER_HALF), F32),
                        pltpu.VMEM((2 * PEER_HEADS, PEER_TOPK, tm), F32),
                        pltpu.VMEM((2 * PEER_HEADS, PEER_TOPK, tm), jnp.int32),
                        pltpu.VMEM((PEER_TOPK, tm), F32)],
        compiler_params=pltpu.CompilerParams(dimension_semantics=("arbitrary",), vmem_limit_bytes=VMEM_LIMIT),
        name="peer_route",
    )(x1, wq, keys)


MIX_TB = SUBLANES
MIX_ROWS = MIX_TB * PEER_PICKS


def _mix_kernel(idx0_ref, idxn_ref, h_ref, g_ref, u_hbm, v_hbm, lg_ref, lb_ref, o_ref,
                ubuf, vbuf, sem_u, sem_v):
    i = pl.program_id(0)
    n = pl.num_programs(0)
    slot = lax.rem(i, 2)
    nslot = 1 - slot
    tb = MIX_TB

    def row_copy(tab, buf, sem, idx_ref, r, s):
        return pltpu.make_async_copy(tab.at[pl.ds(idx_ref[0, 0, r], 1)], buf.at[s, pl.ds(r, 1)], sem.at[s])

    def start_rows(tab, buf, sem, idx_ref, k, s):
        for t in range(tb):
            row_copy(tab, buf, sem, idx_ref, k * tb + t, s).start()

    def wait_all(tab, buf, sem, s):
        pltpu.make_async_copy(tab.at[pl.ds(0, MIX_ROWS)], buf.at[s], sem.at[s]).wait()

    @pl.when(i == 0)
    def _():
        def first(k, _):
            start_rows(u_hbm, ubuf, sem_u, idx0_ref, k, 0)
            start_rows(v_hbm, vbuf, sem_v, idx0_ref, k, 0)
            return 0
        lax.fori_loop(0, PEER_PICKS, first, 0)

    has_next = i + 1 < n
    h = h_ref[...]
    lane = lax.broadcasted_iota(jnp.int32, (tb, PEER_PICKS), 1)

    wait_all(u_hbm, ubuf, sem_u, slot)

    def u_step(k, act):
        @pl.when(has_next)
        def _():
            start_rows(u_hbm, ubuf, sem_u, idxn_ref, k, nslot)
        rows = ubuf[slot, pl.ds(pl.multiple_of(k * tb, tb), tb), :]
        a = jnp.sum(rows * h, axis=1, keepdims=True)
        return jnp.where(lane == k, a, act)

    act = lax.fori_loop(0, PEER_PICKS, u_step, jnp.zeros((tb, PEER_PICKS), F32))
    w = 0.5 * act * (1.0 + lax.erf(act * (2.0 ** -0.5))) * g_ref[...]

    wait_all(v_hbm, vbuf, sem_v, slot)

    def v_step(k, acc):
        @pl.when(has_next)
        def _():
            start_rows(v_hbm, vbuf, sem_v, idxn_ref, k, nslot)
        rows = vbuf[slot, pl.ds(pl.multiple_of(k * tb, tb), tb), :]
        wk = jnp.sum(jnp.where(lane == k, w, 0.0), axis=1, keepdims=True)
        return acc + wk * rows

    ffn = lax.fori_loop(0, PEER_PICKS, v_step, jnp.zeros((tb, D_MODEL), F32))
    o_ref[...] = _layer_norm(ALPHA * h + ffn, lg_ref[...], lb_ref[...])


def _mix(x1, idx_t, gate_t, expert_u, expert_v, ln_gain, ln_bias):
    T = x1.shape[0]
    tb = MIX_TB
    nb = T // tb
    idx_r = idx_t.reshape(PEER_PICKS, nb, tb).transpose(1, 0, 2).reshape(nb, 1, MIX_ROWS)
    gates = gate_t.T
    smem = lambda imap: pl.BlockSpec((1, 1, MIX_ROWS), imap, memory_space=pltpu.SMEM)
    return pl.pallas_call(
        _mix_kernel,
        grid=(nb,),
        in_specs=[smem(lambda i: (0, 0, 0)),
                  smem(lambda i: (jnp.minimum(i + 1, nb - 1), 0, 0)),
                  pl.BlockSpec((tb, D_MODEL), lambda i: (i, 0)),
                  pl.BlockSpec((tb, PEER_PICKS), lambda i: (i, 0)),
                  pl.BlockSpec(memory_space=pl.ANY),
                  pl.BlockSpec(memory_space=pl.ANY),
                  pl.BlockSpec((1, D_MODEL), lambda i: (0, 0)),
                  pl.BlockSpec((1, D_MODEL), lambda i: (0, 0))],
        out_specs=pl.BlockSpec((tb, D_MODEL), lambda i: (i, 0)),
        out_shape=jax.ShapeDtypeStruct((T, D_MODEL), F32),
        scratch_shapes=[pltpu.VMEM((2, MIX_ROWS, D_MODEL), F32),
                        pltpu.VMEM((2, MIX_ROWS, D_MODEL), F32),
                        pltpu.SemaphoreType.DMA((2,)),
                        pltpu.SemaphoreType.DMA((2,))],
        compiler_params=pltpu.CompilerParams(dimension_semantics=("arbitrary",), vmem_limit_bytes=VMEM_LIMIT),
        name="peer_mix",
    )(idx_r, idx_r, x1, gates, expert_u, expert_v, ln_gain.reshape(1, D_MODEL), ln_bias.reshape(1, D_MODEL))


def kernel(x, w_in, fox_f_bias, gla_gate_up, gla_gate_bias, gla_norm_gain, w_out_fox, w_out_gla, w_out,
           ln1_gain, ln1_bias, peer_w_query, peer_sub_keys, peer_expert_u, peer_expert_v, ln2_gain, ln2_bias):
    B, S, D = x.shape
    assert D == D_MODEL and S % max(ROUTE_TM, FOX_TQ) == 0 and w_in.shape[0] == DEPTH
    h = x.reshape(B * S, D)
    for l in range(DEPTH):
        fqkv, gqk, gv, gg, mg, small = _in_proj(h, w_in[l])
        c_t = _fox_gate(small, fox_f_bias[l], B, S)
        attn = _fox_attn(fqkv, c_t, B, S)
        og = _gla(gqk, gv, small, gg, gla_gate_up[l], gla_gate_bias[l], gla_norm_gain[l], B, S)
        h = _merge(attn, og, mg, h, w_out_fox[l], w_out_gla[l], w_out[l], ln1_gain[l], ln1_bias[l])
        idx_t, gate_t = _route(h, peer_w_query[l], peer_sub_keys[l])
        h = _mix(h, idx_t, gate_t, peer_expert_u[l], peer_expert_v[l], ln2_gain[l], ln2_bias[l])
    return h.reshape(B, S, D)
```

```python
import functools
import math

import jax
import jax.numpy as jnp
from jax import lax
from jax.experimental import pallas as pl
from jax.experimental.pallas import tpu as pltpu
from jax.experimental.pallas import tpu_sc as plsc

F32 = jnp.float32
BF16 = jnp.bfloat16

D_MODEL = 1024
FOX_HEADS = 8
FOX_HEAD_DIM = 64
FOX_WIDTH = FOX_HEADS * FOX_HEAD_DIM
GLA_HEADS = 4
GLA_KEY_DIM = 128
GLA_VAL_DIM = 256
GLA_QK_WIDTH = GLA_HEADS * GLA_KEY_DIM
GLA_V_WIDTH = GLA_HEADS * GLA_VAL_DIM
GLA_GATE_RANK = 16
GLA_GATE_TEMP = 16.0
GLA_NORM_EPS = 1e-5
GLA_CHUNK = 64
PEER_HEADS = 8
PEER_N_KEYS = 128
PEER_HALF = 128
PEER_TOPK = 16
PEER_PICKS = PEER_HEADS * PEER_TOPK
DEPTH = 1
ALPHA = (2.0 * DEPTH) ** 0.25
LN_EPS = 1e-5

LANES = 128
SUBLANES = 8
VMEM_LIMIT = 52 * 1024 * 1024

IN_SPLIT_SIZES = (FOX_WIDTH, FOX_WIDTH, FOX_WIDTH, FOX_HEADS,
                  GLA_QK_WIDTH, GLA_QK_WIDTH, GLA_V_WIDTH, GLA_V_WIDTH, GLA_GATE_RANK,
                  D_MODEL, D_MODEL)
FF_COL = 0
GLR_COL = FOX_HEADS


def _dot(a, b, **kw):
    return jnp.dot(a, b, preferred_element_type=F32, **kw)


def _dot_nt(a, b):
    return lax.dot_general(a, b, (((1,), (1,)), ((), ())), preferred_element_type=F32)


def _dot_tn(a, b):
    return lax.dot_general(a, b, (((0,), (0,)), ((), ())), preferred_element_type=F32)


def _layer_norm(y, gain, bias):
    mu = jnp.mean(y, axis=-1, keepdims=True)
    yc = y - mu
    var = jnp.mean(yc * yc, axis=-1, keepdims=True)
    return yc * lax.rsqrt(var + LN_EPS) * gain + bias


def _in_proj_kernel(x_ref, wf_ref, wgqk_ref, wgv_ref, wgg_ref, wm_ref, ws_ref,
                    f_ref, gqk_ref, gv_ref, gg_ref, m_ref, s_ref):
    xb = x_ref[...].astype(BF16)
    f_ref[...] = _dot(xb, wf_ref[...]).astype(BF16)
    gqk_ref[...] = _dot(xb, wgqk_ref[...]).astype(BF16)
    gv_ref[...] = _dot(xb, wgv_ref[...]).astype(BF16)
    gg_ref[...] = _dot(xb, wgg_ref[...])
    m_ref[...] = _dot(xb, wm_ref[...])
    s_ref[...] = _dot(xb, ws_ref[...])


def _in_proj(x2, w_in):
    T = x2.shape[0]
    tm = 256
    pts = [0]
    for s in IN_SPLIT_SIZES:
        pts.append(pts[-1] + s)
    col = lambda i, j: w_in[:, pts[i]:pts[j]]
    wf = col(0, 3).astype(BF16)
    wgqk = col(4, 6).astype(BF16)
    wgv = col(6, 7).astype(BF16)
    wgg = col(7, 8).astype(BF16)
    wm = col(9, 11).astype(BF16)
    ws = jnp.concatenate([col(3, 4), col(8, 9)], axis=1)
    ws = jnp.pad(ws, ((0, 0), (0, LANES - ws.shape[1]))).astype(BF16)
    ws_list = [wf, wgqk, wgv, wgg, wm, ws]
    out_dtypes = [BF16, BF16, BF16, F32, F32, F32]
    const = lambda w: pl.BlockSpec(w.shape, lambda i: (0, 0))
    return pl.pallas_call(
        _in_proj_kernel,
        grid=(T // tm,),
        in_specs=[pl.BlockSpec((tm, D_MODEL), lambda i: (i, 0))] + [const(w) for w in ws_list],
        out_specs=[pl.BlockSpec((tm, w.shape[1]), lambda i: (i, 0)) for w in ws_list],
        out_shape=[jax.ShapeDtypeStruct((T, w.shape[1]), dt) for w, dt in zip(ws_list, out_dtypes)],
        compiler_params=pltpu.CompilerParams(dimension_semantics=("arbitrary",), vmem_limit_bytes=VMEM_LIMIT),
        name="in_proj",
    )(x2, *ws_list)


def _fox_gate_kernel(s_ref, bias_ref, c_ref):
    S = s_ref.shape[0]
    ff_t = s_ref[...].T[FF_COL:FF_COL + FOX_HEADS, :]
    log_f = jax.nn.log_sigmoid(ff_t + bias_ref[...])
    r = lax.broadcasted_iota(jnp.int32, (LANES, LANES), 0)
    c = lax.broadcasted_iota(jnp.int32, (LANES, LANES), 1)
    tri = (r <= c).astype(F32)
    carry = jnp.zeros((FOX_HEADS, 1), F32)
    for j in range(S // LANES):
        blk = log_f[:, j * LANES:(j + 1) * LANES]
        cs = _dot(blk, tri, precision=lax.Precision.HIGHEST) + carry
        c_ref[0, :, j * LANES:(j + 1) * LANES] = cs
        carry = cs[:, LANES - 1:LANES]


def _fox_gate(small, fox_f_bias, B, S):
    return pl.pallas_call(
        _fox_gate_kernel,
        grid=(B,),
        in_specs=[pl.BlockSpec((S, LANES), lambda b: (b, 0)),
                  pl.BlockSpec((FOX_HEADS, 1), lambda b: (0, 0))],
        out_specs=pl.BlockSpec((1, FOX_HEADS, S), lambda b: (b, 0, 0)),
        out_shape=jax.ShapeDtypeStruct((B, FOX_HEADS, S), F32),
        compiler_params=pltpu.CompilerParams(dimension_semantics=("arbitrary",)),
        name="fox_gate",
    )(small, fox_f_bias.reshape(FOX_HEADS, 1))


FOX_TQ = 128
FOX_TK = 128


def _fox_attn_kernel(q_ref, k_ref, v_ref, c_ref, o_ref):
    qi = pl.program_id(2)
    tq, tk, dh = FOX_TQ, FOX_TK, FOX_HEAD_DIM
    q_pos = qi * tq + lax.broadcasted_iota(jnp.int32, (tq, tk), 0)
    k_off = lax.broadcasted_iota(jnp.int32, (tq, tk), 1)
    n_kv = (qi * tq) // tk + tq // tk
    for hh in range(LANES // dh):
        q = q_ref[:, hh * dh:(hh + 1) * dh]

        def body(j, carry):
            m, l, acc = carry
            ks = pl.ds(pl.multiple_of(j * tk, tk), tk)
            k = k_ref[ks, hh * dh:(hh + 1) * dh]
            v = v_ref[ks, hh * dh:(hh + 1) * dh]
            s = _dot_nt(q, k) * (dh ** -0.5) - c_ref[hh, :, ks]
            s = jnp.where(j * tk + k_off <= q_pos, s, -jnp.inf)
            m_new = jnp.maximum(m, jnp.max(s, axis=1, keepdims=True))
            p = jnp.exp(s - m_new)
            a = jnp.exp(m - m_new)
            l = a * l + jnp.sum(p, axis=1, keepdims=True)
            acc = a * acc + _dot(p.astype(BF16), v)
            return m_new, l, acc

        init = (jnp.full((tq, 1), -jnp.inf, F32), jnp.zeros((tq, 1), F32), jnp.zeros((tq, dh), F32))
        _, l, acc = lax.fori_loop(0, n_kv, body, init)
        o_ref[:, hh * dh:(hh + 1) * dh] = (acc / l).astype(BF16)


def _fox_attn(fqkv, c_t, B, S):
    T = B * S
    nq = S // FOX_TQ
    n_hp = FOX_WIDTH // LANES
    return pl.pallas_call(
        _fox_attn_kernel,
        grid=(B, n_hp, nq),
        in_specs=[pl.BlockSpec((FOX_TQ, LANES), lambda b, h, i: (b * nq + i, h)),
                  pl.BlockSpec((S, LANES), lambda b, h, i: (b, n_hp + h)),
                  pl.BlockSpec((S, LANES), lambda b, h, i: (b, 2 * n_hp + h)),
                  pl.BlockSpec((LANES // FOX_HEAD_DIM, 1, S), lambda b, h, i: (b * n_hp + h, 0, 0))],
        out_specs=pl.BlockSpec((FOX_TQ, LANES), lambda b, h, i: (b * nq + i, h)),
        out_shape=jax.ShapeDtypeStruct((T, FOX_WIDTH), BF16),
        compiler_params=pltpu.CompilerParams(dimension_semantics=("arbitrary", "arbitrary", "arbitrary")),
        name="fox_attn",
    )(fqkv, fqkv, fqkv, c_t.reshape(B * FOX_HEADS, 1, S))


def _gla_kernel(q_ref, k_ref, v_ref, s_ref, up_ref, gb_ref, gg_ref, gain_ref, o_ref):
    S = q_ref.shape[0]
    C = GLA_CHUNK
    r = lax.broadcasted_iota(jnp.int32, (C, C), 0)
    c = lax.broadcasted_iota(jnp.int32, (C, C), 1)
    tril = (r >= c).astype(F32)

    def body(ci, st_t):
        rows = pl.ds(pl.multiple_of(ci * C, C), C)
        z = _dot(s_ref[rows, :].astype(BF16), up_ref[...]) + gb_ref[...]
        la = jax.nn.log_sigmoid(z) * (1.0 / GLA_GATE_TEMP)
        cum = _dot(tril, la, precision=lax.Precision.HIGHEST)
        tot = cum[C - 1:C, :]
        kd = (k_ref[rows, :].astype(F32) * jnp.exp(tot - cum)).astype(BF16)
        st_t = st_t * jnp.exp(tot) + _dot_tn(v_ref[rows, :], kd)
        o = _dot_nt(q_ref[rows, :], st_t.astype(BF16)) * (GLA_KEY_DIM ** -0.5)
        o = o * lax.rsqrt(jnp.mean(o * o, axis=-1, keepdims=True) + GLA_NORM_EPS) * gain_ref[...]
        o = o * jax.nn.silu(gg_ref[rows, :])
        o_ref[rows, :] = o.astype(BF16)
        return st_t

    lax.fori_loop(0, S // C, body, jnp.zeros((GLA_VAL_DIM, GLA_KEY_DIM), F32))


def _gla(gqk, gv, small, gg, gla_gate_up, gla_gate_bias, gla_norm_gain, B, S):
    T = B * S
    up = jnp.zeros((LANES, GLA_QK_WIDTH), F32).at[GLR_COL:GLR_COL + GLA_GATE_RANK].set(gla_gate_up).astype(BF16)
    gb = gla_gate_bias.reshape(1, GLA_QK_WIDTH)
    gain = gla_norm_gain.reshape(1, GLA_V_WIDTH)
    H = GLA_HEADS
    return pl.pallas_call(
        _gla_kernel,
        grid=(B, H),
        in_specs=[pl.BlockSpec((S, GLA_KEY_DIM), lambda b, h: (b, h)),
                  pl.BlockSpec((S, GLA_KEY_DIM), lambda b, h: (b, H + h)),
                  pl.BlockSpec((S, GLA_VAL_DIM), lambda b, h: (b, h)),
                  pl.BlockSpec((S, LANES), lambda b, h: (b, 0)),
                  pl.BlockSpec((LANES, GLA_KEY_DIM), lambda b, h: (0, h)),
                  pl.BlockSpec((1, GLA_KEY_DIM), lambda b, h: (0, h)),
                  pl.BlockSpec((S, GLA_VAL_DIM), lambda b, h: (b, h)),
                  pl.BlockSpec((1, GLA_VAL_DIM), lambda b, h: (0, h))],
        out_specs=pl.BlockSpec((S, GLA_VAL_DIM), lambda b, h: (b, h)),
        out_shape=jax.ShapeDtypeStruct((T, GLA_V_WIDTH), BF16),
        compiler_params=pltpu.CompilerParams(dimension_semantics=("arbitrary", "arbitrary")),
        name="gla",
    )(gqk, gqk, gv, small, up, gb, gg, gain)


def _merge_kernel(a_ref, og_ref, m_ref, x_ref, wf_ref, wg_ref, wo_ref, g_ref, b_ref, o_ref):
    y_fox = _dot(a_ref[...], wf_ref[...])
    y_gla = _dot(og_ref[...], wg_ref[...])
    merged = (jax.nn.sigmoid(m_ref[:, :D_MODEL]) * y_fox + jax.nn.sigmoid(m_ref[:, D_MODEL:]) * y_gla)
    mix = _dot(merged.astype(BF16), wo_ref[...])
    o_ref[...] = _layer_norm(ALPHA * x_ref[...] + mix, g_ref[...], b_ref[...])


def _merge(attn, og, mg, x2, w_out_fox, w_out_gla, w_out, ln_gain, ln_bias):
    T = x2.shape[0]
    tm = 256
    row = lambda w: pl.BlockSpec((tm, w), lambda i: (i, 0))
    const = lambda a: pl.BlockSpec(a.shape, lambda i: (0, 0))
    ws = [w_out_fox.astype(BF16), w_out_gla.astype(BF16), w_out.astype(BF16),
          ln_gain.reshape(1, D_MODEL), ln_bias.reshape(1, D_MODEL)]
    return pl.pallas_call(
        _merge_kernel,
        grid=(T // tm,),
        in_specs=[row(FOX_WIDTH), row(GLA_V_WIDTH), row(2 * D_MODEL), row(D_MODEL)] + [const(w) for w in ws],
        out_specs=row(D_MODEL),
        out_shape=jax.ShapeDtypeStruct((T, D_MODEL), F32),
        compiler_params=pltpu.CompilerParams(dimension_semantics=("arbitrary",), vmem_limit_bytes=VMEM_LIMIT),
        name="merge",
    )(attn, og, mg, x2, *ws)


ROUTE_TM = 256
CAND_BLOCKS = 10


def _route_kernel(x_ref, wq_ref, keys_ref, idx_ref, gate_ref, q_scr, st_scr, it_scr, best_scr):
    tm = ROUTE_TM
    K = PEER_TOPK
    q_scr[...] = _dot(x_ref[...].astype(BF16), wq_ref[...])
    key_iota = lax.broadcasted_iota(jnp.int32, (PEER_N_KEYS, tm), 0)

    def stage1(hp, _):
        q = q_scr[:, pl.ds(pl.multiple_of(hp * PEER_HALF, PEER_HALF), PEER_HALF)].astype(BF16)
        s = _dot_nt(keys_ref[hp], q)
        for i in range(K):
            m = jnp.max(s, axis=0, keepdims=True)
            idx = jnp.min(jnp.where(s == m, key_iota, PEER_N_KEYS), axis=0, keepdims=True)
            st_scr[hp, i:i + 1, :] = m
            it_scr[hp, i:i + 1, :] = idx
            s = jnp.where(key_iota == idx, -jnp.inf, s)
        return 0

    lax.fori_loop(0, 2 * PEER_HEADS, stage1, 0)

    n_cand = CAND_BLOCKS * SUBLANES
    cand_iota = lax.broadcasted_iota(jnp.int32, (n_cand, tm), 0)

    def stage2(h, _):
        s0, s1 = st_scr[2 * h], st_scr[2 * h + 1]
        i0, i1 = it_scr[2 * h], it_scr[2 * h + 1]
        lo, hi = slice(0, SUBLANES), slice(SUBLANES, 2 * SUBLANES)
        cs = [s0[0:1] + s1[lo], s0[0:1] + s1[hi]]
        ci = [i0[0:1] * PEER_N_KEYS + i1[lo], i0[0:1] * PEER_N_KEYS + i1[hi]]
        for a in range(1, SUBLANES):
            cs.append(s0[a:a + 1] + s1[lo])
            ci.append(i0[a:a + 1] * PEER_N_KEYS + i1[lo])
        cs.append(s0[hi] + s1[0:1])
        ci.append(i0[hi] * PEER_N_KEYS + i1[0:1])
        cand = jnp.concatenate(cs, axis=0)
        cidx = jnp.concatenate(ci, axis=0)
        for i in range(K):
            m = jnp.max(cand, axis=0, keepdims=True)
            pos = jnp.min(jnp.where(cand == m, cand_iota, n_cand), axis=0, keepdims=True)
            sel = cand_iota == pos
            best_scr[i:i + 1, :] = m
            idx_ref[pl.ds(h * K + i, 1), :] = jnp.max(jnp.where(sel, cidx, -1), axis=0, keepdims=True)
            cand = jnp.where(sel, -jnp.inf, cand)
        best = best_scr[...]
        e = jnp.exp(best - best[0:1])
        gate_ref[pl.ds(pl.multiple_of(h * K, K), K), :] = e / jnp.sum(e, axis=0, keepdims=True)
        return 0

    lax.fori_loop(0, PEER_HEADS, stage2, 0)


def _route(x1, peer_w_query, peer_sub_keys):
    T = x1.shape[0]
    tm = ROUTE_TM
    wq = peer_w_query.reshape(D_MODEL, 2 * PEER_HEADS * PEER_HALF).astype(BF16)
    keys = peer_sub_keys.reshape(2 * PEER_HEADS, PEER_N_KEYS, PEER_HALF).astype(BF16)
    return pl.pallas_call(
        _route_kernel,
        grid=(T // tm,),
        in_specs=[pl.BlockSpec((tm, D_MODEL), lambda i: (i, 0)),
                  pl.BlockSpec(wq.shape, lambda i: (0, 0)),
                  pl.BlockSpec(keys.shape, lambda i: (0, 0, 0))],
        out_specs=[pl.BlockSpec((PEER_PICKS, tm), lambda i: (0, i)),
                   pl.BlockSpec((PEER_PICKS, tm), lambda i: (0, i))],
        out_shape=[jax.ShapeDtypeStruct((PEER_PICKS, T), jnp.int32),
                   jax.ShapeDtypeStruct((PEER_PICKS, T), F32)],
        scratch_shapes=[pltpu.VMEM((tm, 2 * PEER_HEADS * PEER_HALF), F32),
                        pltpu.VMEM((2 * PEER_HEADS, PEER_TOPK, tm), F32),
                        pltpu.VMEM((2 * PEER_HEADS, PEER_TOPK, tm), jnp.int32),
                        pltpu.VMEM((PEER_TOPK, tm), F32)],
        compiler_params=pltpu.CompilerParams(dimension_semantics=("arbitrary",), vmem_limit_bytes=VMEM_LIMIT),
        name="peer_route",
    )(x1, wq, keys)


SC_LANES = 16
SC_ROWS = 32
SC_CHUNKS = PEER_PICKS // SC_ROWS
SC_DCOLS = 256
SC_DVREGS = SC_DCOLS // SC_LANES


def _sc_mesh():
    info = plsc.get_sparse_core_info()
    mesh = plsc.VectorSubcoreMesh(core_axis_name="c", subcore_axis_name="s")
    return mesh, info.num_cores, info.num_cores * info.num_subcores


def _sc_token_pipeline(tab_hbm, idx_hbm, vec_hbm, out_hbm, idx_v, vec_v, rows_v, out_v,
                       sem_rows, sem_tok, sem_out, n_cores, tpw, compute_chunk):
    wid = lax.axis_index("s") * n_cores + lax.axis_index("c")
    base = wid * tpw

    def gather(s, c):
        return pltpu.make_async_copy(tab_hbm.at[idx_v.at[s, pl.ds(c * SC_ROWS, SC_ROWS)]],
                                     rows_v.at[c % 2], sem_rows.at[c % 2])

    def tok_fetch(s, tok):
        return (pltpu.make_async_copy(idx_hbm.at[tok], idx_v.at[s], sem_tok.at[0]),
                pltpu.make_async_copy(vec_hbm.at[tok], vec_v.at[s], sem_tok.at[1]))

    def out_copy(s, tok):
        return pltpu.make_async_copy(out_v.at[s], out_hbm.at[tok], sem_out.at[s])

    for d in tok_fetch(0, base):
        d.start()
    for d in tok_fetch(0, base):
        d.wait()
    gather(0, 0).start()

    def pair_body(tp, _):
        for s in range(2):
            t = tp * 2 + s
            tok = base + t
            has_next = t + 1 < tpw

            @pl.when(has_next)
            def _():
                for d in tok_fetch(1 - s, tok + 1):
                    d.start()

            @pl.when(t >= 2)
            def _():
                out_copy(s, tok - 2).wait()

            for c in range(SC_CHUNKS):
                if c + 1 < SC_CHUNKS:
                    gather(s, c + 1).start()
                else:
                    @pl.when(has_next)
                    def _():
                        for d in tok_fetch(1 - s, tok + 1):
                            d.wait()
                        gather(1 - s, 0).start()
                gather(s, c).wait()
                compute_chunk(s, c)
            out_copy(s, tok).start()
        return 0

    lax.fori_loop(0, tpw // 2, pair_body, 0)
    for s in range(2):
        out_copy(s, base + tpw - 2 + s).wait()


def _sc_scratch(vec_len, out_len):
    return [pltpu.VMEM((2, PEER_PICKS), jnp.int32),
            pltpu.VMEM((2, vec_len), F32),
            pltpu.VMEM((2, SC_ROWS, D_MODEL), F32),
            pltpu.VMEM((2, out_len), F32),
            pltpu.SemaphoreType.DMA((2,)),
            pltpu.SemaphoreType.DMA((2,)),
            pltpu.SemaphoreType.DMA((2,))]


def _sc_dot(expert_u, idx, h):
    T = h.shape[0]
    mesh, n_cores, n_workers = _sc_mesh()
    tpw = T // n_workers
    L = SC_LANES

    @functools.partial(
        pl.kernel, mesh=mesh, out_type=jax.ShapeDtypeStruct((T, PEER_PICKS), F32),
        scratch_types=_sc_scratch(D_MODEL, PEER_PICKS) + [pltpu.VMEM((SC_ROWS, L), F32)],
        compiler_params=pltpu.CompilerParams(needs_layout_passes=False), name="peer_sc_dot")
    def k(u_hbm, idx_hbm, h_hbm, act_hbm, idx_v, h_v, rows_v, act_v, sem_rows, sem_tok, sem_out, part_v):
        lane = lax.iota(jnp.int32, L)

        def compute_chunk(s, c):
            b = c % 2
            for dc in range(D_MODEL // SC_DCOLS):
                hs = [h_v[s, pl.ds(dc * SC_DCOLS + j * L, L)] for j in range(SC_DVREGS)]

                def row_body(r, _, dc=dc, hs=hs):
                    acc = rows_v[b, r, pl.ds(dc * SC_DCOLS, L)] * hs[0]
                    for j in range(1, SC_DVREGS):
                        acc = acc + rows_v[b, r, pl.ds(dc * SC_DCOLS + j * L, L)] * hs[j]
                    if dc == 0:
                        part_v[r, :] = acc
                    else:
                        part_v[r, :] = part_v[r, :] + acc
                    return 0

                lax.fori_loop(0, SC_ROWS, row_body, 0)
            for g in range(SC_ROWS // L):
                tot = jnp.zeros((L,), F32)
                for rr in range(L):
                    tot = jnp.where(lane == rr, jnp.sum(part_v[g * L + rr, :]), tot)
                act_v[s, pl.ds(c * SC_ROWS + g * L, L)] = tot

        _sc_token_pipeline(u_hbm, idx_hbm, h_hbm, act_hbm, idx_v, h_v, rows_v, act_v,
                           sem_rows, sem_tok, sem_out, n_cores, tpw, compute_chunk)

    return k(expert_u, idx, h)


def _sc_acc(expert_v, idx, w):
    T = w.shape[0]
    mesh, n_cores, n_workers = _sc_mesh()
    tpw = T // n_workers
    L = SC_LANES

    @functools.partial(
        pl.kernel, mesh=mesh, out_type=jax.ShapeDtypeStruct((T, D_MODEL), F32),
        scratch_types=_sc_scratch(PEER_PICKS, D_MODEL),
        compiler_params=pltpu.CompilerParams(needs_layout_passes=False), name="peer_sc_acc")
    def k(v_hbm, idx_hbm, w_hbm, out_hbm, idx_v, w_v, rows_v, out_v, sem_rows, sem_tok, sem_out):
        def compute_chunk(s, c):
            b = c % 2
            for dc in range(D_MODEL // SC_DCOLS):
                cols = [pl.ds(dc * SC_DCOLS + j * L, L) for j in range(SC_DVREGS)]
                if c == 0:
                    accs = tuple(jnp.zeros((L,), F32) for _ in cols)
                else:
                    accs = tuple(out_v[s, cs] for cs in cols)

                def row_body(r, accs, cols=cols):
                    wb = plsc.load_gather(w_v.at[s], [jnp.full((L,), c * SC_ROWS, jnp.int32) + r])
                    return tuple(a + wb * rows_v[b, r, cs] for a, cs in zip(accs, cols))

                accs = lax.fori_loop(0, SC_ROWS, row_body, accs)
                for a, cs in zip(accs, cols):
                    out_v[s, cs] = a

        _sc_token_pipeline(v_hbm, idx_hbm, w_hbm, out_hbm, idx_v, w_v, rows_v, out_v,
                           sem_rows, sem_tok, sem_out, n_cores, tpw, compute_chunk)

    return k(expert_v, idx, w)


def _gelu_gate_kernel(a_ref, g_ref, o_ref):
    a = a_ref[...]
    o_ref[...] = 0.5 * a * (1.0 + lax.erf(a * (2.0 ** -0.5))) * g_ref[...]


def _ln2_kernel(h_ref, f_ref, g_ref, b_ref, o_ref):
    o_ref[...] = _layer_norm(ALPHA * h_ref[...] + f_ref[...], g_ref[...], b_ref[...])


def _mix(x1, idx_t, gate_t, expert_u, expert_v, ln_gain, ln_bias):
    T = x1.shape[0]
    tm = 512
    idx = idx_t.T
    gates = gate_t.T
    row = lambda w: pl.BlockSpec((tm, w), lambda i: (i, 0))
    const = pl.BlockSpec((1, D_MODEL), lambda i: (0, 0))
    params = pltpu.CompilerParams(dimension_semantics=("arbitrary",))
    act = _sc_dot(expert_u, idx, x1)
    w = pl.pallas_call(
        _gelu_gate_kernel, grid=(T // tm,), in_specs=[row(PEER_PICKS), row(PEER_PICKS)], out_specs=row(PEER_PICKS),
        out_shape=jax.ShapeDtypeStruct((T, PEER_PICKS), F32), compiler_params=params, name="peer_gelu_gate",
    )(act, gates)
    ffn = _sc_acc(expert_v, idx, w)
    return pl.pallas_call(
        _ln2_kernel, grid=(T // tm,), in_specs=[row(D_MODEL), row(D_MODEL), const, const], out_specs=row(D_MODEL),
        out_shape=jax.ShapeDtypeStruct((T, D_MODEL), F32), compiler_params=params, name="peer_ln2",
    )(x1, ffn, ln_gain.reshape(1, D_MODEL), ln_bias.reshape(1, D_MODEL))


def kernel(x, w_in, fox_f_bias, gla_gate_up, gla_gate_bias, gla_norm_gain, w_out_fox, w_out_gla, w_out,
           ln1_gain, ln1_bias, peer_w_query, peer_sub_keys, peer_expert_u, peer_expert_v, ln2_gain, ln2_bias):
    B, S, D = x.shape
    assert D == D_MODEL and S % max(ROUTE_TM, FOX_TQ) == 0 and w_in.shape[0] == DEPTH
    h = x.reshape(B * S, D)
    for l in range(DEPTH):
        fqkv, gqk, gv, gg, mg, small = _in_proj(h, w_in[l])
        c_t = _fox_gate(small, fox_f_bias[l], B, S)
        attn = _fox_attn(fqkv, c_t, B, S)
        og = _gla(gqk, gv, small, gg, gla_gate_up[l], gla_gate_bias[l], gla_norm_gain[l], B, S)
        h = _merge(attn, og, mg, h, w_out_fox[l], w_out_gla[l], w_out[l], ln1_gain[l], ln1_bias[l])
        idx_t, gate_t = _route(h, peer_w_query[l], peer_sub_keys[l])
        h = _mix(h, idx_t, gate_t, peer_expert_u[l], peer_expert_v[l], ln2_gain[l], ln2_bias[l])
    return h.reshape(B, S, D)
```

```python
import functools
import math

import jax
import jax.numpy as jnp
from jax import lax
from jax.experimental import pallas as pl
from jax.experimental.pallas import tpu as pltpu
from jax.experimental.pallas import tpu_sc as plsc

F32 = jnp.float32
BF16 = jnp.bfloat16

D_MODEL = 1024
FOX_HEADS = 8
FOX_HEAD_DIM = 64
FOX_WIDTH = FOX_HEADS * FOX_HEAD_DIM
GLA_HEADS = 4
GLA_KEY_DIM = 128
GLA_VAL_DIM = 256
GLA_QK_WIDTH = GLA_HEADS * GLA_KEY_DIM
GLA_V_WIDTH = GLA_HEADS * GLA_VAL_DIM
GLA_GATE_RANK = 16
GLA_GATE_TEMP = 16.0
GLA_NORM_EPS = 1e-5
GLA_CHUNK = 64
PEER_HEADS = 8
PEER_N_KEYS = 128
PEER_HALF = 128
PEER_TOPK = 16
PEER_PICKS = PEER_HEADS * PEER_TOPK
DEPTH = 1
ALPHA = (2.0 * DEPTH) ** 0.25
LN_EPS = 1e-5

LANES = 128
SUBLANES = 8
VMEM_LIMIT = 52 * 1024 * 1024

IN_SPLIT_SIZES = (FOX_WIDTH, FOX_WIDTH, FOX_WIDTH, FOX_HEADS,
                  GLA_QK_WIDTH, GLA_QK_WIDTH, GLA_V_WIDTH, GLA_V_WIDTH, GLA_GATE_RANK,
                  D_MODEL, D_MODEL)
FF_COL = 0
GLR_COL = FOX_HEADS


def _dot(a, b, **kw):
    return jnp.dot(a, b, preferred_element_type=F32, **kw)


def _dot_nt(a, b):
    return lax.dot_general(a, b, (((1,), (1,)), ((), ())), preferred_element_type=F32)


def _dot_tn(a, b):
    return lax.dot_general(a, b, (((0,), (0,)), ((), ())), preferred_element_type=F32)


def _layer_norm(y, gain, bias):
    mu = jnp.mean(y, axis=-1, keepdims=True)
    yc = y - mu
    var = jnp.mean(yc * yc, axis=-1, keepdims=True)
    return yc * lax.rsqrt(var + LN_EPS) * gain + bias


def _in_proj_kernel(x_ref, wf_ref, wgqk_ref, wgv_ref, wgg_ref, wm_ref, ws_ref,
                    f_ref, gqk_ref, gv_ref, gg_ref, m_ref, s_ref):
    xb = x_ref[...].astype(BF16)
    f_ref[...] = _dot(xb, wf_ref[...]).astype(BF16)
    gqk_ref[...] = _dot(xb, wgqk_ref[...]).astype(BF16)
    gv_ref[...] = _dot(xb, wgv_ref[...]).astype(BF16)
    gg_ref[...] = _dot(xb, wgg_ref[...])
    m_ref[...] = _dot(xb, wm_ref[...])
    s_ref[...] = _dot(xb, ws_ref[...])


def _in_proj(x2, w_in):
    T = x2.shape[0]
    tm = 256
    pts = [0]
    for s in IN_SPLIT_SIZES:
        pts.append(pts[-1] + s)
    col = lambda i, j: w_in[:, pts[i]:pts[j]]
    wf = col(0, 3).astype(BF16)
    wgqk = col(4, 6).astype(BF16)
    wgv = col(6, 7).astype(BF16)
    wgg = col(7, 8).astype(BF16)
    wm = col(9, 11).astype(BF16)
    ws = jnp.concatenate([col(3, 4), col(8, 9)], axis=1)
    ws = jnp.pad(ws, ((0, 0), (0, LANES - ws.shape[1]))).astype(BF16)
    ws_list = [wf, wgqk, wgv, wgg, wm, ws]
    out_dtypes = [BF16, BF16, BF16, F32, F32, F32]
    const = lambda w: pl.BlockSpec(w.shape, lambda i: (0, 0))
    return pl.pallas_call(
        _in_proj_kernel,
        grid=(T // tm,),
        in_specs=[pl.BlockSpec((tm, D_MODEL), lambda i: (i, 0))] + [const(w) for w in ws_list],
        out_specs=[pl.BlockSpec((tm, w.shape[1]), lambda i: (i, 0)) for w in ws_list],
        out_shape=[jax.ShapeDtypeStruct((T, w.shape[1]), dt) for w, dt in zip(ws_list, out_dtypes)],
        compiler_params=pltpu.CompilerParams(dimension_semantics=("arbitrary",), vmem_limit_bytes=VMEM_LIMIT),
        name="in_proj",
    )(x2, *ws_list)


def _fox_gate_kernel(s_ref, bias_ref, c_ref):
    S = s_ref.shape[0]
    ff_t = s_ref[...].T[FF_COL:FF_COL + FOX_HEADS, :]
    log_f = jax.nn.log_sigmoid(ff_t + bias_ref[...])
    r = lax.broadcasted_iota(jnp.int32, (LANES, LANES), 0)
    c = lax.broadcasted_iota(jnp.int32, (LANES, LANES), 1)
    tri = (r <= c).astype(F32)
    carry = jnp.zeros((FOX_HEADS, 1), F32)
    for j in range(S // LANES):
        blk = log_f[:, j * LANES:(j + 1) * LANES]
        cs = _dot(blk, tri, precision=lax.Precision.HIGHEST) + carry
        c_ref[0, :, j * LANES:(j + 1) * LANES] = cs
        carry = cs[:, LANES - 1:LANES]


def _fox_gate(small, fox_f_bias, B, S):
    return pl.pallas_call(
        _fox_gate_kernel,
        grid=(B,),
        in_specs=[pl.BlockSpec((S, LANES), lambda b: (b, 0)),
                  pl.BlockSpec((FOX_HEADS, 1), lambda b: (0, 0))],
        out_specs=pl.BlockSpec((1, FOX_HEADS, S), lambda b: (b, 0, 0)),
        out_shape=jax.ShapeDtypeStruct((B, FOX_HEADS, S), F32),
        compiler_params=pltpu.CompilerParams(dimension_semantics=("arbitrary",)),
        name="fox_gate",
    )(small, fox_f_bias.reshape(FOX_HEADS, 1))


FOX_TQ = 256
FOX_TK = FOX_TQ


def _fox_attn_kernel(q_ref, k_ref, v_ref, c_ref, o_ref):
    qi = pl.program_id(2)
    tq, tk, dh = FOX_TQ, FOX_TK, FOX_HEAD_DIM
    n_h = LANES // dh
    qs = [q_ref[:, hh * dh:(hh + 1) * dh] * (dh ** -0.5) for hh in range(n_h)]

    def step(j, carry, masked):
        ks = pl.ds(pl.multiple_of(j * tk, tk), tk)
        k2 = k_ref[ks, :]
        v2 = v_ref[ks, :]
        out = []
        for hh in range(n_h):
            m, l, acc = carry[hh]
            s = _dot_nt(qs[hh], k2[:, hh * dh:(hh + 1) * dh]) - c_ref[hh, :, ks]
            if masked:
                r = lax.broadcasted_iota(jnp.int32, (tq, tk), 0)
                c = lax.broadcasted_iota(jnp.int32, (tq, tk), 1)
                s = jnp.where(c <= r, s, -jnp.inf)
            m_new = jnp.maximum(m, jnp.max(s, axis=1, keepdims=True))
            p = jnp.exp(s - m_new)
            a = jnp.exp(m - m_new)
            l = a * l + jnp.sum(p, axis=1, keepdims=True)
            acc = a * acc + _dot(p.astype(BF16), v2[:, hh * dh:(hh + 1) * dh])
            out.append((m_new, l, acc))
        return tuple(out)

    init = tuple((jnp.full((tq, 1), -jnp.inf, F32), jnp.zeros((tq, 1), F32), jnp.zeros((tq, dh), F32))
                 for _ in range(n_h))
    carry = lax.fori_loop(0, qi, lambda j, c: step(j, c, False), init)
    carry = step(qi, carry, True)
    o_ref[...] = jnp.concatenate([acc / l for _, l, acc in carry], axis=1).astype(BF16)


def _fox_attn(fqkv, c_t, B, S):
    T = B * S
    nq = S // FOX_TQ
    n_hp = FOX_WIDTH // LANES
    return pl.pallas_call(
        _fox_attn_kernel,
        grid=(B, n_hp, nq),
        in_specs=[pl.BlockSpec((FOX_TQ, LANES), lambda b, h, i: (b * nq + i, h)),
                  pl.BlockSpec((S, LANES), lambda b, h, i: (b, n_hp + h)),
                  pl.BlockSpec((S, LANES), lambda b, h, i: (b, 2 * n_hp + h)),
                  pl.BlockSpec((LANES // FOX_HEAD_DIM, 1, S), lambda b, h, i: (b * n_hp + h, 0, 0))],
        out_specs=pl.BlockSpec((FOX_TQ, LANES), lambda b, h, i: (b * nq + i, h)),
        out_shape=jax.ShapeDtypeStruct((T, FOX_WIDTH), BF16),
        compiler_params=pltpu.CompilerParams(dimension_semantics=("arbitrary", "arbitrary", "arbitrary")),
        name="fox_attn",
    )(fqkv, fqkv, fqkv, c_t.reshape(B * FOX_HEADS, 1, S))


def _gla_kernel(q_ref, k_ref, v_ref, s_ref, up_ref, gb_ref, gg_ref, gain_ref, o_ref):
    S = q_ref.shape[0]
    C = GLA_CHUNK
    r = lax.broadcasted_iota(jnp.int32, (C, C), 0)
    c = lax.broadcasted_iota(jnp.int32, (C, C), 1)
    tril = (r >= c).astype(F32)

    def body(ci, st_t):
        rows = pl.ds(pl.multiple_of(ci * C, C), C)
        z = _dot(s_ref[rows, :].astype(BF16), up_ref[...]) + gb_ref[...]
        la = jax.nn.log_sigmoid(z) * (1.0 / GLA_GATE_TEMP)
        cum = _dot(tril, la, precision=lax.Precision.HIGHEST)
        tot = cum[C - 1:C, :]
        kd = (k_ref[rows, :].astype(F32) * jnp.exp(tot - cum)).astype(BF16)
        st_t = st_t * jnp.exp(tot) + _dot_tn(v_ref[rows, :], kd)
        o = _dot_nt(q_ref[rows, :], st_t.astype(BF16)) * (GLA_KEY_DIM ** -0.5)
        o = o * lax.rsqrt(jnp.mean(o * o, axis=-1, keepdims=True) + GLA_NORM_EPS) * gain_ref[...]
        o = o * jax.nn.silu(gg_ref[rows, :])
        o_ref[rows, :] = o.astype(BF16)
        return st_t

    lax.fori_loop(0, S // C, body, jnp.zeros((GLA_VAL_DIM, GLA_KEY_DIM), F32))


def _gla(gqk, gv, small, gg, gla_gate_up, gla_gate_bias, gla_norm_gain, B, S):
    T = B * S
    up = jnp.zeros((LANES, GLA_QK_WIDTH), F32).at[GLR_COL:GLR_COL + GLA_GATE_RANK].set(gla_gate_up).astype(BF16)
    gb = gla_gate_bias.reshape(1, GLA_QK_WIDTH)
    gain = gla_norm_gain.reshape(1, GLA_V_WIDTH)
    H = GLA_HEADS
    return pl.pallas_call(
        _gla_kernel,
        grid=(B, H),
        in_specs=[pl.BlockSpec((S, GLA_KEY_DIM), lambda b, h: (b, h)),
                  pl.BlockSpec((S, GLA_KEY_DIM), lambda b, h: (b, H + h)),
                  pl.BlockSpec((S, GLA_VAL_DIM), lambda b, h: (b, h)),
                  pl.BlockSpec((S, LANES), lambda b, h: (b, 0)),
                  pl.BlockSpec((LANES, GLA_KEY_DIM), lambda b, h: (0, h)),
                  pl.BlockSpec((1, GLA_KEY_DIM), lambda b, h: (0, h)),
                  pl.BlockSpec((S, GLA_VAL_DIM), lambda b, h: (b, h)),
                  pl.BlockSpec((1, GLA_VAL_DIM), lambda b, h: (0, h))],
        out_specs=pl.BlockSpec((S, GLA_VAL_DIM), lambda b, h: (b, h)),
        out_shape=jax.ShapeDtypeStruct((T, GLA_V_WIDTH), BF16),
        compiler_params=pltpu.CompilerParams(dimension_semantics=("arbitrary", "arbitrary")),
        name="gla",
    )(gqk, gqk, gv, small, up, gb, gg, gain)


def _merge_kernel(a_ref, og_ref, m_ref, x_ref, wf_ref, wg_ref, wo_ref, g_ref, b_ref, o_ref):
    y_fox = _dot(a_ref[...], wf_ref[...])
    y_gla = _dot(og_ref[...], wg_ref[...])
    merged = (jax.nn.sigmoid(m_ref[:, :D_MODEL]) * y_fox + jax.nn.sigmoid(m_ref[:, D_MODEL:]) * y_gla)
    mix = _dot(merged.astype(BF16), wo_ref[...])
    o_ref[...] = _layer_norm(ALPHA * x_ref[...] + mix, g_ref[...], b_ref[...])


def _merge(attn, og, mg, x2, w_out_fox, w_out_gla, w_out, ln_gain, ln_bias):
    T = x2.shape[0]
    tm = 256
    row = lambda w: pl.BlockSpec((tm, w), lambda i: (i, 0))
    const = lambda a: pl.BlockSpec(a.shape, lambda i: (0, 0))
    ws = [w_out_fox.astype(BF16), w_out_gla.astype(BF16), w_out.astype(BF16),
          ln_gain.reshape(1, D_MODEL), ln_bias.reshape(1, D_MODEL)]
    return pl.pallas_call(
        _merge_kernel,
        grid=(T // tm,),
        in_specs=[row(FOX_WIDTH), row(GLA_V_WIDTH), row(2 * D_MODEL), row(D_MODEL)] + [const(w) for w in ws],
        out_specs=row(D_MODEL),
        out_shape=jax.ShapeDtypeStruct((T, D_MODEL), F32),
        compiler_params=pltpu.CompilerParams(dimension_semantics=("arbitrary",), vmem_limit_bytes=VMEM_LIMIT),
        name="merge",
    )(attn, og, mg, x2, *ws)


ROUTE_TM = 256
CAND_BLOCKS = 10


def _route_kernel(x_ref, wq_ref, keys_ref, idx_ref, gate_ref, q_scr, st_scr, it_scr, best_scr):
    tm = ROUTE_TM
    K = PEER_TOPK
    q_scr[...] = _dot(x_ref[...].astype(BF16), wq_ref[...])
    key_iota = lax.broadcasted_iota(jnp.int32, (PEER_N_KEYS, tm), 0)

    def stage1(hp, _):
        q = q_scr[:, pl.ds(pl.multiple_of(hp * PEER_HALF, PEER_HALF), PEER_HALF)].astype(BF16)
        s = _dot_nt(keys_ref[hp], q)
        for i in range(K):
            m = jnp.max(s, axis=0, keepdims=True)
            idx = jnp.min(jnp.where(s == m, key_iota, PEER_N_KEYS), axis=0, keepdims=True)
            st_scr[hp, i:i + 1, :] = m
            it_scr[hp, i:i + 1, :] = idx
            s = jnp.where(key_iota == idx, -jnp.inf, s)
        return 0

    lax.fori_loop(0, 2 * PEER_HEADS, stage1, 0)

    n_cand = CAND_BLOCKS * SUBLANES
    cand_iota = lax.broadcasted_iota(jnp.int32, (n_cand, tm), 0)

    def stage2(h, _):
        s0, s1 = st_scr[2 * h], st_scr[2 * h + 1]
        i0, i1 = it_scr[2 * h], it_scr[2 * h + 1]
        lo, hi = slice(0, SUBLANES), slice(SUBLANES, 2 * SUBLANES)
        cs = [s0[0:1] + s1[lo], s0[0:1] + s1[hi]]
        ci = [i0[0:1] * PEER_N_KEYS + i1[lo], i0[0:1] * PEER_N_KEYS + i1[hi]]
        for a in range(1, SUBLANES):
            cs.append(s0[a:a + 1] + s1[lo])
            ci.append(i0[a:a + 1] * PEER_N_KEYS + i1[lo])
        cs.append(s0[hi] + s1[0:1])
        ci.append(i0[hi] * PEER_N_KEYS + i1[0:1])
        cand = jnp.concatenate(cs, axis=0)
        cidx = jnp.concatenate(ci, axis=0)
        for i in range(K):
            m = jnp.max(cand, axis=0, keepdims=True)
            pos = jnp.min(jnp.where(cand == m, cand_iota, n_cand), axis=0, keepdims=True)
            sel = cand_iota == pos
            best_scr[i:i + 1, :] = m
            idx_ref[pl.ds(h * K + i, 1), :] = jnp.max(jnp.where(sel, cidx, -1), axis=0, keepdims=True)
            cand = jnp.where(sel, -jnp.inf, cand)
        best = best_scr[...]
        e = jnp.exp(best - best[0:1])
        gate_ref[pl.ds(pl.multiple_of(h * K, K), K), :] = e / jnp.sum(e, axis=0, keepdims=True)
        return 0

    lax.fori_loop(0, PEER_HEADS, stage2, 0)


def _route(x1, peer_w_query, peer_sub_keys):
    T = x1.shape[0]
    tm = ROUTE_TM
    wq = peer_w_query.reshape(D_MODEL, 2 * PEER_HEADS * PEER_HALF).astype(BF16)
    keys = peer_sub_keys.reshape(2 * PEER_HEADS, PEER_N_KEYS, PEER_HALF).astype(BF16)
    return pl.pallas_call(
        _route_kernel,
        grid=(T // tm,),
        in_specs=[pl.BlockSpec((tm, D_MODEL), lambda i: (i, 0)),
                  pl.BlockSpec(wq.shape, lambda i: (0, 0)),
                  pl.BlockSpec(keys.shape, lambda i: (0, 0, 0))],
        out_specs=[pl.BlockSpec((PEER_PICKS, tm), lambda i: (0, i)),
                   pl.BlockSpec((PEER_PICKS, tm), lambda i: (0, i))],
        out_shape=[jax.ShapeDtypeStruct((PEER_PICKS, T), jnp.int32),
                   jax.ShapeDtypeStruct((PEER_PICKS, T), F32)],
        scratch_shapes=[pltpu.VMEM((tm, 2 * PEER_HEADS * PEER_HALF), F32),
                        pltpu.VMEM((2 * PEER_HEADS, PEER_TOPK, tm), F32),
                        pltpu.VMEM((2 * PEER_HEADS, PEER_TOPK, tm), jnp.int32),
                        pltpu.VMEM((PEER_TOPK, tm), F32)],
        compiler_params=pltpu.CompilerParams(dimension_semantics=("arbitrary",), vmem_limit_bytes=VMEM_LIMIT),
        name="peer_route",
    )(x1, wq, keys)


SC_LANES = 16
SC_ROWS = 32
SC_CHUNKS = PEER_PICKS // SC_ROWS
SC_DCOLS = 256
SC_DVREGS = SC_DCOLS // SC_LANES
SC_PART = (D_MODEL // SC_DCOLS) * SC_LANES


def _sc_mesh():
    info = plsc.get_sparse_core_info()
    mesh = plsc.VectorSubcoreMesh(core_axis_name="c", subcore_axis_name="s")
    return mesh, info.num_cores, info.num_cores * info.num_subcores


def _sc_token_pipeline(tab_hbm, idx_hbm, vec_hbm, out_hbm, idx_v, vec_v, rows_v, out_v,
                       sem_rows, sem_tok, sem_out, n_cores, tpw, compute_chunk):
    wid = lax.axis_index("s") * n_cores + lax.axis_index("c")
    base = wid * tpw

    def gather(s, c):
        return pltpu.make_async_copy(tab_hbm.at[idx_v.at[s, pl.ds(c * SC_ROWS, SC_ROWS)]],
                                     rows_v.at[c % 2], sem_rows.at[c % 2])

    def tok_fetch(s, tok):
        return (pltpu.make_async_copy(idx_hbm.at[tok], idx_v.at[s], sem_tok.at[0]),
                pltpu.make_async_copy(vec_hbm.at[tok], vec_v.at[s], sem_tok.at[1]))

    def out_copy(s, tok):
        return pltpu.make_async_copy(out_v.at[s], out_hbm.at[tok], sem_out.at[s])

    for d in tok_fetch(0, base):
        d.start()
    for d in tok_fetch(0, base):
        d.wait()
    gather(0, 0).start()

    def pair_body(tp, _):
        for s in range(2):
            t = tp * 2 + s
            tok = base + t
            has_next = t + 1 < tpw

            @pl.when(has_next)
            def _():
                for d in tok_fetch(1 - s, tok + 1):
                    d.start()

            @pl.when(t >= 2)
            def _():
                out_copy(s, tok - 2).wait()

            for c in range(SC_CHUNKS):
                if c + 1 < SC_CHUNKS:
                    gather(s, c + 1).start()
                else:
                    @pl.when(has_next)
                    def _():
                        for d in tok_fetch(1 - s, tok + 1):
                            d.wait()
                        gather(1 - s, 0).start()
                gather(s, c).wait()
                compute_chunk(s, c)
            out_copy(s, tok).start()
        return 0

    lax.fori_loop(0, tpw // 2, pair_body, 0)
    for s in range(2):
        out_copy(s, base + tpw - 2 + s).wait()


def _sc_scratch(vec_len, out_len):
    return [pltpu.VMEM((2, PEER_PICKS), jnp.int32),
            pltpu.VMEM((2, vec_len), F32),
            pltpu.VMEM((2, SC_ROWS, D_MODEL), F32),
            pltpu.VMEM((2, out_len), F32),
            pltpu.SemaphoreType.DMA((2,)),
            pltpu.SemaphoreType.DMA((2,)),
            pltpu.SemaphoreType.DMA((2,))]


def _sc_dot(expert_u, idx, h):
    T = h.shape[0]
    mesh, n_cores, n_workers = _sc_mesh()
    tpw = T // n_workers
    L = SC_LANES

    @functools.partial(
        pl.kernel, mesh=mesh, out_type=jax.ShapeDtypeStruct((T, PEER_PICKS), F32),
        scratch_types=_sc_scratch(D_MODEL, PEER_PICKS) + [pltpu.VMEM((SC_ROWS, SC_PART), F32)],
        compiler_params=pltpu.CompilerParams(needs_layout_passes=False), name="peer_sc_dot")
    def k(u_hbm, idx_hbm, h_hbm, act_hbm, idx_v, h_v, rows_v, act_v, sem_rows, sem_tok, sem_out, part_v):
        lane = lax.iota(jnp.int32, L)

        def compute_chunk(s, c):
            b = c % 2
            for dc in range(D_MODEL // SC_DCOLS):
                hs = [h_v[s, pl.ds(dc * SC_DCOLS + j * L, L)] for j in range(SC_DVREGS)]

                @plsc.parallel_loop(0, SC_ROWS, unroll=2)
                def _(r, dc=dc, hs=hs):
                    prods = [rows_v[b, r, pl.ds(dc * SC_DCOLS + j * L, L)] * hs[j] for j in range(SC_DVREGS)]
                    while len(prods) > 1:
                        prods = [prods[i] + prods[i + 1] for i in range(0, len(prods), 2)]
                    part_v[r, pl.ds(dc * L, L)] = prods[0]
            for g in range(SC_ROWS // L):
                rows16 = lane + g * L

                def col_body(col, tots):
                    rot = (lane + col) & (L - 1)
                    return tuple(t + plsc.load_gather(part_v, [rows16, rot + q * L]) for q, t in enumerate(tots))

                tots = lax.fori_loop(0, L, col_body, tuple(jnp.zeros((L,), F32) for _ in range(SC_PART // L)))
                act_v[s, pl.ds(c * SC_ROWS + g * L, L)] = (tots[0] + tots[1]) + (tots[2] + tots[3])

        _sc_token_pipeline(u_hbm, idx_hbm, h_hbm, act_hbm, idx_v, h_v, rows_v, act_v,
                           sem_rows, sem_tok, sem_out, n_cores, tpw, compute_chunk)

    return k(expert_u, idx, h)


def _sc_acc(expert_v, idx, w):
    T = w.shape[0]
    mesh, n_cores, n_workers = _sc_mesh()
    tpw = T // n_workers
    L = SC_LANES

    @functools.partial(
        pl.kernel, mesh=mesh, out_type=jax.ShapeDtypeStruct((T, D_MODEL), F32),
        scratch_types=_sc_scratch(PEER_PICKS, D_MODEL),
        compiler_params=pltpu.CompilerParams(needs_layout_passes=False), name="peer_sc_acc")
    def k(v_hbm, idx_hbm, w_hbm, out_hbm, idx_v, w_v, rows_v, out_v, sem_rows, sem_tok, sem_out):
        def compute_chunk(s, c):
            b = c % 2
            for dc in range(D_MODEL // SC_DCOLS):
                cols = [pl.ds(dc * SC_DCOLS + j * L, L) for j in range(SC_DVREGS)]
                if c == 0:
                    accs = tuple(jnp.zeros((L,), F32) for _ in cols)
                else:
                    accs = tuple(out_v[s, cs] for cs in cols)

                def row_body(r, accs, cols=cols):
                    wb = plsc.load_gather(w_v.at[s], [jnp.full((L,), c * SC_ROWS, jnp.int32) + r])
                    return tuple(a + wb * rows_v[b, r, cs] for a, cs in zip(accs, cols))

                accs = lax.fori_loop(0, SC_ROWS, row_body, accs)
                for a, cs in zip(accs, cols):
                    out_v[s, cs] = a

        _sc_token_pipeline(v_hbm, idx_hbm, w_hbm, out_hbm, idx_v, w_v, rows_v, out_v,
                           sem_rows, sem_tok, sem_out, n_cores, tpw, compute_chunk)

    return k(expert_v, idx, w)


def _gelu_gate_kernel(a_ref, g_ref, o_ref):
    a = a_ref[...]
    o_ref[...] = 0.5 * a * (1.0 + lax.erf(a * (2.0 ** -0.5))) * g_ref[...]


def _ln2_kernel(h_ref, f_ref, g_ref, b_ref, o_ref):
    o_ref[...] = _layer_norm(ALPHA * h_ref[...] + f_ref[...], g_ref[...], b_ref[...])


def _mix(x1, idx_t, gate_t, expert_u, expert_v, ln_gain, ln_bias):
    T = x1.shape[0]
    tm = 512
    idx = idx_t.T
    gates = gate_t.T
    row = lambda w: pl.BlockSpec((tm, w), lambda i: (i, 0))
    const = pl.BlockSpec((1, D_MODEL), lambda i: (0, 0))
    params = pltpu.CompilerParams(dimension_semantics=("arbitrary",))
    act = _sc_dot(expert_u, idx, x1)
    w = pl.pallas_call(
        _gelu_gate_kernel, grid=(T // tm,), in_specs=[row(PEER_PICKS), row(PEER_PICKS)], out_specs=row(PEER_PICKS),
        out_shape=jax.ShapeDtypeStruct((T, PEER_PICKS), F32), compiler_params=params, name="peer_gelu_gate",
    )(act, gates)
    ffn = _sc_acc(expert_v, idx, w)
    return pl.pallas_call(
        _ln2_kernel, grid=(T // tm,), in_specs=[row(D_MODEL), row(D_MODEL), const, const], out_specs=row(D_MODEL),
        out_shape=jax.ShapeDtypeStruct((T, D_MODEL), F32), compiler_params=params, name="peer_ln2",
    )(x1, ffn, ln_gain.reshape(1, D_MODEL), ln_bias.reshape(1, D_MODEL))


BATCH_CHUNKS = 2


def kernel(x, w_in, fox_f_bias, gla_gate_up, gla_gate_bias, gla_norm_gain, w_out_fox, w_out_gla, w_out,
           ln1_gain, ln1_bias, peer_w_query, peer_sub_keys, peer_expert_u, peer_expert_v, ln2_gain, ln2_bias):
    B, S, D = x.shape
    assert D == D_MODEL and S % max(ROUTE_TM, FOX_TQ) == 0 and w_in.shape[0] == DEPTH
    n_chunks = BATCH_CHUNKS if B % BATCH_CHUNKS == 0 else 1
    bc = B // n_chunks
    outs = []
    for ci in range(n_chunks):
        h = x[ci * bc:(ci + 1) * bc].reshape(bc * S, D)
        for l in range(DEPTH):
            fqkv, gqk, gv, gg, mg, small = _in_proj(h, w_in[l])
            c_t = _fox_gate(small, fox_f_bias[l], bc, S)
            attn = _fox_attn(fqkv, c_t, bc, S)
            og = _gla(gqk, gv, small, gg, gla_gate_up[l], gla_gate_bias[l], gla_norm_gain[l], bc, S)
            h = _merge(attn, og, mg, h, w_out_fox[l], w_out_gla[l], w_out[l], ln1_gain[l], ln1_bias[l])
            idx_t, gate_t = _route(h, peer_w_query[l], peer_sub_keys[l])
            h = _mix(h, idx_t, gate_t, peer_expert_u[l], peer_expert_v[l], ln2_gain[l], ln2_bias[l])
        outs.append(h)
    return jnp.concatenate(outs, axis=0).reshape(B, S, D)
```

```python
import functools
import math

import jax
import jax.numpy as jnp
from jax import lax
from jax.experimental import pallas as pl
from jax.experimental.pallas import tpu as pltpu
from jax.experimental.pallas import tpu_sc as plsc

F32 = jnp.float32
BF16 = jnp.bfloat16

D_MODEL = 1024
FOX_HEADS = 8
FOX_HEAD_DIM = 64
FOX_WIDTH = FOX_HEADS * FOX_HEAD_DIM
GLA_HEADS = 4
GLA_KEY_DIM = 128
GLA_VAL_DIM = 256
GLA_QK_WIDTH = GLA_HEADS * GLA_KEY_DIM
GLA_V_WIDTH = GLA_HEADS * GLA_VAL_DIM
GLA_GATE_RANK = 16
GLA_GATE_TEMP = 16.0
GLA_NORM_EPS = 1e-5
GLA_CHUNK = 64
PEER_HEADS = 8
PEER_N_KEYS = 128
PEER_HALF = 128
PEER_TOPK = 16
PEER_PICKS = PEER_HEADS * PEER_TOPK
DEPTH = 1
ALPHA = (2.0 * DEPTH) ** 0.25
LN_EPS = 1e-5

LANES = 128
SUBLANES = 8
VMEM_LIMIT = 52 * 1024 * 1024

IN_SPLIT_SIZES = (FOX_WIDTH, FOX_WIDTH, FOX_WIDTH, FOX_HEADS,
                  GLA_QK_WIDTH, GLA_QK_WIDTH, GLA_V_WIDTH, GLA_V_WIDTH, GLA_GATE_RANK,
                  D_MODEL, D_MODEL)
FF_COL = 0
GLR_COL = FOX_HEADS


def _dot(a, b, **kw):
    return jnp.dot(a, b, preferred_element_type=F32, **kw)


def _dot_nt(a, b):
    return lax.dot_general(a, b, (((1,), (1,)), ((), ())), preferred_element_type=F32)


def _dot_tn(a, b):
    return lax.dot_general(a, b, (((0,), (0,)), ((), ())), preferred_element_type=F32)


def _layer_norm(y, gain, bias):
    mu = jnp.mean(y, axis=-1, keepdims=True)
    yc = y - mu
    var = jnp.mean(yc * yc, axis=-1, keepdims=True)
    return yc * lax.rsqrt(var + LN_EPS) * gain + bias


def _in_proj_kernel(x_ref, wf_ref, wgqk_ref, wgv_ref, wgg_ref, wm_ref, ws_ref,
                    f_ref, gqk_ref, gv_ref, gg_ref, m_ref, s_ref):
    xb = x_ref[...].astype(BF16)
    f_ref[...] = _dot(xb, wf_ref[...]).astype(BF16)
    gqk_ref[...] = _dot(xb, wgqk_ref[...]).astype(BF16)
    gv_ref[...] = _dot(xb, wgv_ref[...]).astype(BF16)
    gg_ref[...] = _dot(xb, wgg_ref[...])
    m_ref[...] = _dot(xb, wm_ref[...])
    s_ref[...] = _dot(xb, ws_ref[...])


def _in_proj(x2, w_in):
    T = x2.shape[0]
    tm = 256
    pts = [0]
    for s in IN_SPLIT_SIZES:
        pts.append(pts[-1] + s)
    col = lambda i, j: w_in[:, pts[i]:pts[j]]
    wf = col(0, 3).astype(BF16)
    wgqk = col(4, 6).astype(BF16)
    wgv = col(6, 7).astype(BF16)
    wgg = col(7, 8).astype(BF16)
    wm = col(9, 11).astype(BF16)
    ws = jnp.concatenate([col(3, 4), col(8, 9)], axis=1)
    ws = jnp.pad(ws, ((0, 0), (0, LANES - ws.shape[1]))).astype(BF16)
    ws_list = [wf, wgqk, wgv, wgg, wm, ws]
    out_dtypes = [BF16, BF16, BF16, F32, F32, F32]
    const = lambda w: pl.BlockSpec(w.shape, lambda i: (0, 0))
    return pl.pallas_call(
        _in_proj_kernel,
        grid=(T // tm,),
        in_specs=[pl.BlockSpec((tm, D_MODEL), lambda i: (i, 0))] + [const(w) for w in ws_list],
        out_specs=[pl.BlockSpec((tm, w.shape[1]), lambda i: (i, 0)) for w in ws_list],
        out_shape=[jax.ShapeDtypeStruct((T, w.shape[1]), dt) for w, dt in zip(ws_list, out_dtypes)],
        compiler_params=pltpu.CompilerParams(dimension_semantics=("arbitrary",), vmem_limit_bytes=VMEM_LIMIT),
        name="in_proj",
    )(x2, *ws_list)


def _fox_gate_kernel(s_ref, bias_ref, c_ref):
    S = s_ref.shape[0]
    ff_t = s_ref[...].T[FF_COL:FF_COL + FOX_HEADS, :]
    log_f = jax.nn.log_sigmoid(ff_t + bias_ref[...])
    r = lax.broadcasted_iota(jnp.int32, (LANES, LANES), 0)
    c = lax.broadcasted_iota(jnp.int32, (LANES, LANES), 1)
    tri = (r <= c).astype(F32)
    carry = jnp.zeros((FOX_HEADS, 1), F32)
    for j in range(S // LANES):
        blk = log_f[:, j * LANES:(j + 1) * LANES]
        cs = _dot(blk, tri, precision=lax.Precision.HIGHEST) + carry
        c_ref[0, :, j * LANES:(j + 1) * LANES] = cs
        carry = cs[:, LANES - 1:LANES]


def _fox_gate(small, fox_f_bias, B, S):
    return pl.pallas_call(
        _fox_gate_kernel,
        grid=(B,),
        in_specs=[pl.BlockSpec((S, LANES), lambda b: (b, 0)),
                  pl.BlockSpec((FOX_HEADS, 1), lambda b: (0, 0))],
        out_specs=pl.BlockSpec((1, FOX_HEADS, S), lambda b: (b, 0, 0)),
        out_shape=jax.ShapeDtypeStruct((B, FOX_HEADS, S), F32),
        compiler_params=pltpu.CompilerParams(dimension_semantics=("arbitrary",)),
        name="fox_gate",
    )(small, fox_f_bias.reshape(FOX_HEADS, 1))


FOX_TQ = 256
FOX_TK = FOX_TQ


def _fox_attn_kernel(q_ref, k_ref, v_ref, c_ref, o_ref):
    qi = pl.program_id(2)
    tq, tk, dh = FOX_TQ, FOX_TK, FOX_HEAD_DIM
    n_h = LANES // dh
    qs = [q_ref[:, hh * dh:(hh + 1) * dh] * (dh ** -0.5) for hh in range(n_h)]

    def step(j, carry, masked):
        ks = pl.ds(pl.multiple_of(j * tk, tk), tk)
        k2 = k_ref[ks, :]
        v2 = v_ref[ks, :]
        out = []
        for hh in range(n_h):
            m, l, acc = carry[hh]
            s = _dot_nt(qs[hh], k2[:, hh * dh:(hh + 1) * dh]) - c_ref[hh, :, ks]
            if masked:
                r = lax.broadcasted_iota(jnp.int32, (tq, tk), 0)
                c = lax.broadcasted_iota(jnp.int32, (tq, tk), 1)
                s = jnp.where(c <= r, s, -jnp.inf)
            m_new = jnp.maximum(m, jnp.max(s, axis=1, keepdims=True))
            p = jnp.exp(s - m_new)
            a = jnp.exp(m - m_new)
            l = a * l + jnp.sum(p, axis=1, keepdims=True)
            acc = a * acc + _dot(p.astype(BF16), v2[:, hh * dh:(hh + 1) * dh])
            out.append((m_new, l, acc))
        return tuple(out)

    init = tuple((jnp.full((tq, 1), -jnp.inf, F32), jnp.zeros((tq, 1), F32), jnp.zeros((tq, dh), F32))
                 for _ in range(n_h))
    carry = lax.fori_loop(0, qi, lambda j, c: step(j, c, False), init)
    carry = step(qi, carry, True)
    o_ref[...] = jnp.concatenate([acc / l for _, l, acc in carry], axis=1).astype(BF16)


def _fox_attn(fqkv, c_t, B, S):
    T = B * S
    nq = S // FOX_TQ
    n_hp = FOX_WIDTH // LANES
    return pl.pallas_call(
        _fox_attn_kernel,
        grid=(B, n_hp, nq),
        in_specs=[pl.BlockSpec((FOX_TQ, LANES), lambda b, h, i: (b * nq + i, h)),
                  pl.BlockSpec((S, LANES), lambda b, h, i: (b, n_hp + h)),
                  pl.BlockSpec((S, LANES), lambda b, h, i: (b, 2 * n_hp + h)),
                  pl.BlockSpec((LANES // FOX_HEAD_DIM, 1, S), lambda b, h, i: (b * n_hp + h, 0, 0))],
        out_specs=pl.BlockSpec((FOX_TQ, LANES), lambda b, h, i: (b * nq + i, h)),
        out_shape=jax.ShapeDtypeStruct((T, FOX_WIDTH), BF16),
        compiler_params=pltpu.CompilerParams(dimension_semantics=("arbitrary", "arbitrary", "arbitrary")),
        name="fox_attn",
    )(fqkv, fqkv, fqkv, c_t.reshape(B * FOX_HEADS, 1, S))


def _gla_kernel(q_ref, k_ref, v_ref, s_ref, up_ref, gb_ref, gg_ref, gain_ref, o_ref):
    S = q_ref.shape[0]
    C = GLA_CHUNK
    r = lax.broadcasted_iota(jnp.int32, (C, C), 0)
    c = lax.broadcasted_iota(jnp.int32, (C, C), 1)
    tril = (r >= c).astype(F32)

    def body(ci, st_t):
        rows = pl.ds(pl.multiple_of(ci * C, C), C)
        z = _dot(s_ref[rows, :].astype(BF16), up_ref[...]) + gb_ref[...]
        la = jax.nn.log_sigmoid(z) * (1.0 / GLA_GATE_TEMP)
        cum = _dot(tril, la, precision=lax.Precision.HIGHEST)
        tot = cum[C - 1:C, :]
        kd = (k_ref[rows, :].astype(F32) * jnp.exp(tot - cum)).astype(BF16)
        st_t = st_t * jnp.exp(tot) + _dot_tn(v_ref[rows, :], kd)
        o = _dot_nt(q_ref[rows, :], st_t.astype(BF16)) * (GLA_KEY_DIM ** -0.5)
        o = o * lax.rsqrt(jnp.mean(o * o, axis=-1, keepdims=True) + GLA_NORM_EPS) * gain_ref[...]
        o = o * jax.nn.silu(gg_ref[rows, :])
        o_ref[rows, :] = o.astype(BF16)
        return st_t

    lax.fori_loop(0, S // C, body, jnp.zeros((GLA_VAL_DIM, GLA_KEY_DIM), F32))


def _gla(gqk, gv, small, gg, gla_gate_up, gla_gate_bias, gla_norm_gain, B, S):
    T = B * S
    up = jnp.zeros((LANES, GLA_QK_WIDTH), F32).at[GLR_COL:GLR_COL + GLA_GATE_RANK].set(gla_gate_up).astype(BF16)
    gb = gla_gate_bias.reshape(1, GLA_QK_WIDTH)
    gain = gla_norm_gain.reshape(1, GLA_V_WIDTH)
    H = GLA_HEADS
    return pl.pallas_call(
        _gla_kernel,
        grid=(B, H),
        in_specs=[pl.BlockSpec((S, GLA_KEY_DIM), lambda b, h: (b, h)),
                  pl.BlockSpec((S, GLA_KEY_DIM), lambda b, h: (b, H + h)),
                  pl.BlockSpec((S, GLA_VAL_DIM), lambda b, h: (b, h)),
                  pl.BlockSpec((S, LANES), lambda b, h: (b, 0)),
                  pl.BlockSpec((LANES, GLA_KEY_DIM), lambda b, h: (0, h)),
                  pl.BlockSpec((1, GLA_KEY_DIM), lambda b, h: (0, h)),
                  pl.BlockSpec((S, GLA_VAL_DIM), lambda b, h: (b, h)),
                  pl.BlockSpec((1, GLA_VAL_DIM), lambda b, h: (0, h))],
        out_specs=pl.BlockSpec((S, GLA_VAL_DIM), lambda b, h: (b, h)),
        out_shape=jax.ShapeDtypeStruct((T, GLA_V_WIDTH), BF16),
        compiler_params=pltpu.CompilerParams(dimension_semantics=("arbitrary", "arbitrary")),
        name="gla",
    )(gqk, gqk, gv, small, up, gb, gg, gain)


def _merge_kernel(a_ref, og_ref, m_ref, x_ref, wf_ref, wg_ref, wo_ref, g_ref, b_ref, o_ref):
    y_fox = _dot(a_ref[...], wf_ref[...])
    y_gla = _dot(og_ref[...], wg_ref[...])
    merged = (jax.nn.sigmoid(m_ref[:, :D_MODEL]) * y_fox + jax.nn.sigmoid(m_ref[:, D_MODEL:]) * y_gla)
    mix = _dot(merged.astype(BF16), wo_ref[...])
    o_ref[...] = _layer_norm(ALPHA * x_ref[...] + mix, g_ref[...], b_ref[...])


def _merge(attn, og, mg, x2, w_out_fox, w_out_gla, w_out, ln_gain, ln_bias):
    T = x2.shape[0]
    tm = 256
    row = lambda w: pl.BlockSpec((tm, w), lambda i: (i, 0))
    const = lambda a: pl.BlockSpec(a.shape, lambda i: (0, 0))
    ws = [w_out_fox.astype(BF16), w_out_gla.astype(BF16), w_out.astype(BF16),
          ln_gain.reshape(1, D_MODEL), ln_bias.reshape(1, D_MODEL)]
    return pl.pallas_call(
        _merge_kernel,
        grid=(T // tm,),
        in_specs=[row(FOX_WIDTH), row(GLA_V_WIDTH), row(2 * D_MODEL), row(D_MODEL)] + [const(w) for w in ws],
        out_specs=row(D_MODEL),
        out_shape=jax.ShapeDtypeStruct((T, D_MODEL), F32),
        compiler_params=pltpu.CompilerParams(dimension_semantics=("arbitrary",), vmem_limit_bytes=VMEM_LIMIT),
        name="merge",
    )(attn, og, mg, x2, *ws)


ROUTE_TM = 256
CAND_BLOCKS = 10


def _route_kernel(x_ref, wq_ref, keys_ref, idx_ref, gate_ref, q_scr, st_scr, it_scr, best_scr, pick_scr, gsel_scr):
    tm = ROUTE_TM
    K = PEER_TOPK
    q_scr[...] = _dot(x_ref[...].astype(BF16), wq_ref[...])
    key_iota = lax.broadcasted_iota(jnp.int32, (PEER_N_KEYS, tm), 0)

    def stage1(hp, _):
        q = q_scr[:, pl.ds(pl.multiple_of(hp * PEER_HALF, PEER_HALF), PEER_HALF)].astype(BF16)
        s = _dot_nt(keys_ref[hp], q)
        for i in range(K):
            m = jnp.max(s, axis=0, keepdims=True)
            idx = jnp.min(jnp.where(s == m, key_iota, PEER_N_KEYS), axis=0, keepdims=True)
            st_scr[hp, i:i + 1, :] = m
            it_scr[hp, i:i + 1, :] = idx
            s = jnp.where(key_iota == idx, -jnp.inf, s)
        return 0

    lax.fori_loop(0, 2 * PEER_HEADS, stage1, 0)

    n_cand = CAND_BLOCKS * SUBLANES
    cand_iota = lax.broadcasted_iota(jnp.int32, (n_cand, tm), 0)

    def stage2(h, _):
        s0, s1 = st_scr[2 * h], st_scr[2 * h + 1]
        i0, i1 = it_scr[2 * h], it_scr[2 * h + 1]
        lo, hi = slice(0, SUBLANES), slice(SUBLANES, 2 * SUBLANES)
        cs = [s0[0:1] + s1[lo], s0[0:1] + s1[hi]]
        ci = [i0[0:1] * PEER_N_KEYS + i1[lo], i0[0:1] * PEER_N_KEYS + i1[hi]]
        for a in range(1, SUBLANES):
            cs.append(s0[a:a + 1] + s1[lo])
            ci.append(i0[a:a + 1] * PEER_N_KEYS + i1[lo])
        cs.append(s0[hi] + s1[0:1])
        ci.append(i0[hi] * PEER_N_KEYS + i1[0:1])
        cand = jnp.concatenate(cs, axis=0)
        cidx = jnp.concatenate(ci, axis=0)
        for i in range(K):
            m = jnp.max(cand, axis=0, keepdims=True)
            pos = jnp.min(jnp.where(cand == m, cand_iota, n_cand), axis=0, keepdims=True)
            sel = cand_iota == pos
            best_scr[i:i + 1, :] = m
            pick_scr[pl.ds(h * K + i, 1), :] = jnp.max(jnp.where(sel, cidx, -1), axis=0, keepdims=True)
            cand = jnp.where(sel, -jnp.inf, cand)
        best = best_scr[...]
        e = jnp.exp(best - best[0:1])
        gsel_scr[pl.ds(pl.multiple_of(h * K, K), K), :] = e / jnp.sum(e, axis=0, keepdims=True)
        return 0

    lax.fori_loop(0, PEER_HEADS, stage2, 0)
    idx_ref[...] = pick_scr[...].T
    gate_ref[...] = gsel_scr[...].T


def _route(x1, peer_w_query, peer_sub_keys):
    T = x1.shape[0]
    tm = ROUTE_TM
    wq = peer_w_query.reshape(D_MODEL, 2 * PEER_HEADS * PEER_HALF).astype(BF16)
    keys = peer_sub_keys.reshape(2 * PEER_HEADS, PEER_N_KEYS, PEER_HALF).astype(BF16)
    return pl.pallas_call(
        _route_kernel,
        grid=(T // tm,),
        in_specs=[pl.BlockSpec((tm, D_MODEL), lambda i: (i, 0)),
                  pl.BlockSpec(wq.shape, lambda i: (0, 0)),
                  pl.BlockSpec(keys.shape, lambda i: (0, 0, 0))],
        out_specs=[pl.BlockSpec((tm, PEER_PICKS), lambda i: (i, 0)),
                   pl.BlockSpec((tm, PEER_PICKS), lambda i: (i, 0))],
        out_shape=[jax.ShapeDtypeStruct((T, PEER_PICKS), jnp.int32),
                   jax.ShapeDtypeStruct((T, PEER_PICKS), F32)],
        scratch_shapes=[pltpu.VMEM((tm, 2 * PEER_HEADS * PEER_HALF), F32),
                        pltpu.VMEM((2 * PEER_HEADS, PEER_TOPK, tm), F32),
                        pltpu.VMEM((2 * PEER_HEADS, PEER_TOPK, tm), jnp.int32),
                        pltpu.VMEM((PEER_TOPK, tm), F32),
                        pltpu.VMEM((PEER_PICKS, tm), jnp.int32),
                        pltpu.VMEM((PEER_PICKS, tm), F32)],
        compiler_params=pltpu.CompilerParams(dimension_semantics=("arbitrary",), vmem_limit_bytes=VMEM_LIMIT),
        name="peer_route",
    )(x1, wq, keys)


SC_LANES = 16
SC_ROWS = 64
SC_CHUNKS = PEER_PICKS // SC_ROWS
SC_DCOLS = 256
SC_DVREGS = SC_DCOLS // SC_LANES
SC_PART = (D_MODEL // SC_DCOLS) * SC_LANES


def _sc_mesh():
    info = plsc.get_sparse_core_info()
    mesh = plsc.VectorSubcoreMesh(core_axis_name="c", subcore_axis_name="s")
    return mesh, info.num_cores, info.num_cores * info.num_subcores


def _sc_token_pipeline(tab_hbm, idx_hbm, vec_hbm, out_hbm, idx_v, vec_v, rows_v, out_v,
                       sem_rows, sem_tok, sem_out, n_cores, tpw, compute_chunk):
    wid = lax.axis_index("s") * n_cores + lax.axis_index("c")
    base = wid * tpw

    def gather(s, c):
        return pltpu.make_async_copy(tab_hbm.at[idx_v.at[s, pl.ds(c * SC_ROWS, SC_ROWS)]],
                                     rows_v.at[c % 2], sem_rows.at[c % 2])

    def tok_fetch(s, tok):
        return (pltpu.make_async_copy(idx_hbm.at[tok], idx_v.at[s], sem_tok.at[0]),
                pltpu.make_async_copy(vec_hbm.at[tok], vec_v.at[s], sem_tok.at[1]))

    def out_copy(s, tok):
        return pltpu.make_async_copy(out_v.at[s], out_hbm.at[tok], sem_out.at[s])

    for d in tok_fetch(0, base):
        d.start()
    for d in tok_fetch(0, base):
        d.wait()
    gather(0, 0).start()

    def pair_body(tp, _):
        for s in range(2):
            t = tp * 2 + s
            tok = base + t
            has_next = t + 1 < tpw

            @pl.when(has_next)
            def _():
                for d in tok_fetch(1 - s, tok + 1):
                    d.start()

            @pl.when(t >= 2)
            def _():
                out_copy(s, tok - 2).wait()

            for c in range(SC_CHUNKS):
                if c + 1 < SC_CHUNKS:
                    gather(s, c + 1).start()
                else:
                    @pl.when(has_next)
                    def _():
                        for d in tok_fetch(1 - s, tok + 1):
                            d.wait()
                        gather(1 - s, 0).start()
                gather(s, c).wait()
                compute_chunk(s, c)
            out_copy(s, tok).start()
        return 0

    lax.fori_loop(0, tpw // 2, pair_body, 0)
    for s in range(2):
        out_copy(s, base + tpw - 2 + s).wait()


def _sc_scratch(vec_len, out_len):
    return [pltpu.VMEM((2, PEER_PICKS), jnp.int32),
            pltpu.VMEM((2, vec_len), F32),
            pltpu.VMEM((2, SC_ROWS, D_MODEL // 2), jnp.int32),
            pltpu.VMEM((2, out_len), F32),
            pltpu.SemaphoreType.DMA((2,)),
            pltpu.SemaphoreType.DMA((2,)),
            pltpu.SemaphoreType.DMA((2,))]


def _pack_table(tab):
    n, d = tab.shape
    bits = lax.bitcast_convert_type(tab.astype(BF16), jnp.uint16).astype(jnp.uint32).reshape(n, d // 32, 2, SC_LANES)
    words = bits[:, :, 0, :] | (bits[:, :, 1, :] << 16)
    return lax.bitcast_convert_type(words, jnp.int32).reshape(n, d // 2)


def _unpack_words(x):
    return plsc.bitcast(x << 16, F32), plsc.bitcast(x & jnp.int32(-65536), F32)


def _sc_dot(u_packed, idx, h):
    T = h.shape[0]
    mesh, n_cores, n_workers = _sc_mesh()
    tpw = T // n_workers
    L = SC_LANES

    @functools.partial(
        pl.kernel, mesh=mesh, out_type=jax.ShapeDtypeStruct((T, PEER_PICKS), F32),
        scratch_types=_sc_scratch(D_MODEL, PEER_PICKS) + [pltpu.VMEM((SC_ROWS, SC_PART), F32)],
        compiler_params=pltpu.CompilerParams(needs_layout_passes=False), name="peer_sc_dot")
    def k(u_hbm, idx_hbm, h_hbm, act_hbm, idx_v, h_v, rows_v, act_v, sem_rows, sem_tok, sem_out, part_v):
        lane = lax.iota(jnp.int32, L)

        def compute_chunk(s, c):
            b = c % 2
            for dc in range(D_MODEL // SC_DCOLS):
                hs = [h_v[s, pl.ds(dc * SC_DCOLS + j * L, L)] for j in range(SC_DVREGS)]

                @plsc.parallel_loop(0, SC_ROWS, unroll=2)
                def _(r, dc=dc, hs=hs):
                    prods = []
                    for j in range(SC_DVREGS // 2):
                        lo, hi = _unpack_words(rows_v[b, r, pl.ds(dc * (SC_DCOLS // 2) + j * L, L)])
                        prods += [lo * hs[2 * j], hi * hs[2 * j + 1]]
                    while len(prods) > 1:
                        prods = [prods[i] + prods[i + 1] for i in range(0, len(prods), 2)]
                    part_v[r, pl.ds(dc * L, L)] = prods[0]
            for g in range(SC_ROWS // L):
                rows16 = lane + g * L

                def col_body(col, tots):
                    rot = (lane + col) & (L - 1)
                    return tuple(t + plsc.load_gather(part_v, [rows16, rot + q * L]) for q, t in enumerate(tots))

                tots = lax.fori_loop(0, L, col_body, tuple(jnp.zeros((L,), F32) for _ in range(SC_PART // L)))
                act_v[s, pl.ds(c * SC_ROWS + g * L, L)] = (tots[0] + tots[1]) + (tots[2] + tots[3])

        _sc_token_pipeline(u_hbm, idx_hbm, h_hbm, act_hbm, idx_v, h_v, rows_v, act_v,
                           sem_rows, sem_tok, sem_out, n_cores, tpw, compute_chunk)

    return k(u_packed, idx, h)


def _sc_acc(v_packed, idx, w):
    T = w.shape[0]
    mesh, n_cores, n_workers = _sc_mesh()
    tpw = T // n_workers
    L = SC_LANES

    @functools.partial(
        pl.kernel, mesh=mesh, out_type=jax.ShapeDtypeStruct((T, D_MODEL), F32),
        scratch_types=_sc_scratch(PEER_PICKS, D_MODEL),
        compiler_params=pltpu.CompilerParams(needs_layout_passes=False), name="peer_sc_acc")
    def k(v_hbm, idx_hbm, w_hbm, out_hbm, idx_v, w_v, rows_v, out_v, sem_rows, sem_tok, sem_out):
        def compute_chunk(s, c):
            b = c % 2
            for dc in range(D_MODEL // SC_DCOLS):
                cols = [pl.ds(dc * SC_DCOLS + j * L, L) for j in range(SC_DVREGS)]
                if c == 0:
                    accs = tuple(jnp.zeros((L,), F32) for _ in cols)
                else:
                    accs = tuple(out_v[s, cs] for cs in cols)

                def row_body(r, accs, dc=dc):
                    wb = plsc.load_gather(w_v.at[s], [jnp.full((L,), c * SC_ROWS, jnp.int32) + r])
                    new = []
                    for j in range(SC_DVREGS // 2):
                        lo, hi = _unpack_words(rows_v[b, r, pl.ds(dc * (SC_DCOLS // 2) + j * L, L)])
                        new += [accs[2 * j] + wb * lo, accs[2 * j + 1] + wb * hi]
                    return tuple(new)

                accs = lax.fori_loop(0, SC_ROWS, row_body, accs)
                for a, cs in zip(accs, cols):
                    out_v[s, cs] = a

        _sc_token_pipeline(v_hbm, idx_hbm, w_hbm, out_hbm, idx_v, w_v, rows_v, out_v,
                           sem_rows, sem_tok, sem_out, n_cores, tpw, compute_chunk)

    return k(v_packed, idx, w)


def _gelu_gate_kernel(a_ref, g_ref, o_ref):
    a = a_ref[...]
    o_ref[...] = 0.5 * a * (1.0 + lax.erf(a * (2.0 ** -0.5))) * g_ref[...]


def _ln2_kernel(h_ref, f_ref, g_ref, b_ref, o_ref):
    o_ref[...] = _layer_norm(ALPHA * h_ref[...] + f_ref[...], g_ref[...], b_ref[...])


def _mix(x1, idx, gates, expert_u, expert_v, ln_gain, ln_bias):
    T = x1.shape[0]
    tm = 512
    row = lambda w: pl.BlockSpec((tm, w), lambda i: (i, 0))
    const = pl.BlockSpec((1, D_MODEL), lambda i: (0, 0))
    params = pltpu.CompilerParams(dimension_semantics=("arbitrary",))
    act = _sc_dot(expert_u, idx, x1)
    w = pl.pallas_call(
        _gelu_gate_kernel, grid=(T // tm,), in_specs=[row(PEER_PICKS), row(PEER_PICKS)], out_specs=row(PEER_PICKS),
        out_shape=jax.ShapeDtypeStruct((T, PEER_PICKS), F32), compiler_params=params, name="peer_gelu_gate",
    )(act, gates)
    ffn = _sc_acc(expert_v, idx, w)
    out = pl.pallas_call(
        _ln2_kernel, grid=(T // tm,), in_specs=[row(D_MODEL), row(D_MODEL), const, const], out_specs=row(D_MODEL),
        out_shape=jax.ShapeDtypeStruct((T, D_MODEL), F32), compiler_params=params, name="peer_ln2",
    )(x1, ffn, ln_gain.reshape(1, D_MODEL), ln_bias.reshape(1, D_MODEL))
    return out


BATCH_CHUNKS = 4


def kernel(x, w_in, fox_f_bias, gla_gate_up, gla_gate_bias, gla_norm_gain, w_out_fox, w_out_gla, w_out,
           ln1_gain, ln1_bias, peer_w_query, peer_sub_keys, peer_expert_u, peer_expert_v, ln2_gain, ln2_bias):
    B, S, D = x.shape
    assert D == D_MODEL and S % max(ROUTE_TM, FOX_TQ) == 0 and w_in.shape[0] == DEPTH
    tables = [(_pack_table(peer_expert_u[l]), _pack_table(peer_expert_v[l])) for l in range(DEPTH)]
    n_chunks = BATCH_CHUNKS if B % BATCH_CHUNKS == 0 else 1
    bc = B // n_chunks
    outs = []
    for ci in range(n_chunks):
        h = x[ci * bc:(ci + 1) * bc].reshape(bc * S, D)
        for l in range(DEPTH):
            fqkv, gqk, gv, gg, mg, small = _in_proj(h, w_in[l])
            c_t = _fox_gate(small, fox_f_bias[l], bc, S)
            attn = _fox_attn(fqkv, c_t, bc, S)
            og = _gla(gqk, gv, small, gg, gla_gate_up[l], gla_gate_bias[l], gla_norm_gain[l], bc, S)
            h = _merge(attn, og, mg, h, w_out_fox[l], w_out_gla[l], w_out[l], ln1_gain[l], ln1_bias[l])
            idx, gates = _route(h, peer_w_query[l], peer_sub_keys[l])
            h = _mix(h, idx, gates, *tables[l], ln2_gain[l], ln2_bias[l])
        outs.append(h)
    return jnp.concatenate(outs, axis=0).reshape(B, S, D)
```

```python
import functools
import math

import jax
import jax.numpy as jnp
from jax import lax
from jax.experimental import pallas as pl
from jax.experimental.pallas import tpu as pltpu
from jax.experimental.pallas import tpu_sc as plsc

F32 = jnp.float32
BF16 = jnp.bfloat16

D_MODEL = 1024
FOX_HEADS = 8
FOX_HEAD_DIM = 64
FOX_WIDTH = FOX_HEADS * FOX_HEAD_DIM
GLA_HEADS = 4
GLA_KEY_DIM = 128
GLA_VAL_DIM = 256
GLA_QK_WIDTH = GLA_HEADS * GLA_KEY_DIM
GLA_V_WIDTH = GLA_HEADS * GLA_VAL_DIM
GLA_GATE_RANK = 16
GLA_GATE_TEMP = 16.0
GLA_NORM_EPS = 1e-5
GLA_CHUNK = 64
PEER_HEADS = 8
PEER_N_KEYS = 128
PEER_HALF = 128
PEER_TOPK = 16
PEER_PICKS = PEER_HEADS * PEER_TOPK
DEPTH = 1
ALPHA = (2.0 * DEPTH) ** 0.25
LN_EPS = 1e-5

LANES = 128
SUBLANES = 8
VMEM_LIMIT = 52 * 1024 * 1024

IN_SPLIT_SIZES = (FOX_WIDTH, FOX_WIDTH, FOX_WIDTH, FOX_HEADS,
                  GLA_QK_WIDTH, GLA_QK_WIDTH, GLA_V_WIDTH, GLA_V_WIDTH, GLA_GATE_RANK,
                  D_MODEL, D_MODEL)
FF_COL = 0
GLR_COL = FOX_HEADS


def _dot(a, b, **kw):
    return jnp.dot(a, b, preferred_element_type=F32, **kw)


def _dot_nt(a, b):
    return lax.dot_general(a, b, (((1,), (1,)), ((), ())), preferred_element_type=F32)


def _dot_tn(a, b):
    return lax.dot_general(a, b, (((0,), (0,)), ((), ())), preferred_element_type=F32)


def _layer_norm(y, gain, bias):
    mu = jnp.mean(y, axis=-1, keepdims=True)
    yc = y - mu
    var = jnp.mean(yc * yc, axis=-1, keepdims=True)
    return yc * lax.rsqrt(var + LN_EPS) * gain + bias


def _in_proj_kernel(x_ref, wf_ref, wgqk_ref, wgv_ref, wgg_ref, wm_ref, ws_ref,
                    f_ref, gqk_ref, gv_ref, gg_ref, m_ref, s_ref):
    xb = x_ref[...].astype(BF16)
    f_ref[...] = _dot(xb, wf_ref[...]).astype(BF16)
    gqk_ref[...] = _dot(xb, wgqk_ref[...]).astype(BF16)
    gv_ref[...] = _dot(xb, wgv_ref[...]).astype(BF16)
    gg_ref[...] = _dot(xb, wgg_ref[...])
    m_ref[...] = _dot(xb, wm_ref[...])
    s_ref[...] = _dot(xb, ws_ref[...])


def _in_proj(x2, w_in):
    T = x2.shape[0]
    tm = 256
    pts = [0]
    for s in IN_SPLIT_SIZES:
        pts.append(pts[-1] + s)
    col = lambda i, j: w_in[:, pts[i]:pts[j]]
    wf = col(0, 3).astype(BF16)
    wgqk = col(4, 6).astype(BF16)
    wgv = col(6, 7).astype(BF16)
    wgg = col(7, 8).astype(BF16)
    wm = col(9, 11).astype(BF16)
    ws = jnp.concatenate([col(3, 4), col(8, 9)], axis=1)
    ws = jnp.pad(ws, ((0, 0), (0, LANES - ws.shape[1]))).astype(BF16)
    ws_list = [wf, wgqk, wgv, wgg, wm, ws]
    out_dtypes = [BF16, BF16, BF16, F32, F32, F32]
    const = lambda w: pl.BlockSpec(w.shape, lambda i: (0, 0))
    return pl.pallas_call(
        _in_proj_kernel,
        grid=(T // tm,),
        in_specs=[pl.BlockSpec((tm, D_MODEL), lambda i: (i, 0))] + [const(w) for w in ws_list],
        out_specs=[pl.BlockSpec((tm, w.shape[1]), lambda i: (i, 0)) for w in ws_list],
        out_shape=[jax.ShapeDtypeStruct((T, w.shape[1]), dt) for w, dt in zip(ws_list, out_dtypes)],
        compiler_params=pltpu.CompilerParams(dimension_semantics=("arbitrary",), vmem_limit_bytes=VMEM_LIMIT),
        name="in_proj",
    )(x2, *ws_list)


def _fox_gate_kernel(s_ref, bias_ref, c_ref):
    S = s_ref.shape[0]
    ff_t = s_ref[...].T[FF_COL:FF_COL + FOX_HEADS, :]
    log_f = jax.nn.log_sigmoid(ff_t + bias_ref[...])
    r = lax.broadcasted_iota(jnp.int32, (LANES, LANES), 0)
    c = lax.broadcasted_iota(jnp.int32, (LANES, LANES), 1)
    tri = (r <= c).astype(F32)
    carry = jnp.zeros((FOX_HEADS, 1), F32)
    for j in range(S // LANES):
        blk = log_f[:, j * LANES:(j + 1) * LANES]
        cs = _dot(blk, tri, precision=lax.Precision.HIGHEST) + carry
        c_ref[0, :, j * LANES:(j + 1) * LANES] = cs
        carry = cs[:, LANES - 1:LANES]


def _fox_gate(small, fox_f_bias, B, S):
    return pl.pallas_call(
        _fox_gate_kernel,
        grid=(B,),
        in_specs=[pl.BlockSpec((S, LANES), lambda b: (b, 0)),
                  pl.BlockSpec((FOX_HEADS, 1), lambda b: (0, 0))],
        out_specs=pl.BlockSpec((1, FOX_HEADS, S), lambda b: (b, 0, 0)),
        out_shape=jax.ShapeDtypeStruct((B, FOX_HEADS, S), F32),
        compiler_params=pltpu.CompilerParams(dimension_semantics=("arbitrary",)),
        name="fox_gate",
    )(small, fox_f_bias.reshape(FOX_HEADS, 1))


FOX_TQ = 256
FOX_TK = FOX_TQ


def _fox_attn_kernel(q_ref, k_ref, v_ref, c_ref, o_ref):
    qi = pl.program_id(2)
    tq, tk, dh = FOX_TQ, FOX_TK, FOX_HEAD_DIM
    n_h = LANES // dh
    qs = [q_ref[:, hh * dh:(hh + 1) * dh] * (dh ** -0.5) for hh in range(n_h)]

    def step(j, carry, masked):
        ks = pl.ds(pl.multiple_of(j * tk, tk), tk)
        k2 = k_ref[ks, :]
        v2 = v_ref[ks, :]
        out = []
        for hh in range(n_h):
            m, l, acc = carry[hh]
            s = _dot_nt(qs[hh], k2[:, hh * dh:(hh + 1) * dh]) - c_ref[hh, :, ks]
            if masked:
                r = lax.broadcasted_iota(jnp.int32, (tq, tk), 0)
                c = lax.broadcasted_iota(jnp.int32, (tq, tk), 1)
                s = jnp.where(c <= r, s, -jnp.inf)
            m_new = jnp.maximum(m, jnp.max(s, axis=1, keepdims=True))
            p = jnp.exp(s - m_new)
            a = jnp.exp(m - m_new)
            l = a * l + jnp.sum(p, axis=1, keepdims=True)
            acc = a * acc + _dot(p.astype(BF16), v2[:, hh * dh:(hh + 1) * dh])
            out.append((m_new, l, acc))
        return tuple(out)

    init = tuple((jnp.full((tq, 1), -jnp.inf, F32), jnp.zeros((tq, 1), F32), jnp.zeros((tq, dh), F32))
                 for _ in range(n_h))
    carry = lax.fori_loop(0, qi, lambda j, c: step(j, c, False), init)
    carry = step(qi, carry, True)
    o_ref[...] = jnp.concatenate([acc / l for _, l, acc in carry], axis=1).astype(BF16)


def _fox_attn(fqkv, c_t, B, S):
    T = B * S
    nq = S // FOX_TQ
    n_hp = FOX_WIDTH // LANES
    return pl.pallas_call(
        _fox_attn_kernel,
        grid=(B, n_hp, nq),
        in_specs=[pl.BlockSpec((FOX_TQ, LANES), lambda b, h, i: (b * nq + i, h)),
                  pl.BlockSpec((S, LANES), lambda b, h, i: (b, n_hp + h)),
                  pl.BlockSpec((S, LANES), lambda b, h, i: (b, 2 * n_hp + h)),
                  pl.BlockSpec((LANES // FOX_HEAD_DIM, 1, S), lambda b, h, i: (b * n_hp + h, 0, 0))],
        out_specs=pl.BlockSpec((FOX_TQ, LANES), lambda b, h, i: (b * nq + i, h)),
        out_shape=jax.ShapeDtypeStruct((T, FOX_WIDTH), BF16),
        compiler_params=pltpu.CompilerParams(dimension_semantics=("arbitrary", "arbitrary", "arbitrary")),
        name="fox_attn",
    )(fqkv, fqkv, fqkv, c_t.reshape(B * FOX_HEADS, 1, S))


def _gla_kernel(q_ref, k_ref, v_ref, s_ref, up_ref, gb_ref, gg_ref, gain_ref, o_ref):
    S = q_ref.shape[0]
    C = GLA_CHUNK
    r = lax.broadcasted_iota(jnp.int32, (C, C), 0)
    c = lax.broadcasted_iota(jnp.int32, (C, C), 1)
    tril = (r >= c).astype(F32)

    def body(ci, st_t):
        rows = pl.ds(pl.multiple_of(ci * C, C), C)
        z = _dot(s_ref[rows, :].astype(BF16), up_ref[...]) + gb_ref[...]
        la = jax.nn.log_sigmoid(z) * (1.0 / GLA_GATE_TEMP)
        cum = _dot(tril, la, precision=lax.Precision.HIGHEST)
        tot = cum[C - 1:C, :]
        kd = (k_ref[rows, :].astype(F32) * jnp.exp(tot - cum)).astype(BF16)
        st_t = st_t * jnp.exp(tot) + _dot_tn(v_ref[rows, :], kd)
        o = _dot_nt(q_ref[rows, :], st_t.astype(BF16)) * (GLA_KEY_DIM ** -0.5)
        o = o * lax.rsqrt(jnp.mean(o * o, axis=-1, keepdims=True) + GLA_NORM_EPS) * gain_ref[...]
        o = o * jax.nn.silu(gg_ref[rows, :])
        o_ref[rows, :] = o.astype(BF16)
        return st_t

    lax.fori_loop(0, S // C, body, jnp.zeros((GLA_VAL_DIM, GLA_KEY_DIM), F32))


def _gla(gqk, gv, small, gg, gla_gate_up, gla_gate_bias, gla_norm_gain, B, S):
    T = B * S
    up = jnp.zeros((LANES, GLA_QK_WIDTH), F32).at[GLR_COL:GLR_COL + GLA_GATE_RANK].set(gla_gate_up).astype(BF16)
    gb = gla_gate_bias.reshape(1, GLA_QK_WIDTH)
    gain = gla_norm_gain.reshape(1, GLA_V_WIDTH)
    H = GLA_HEADS
    return pl.pallas_call(
        _gla_kernel,
        grid=(B, H),
        in_specs=[pl.BlockSpec((S, GLA_KEY_DIM), lambda b, h: (b, h)),
                  pl.BlockSpec((S, GLA_KEY_DIM), lambda b, h: (b, H + h)),
                  pl.BlockSpec((S, GLA_VAL_DIM), lambda b, h: (b, h)),
                  pl.BlockSpec((S, LANES), lambda b, h: (b, 0)),
                  pl.BlockSpec((LANES, GLA_KEY_DIM), lambda b, h: (0, h)),
                  pl.BlockSpec((1, GLA_KEY_DIM), lambda b, h: (0, h)),
                  pl.BlockSpec((S, GLA_VAL_DIM), lambda b, h: (b, h)),
                  pl.BlockSpec((1, GLA_VAL_DIM), lambda b, h: (0, h))],
        out_specs=pl.BlockSpec((S, GLA_VAL_DIM), lambda b, h: (b, h)),
        out_shape=jax.ShapeDtypeStruct((T, GLA_V_WIDTH), BF16),
        compiler_params=pltpu.CompilerParams(dimension_semantics=("arbitrary", "arbitrary")),
        name="gla",
    )(gqk, gqk, gv, small, up, gb, gg, gain)


def _merge_kernel(a_ref, og_ref, m_ref, x_ref, wf_ref, wg_ref, wo_ref, g_ref, b_ref, o_ref):
    y_fox = _dot(a_ref[...], wf_ref[...])
    y_gla = _dot(og_ref[...], wg_ref[...])
    merged = (jax.nn.sigmoid(m_ref[:, :D_MODEL]) * y_fox + jax.nn.sigmoid(m_ref[:, D_MODEL:]) * y_gla)
    mix = _dot(merged.astype(BF16), wo_ref[...])
    o_ref[...] = _layer_norm(ALPHA * x_ref[...] + mix, g_ref[...], b_ref[...])


def _merge(attn, og, mg, x2, w_out_fox, w_out_gla, w_out, ln_gain, ln_bias):
    T = x2.shape[0]
    tm = 256
    row = lambda w: pl.BlockSpec((tm, w), lambda i: (i, 0))
    const = lambda a: pl.BlockSpec(a.shape, lambda i: (0, 0))
    ws = [w_out_fox.astype(BF16), w_out_gla.astype(BF16), w_out.astype(BF16),
          ln_gain.reshape(1, D_MODEL), ln_bias.reshape(1, D_MODEL)]
    return pl.pallas_call(
        _merge_kernel,
        grid=(T // tm,),
        in_specs=[row(FOX_WIDTH), row(GLA_V_WIDTH), row(2 * D_MODEL), row(D_MODEL)] + [const(w) for w in ws],
        out_specs=row(D_MODEL),
        out_shape=jax.ShapeDtypeStruct((T, D_MODEL), F32),
        compiler_params=pltpu.CompilerParams(dimension_semantics=("arbitrary",), vmem_limit_bytes=VMEM_LIMIT),
        name="merge",
    )(attn, og, mg, x2, *ws)


ROUTE_TM = 256
CAND_BLOCKS = 10


def _sublane_all(x, op):
    for shift in (4, 2, 1):
        x = op(x, pltpu.roll(x, shift, axis=0))
    return x


def _take_max(s3, iota3, sentinel):
    m8 = _sublane_all(jnp.max(s3, axis=0), jnp.maximum)
    idx8 = _sublane_all(jnp.min(jnp.where(s3 == m8[None], iota3, sentinel), axis=0), jnp.minimum)
    return m8, idx8, iota3 == idx8[None]


def _row_iota(groups, n):
    shape = (groups, SUBLANES, n)
    return lax.broadcasted_iota(jnp.int32, shape, 0) * SUBLANES + lax.broadcasted_iota(jnp.int32, shape, 1)


def _route_kernel(x_ref, wq_ref, keys_ref, after_ref, idx_ref, gate_ref,
                  q_scr, st_scr, it_scr, best_scr, pick_scr, gsel_scr):
    del after_ref
    tm = ROUTE_TM
    K = PEER_TOPK
    q_scr[...] = _dot(x_ref[...].astype(BF16), wq_ref[...])
    key_iota = _row_iota(PEER_N_KEYS // SUBLANES, tm)

    def stage1(hp, _):
        q = q_scr[:, pl.ds(pl.multiple_of(hp * PEER_HALF, PEER_HALF), PEER_HALF)].astype(BF16)
        s = _dot_nt(keys_ref[hp], q).reshape(PEER_N_KEYS // SUBLANES, SUBLANES, tm)
        for i in range(K):
            m8, idx8, hit = _take_max(s, key_iota, PEER_N_KEYS)
            st_scr[hp, i:i + 1, :] = m8[0:1]
            it_scr[hp, i:i + 1, :] = idx8[0:1]
            s = jnp.where(hit, -jnp.inf, s)
        return 0

    lax.fori_loop(0, 2 * PEER_HEADS, stage1, 0)

    cand_iota = _row_iota(CAND_BLOCKS, tm)

    def stage2(h, _):
        s0, s1 = st_scr[2 * h], st_scr[2 * h + 1]
        i0, i1 = it_scr[2 * h], it_scr[2 * h + 1]
        lo, hi = slice(0, SUBLANES), slice(SUBLANES, 2 * SUBLANES)
        cs = [s0[0:1] + s1[lo], s0[0:1] + s1[hi]]
        ci = [i0[0:1] * PEER_N_KEYS + i1[lo], i0[0:1] * PEER_N_KEYS + i1[hi]]
        for a in range(1, SUBLANES):
            cs.append(s0[a:a + 1] + s1[lo])
            ci.append(i0[a:a + 1] * PEER_N_KEYS + i1[lo])
        cs.append(s0[hi] + s1[0:1])
        ci.append(i0[hi] * PEER_N_KEYS + i1[0:1])
        cand = jnp.stack(cs, axis=0)
        cidx = jnp.stack(ci, axis=0)
        for i in range(K):
            m8, _, hit = _take_max(cand, cand_iota, CAND_BLOCKS * SUBLANES)
            pick8 = _sublane_all(jnp.max(jnp.where(hit, cidx, -1), axis=0), jnp.maximum)
            best_scr[i:i + 1, :] = m8[0:1]
            pick_scr[pl.ds(h * K + i, 1), :] = pick8[0:1]
            cand = jnp.where(hit, -jnp.inf, cand)
        best = best_scr[...]
        e = jnp.exp(best - best[0:1])
        gsel_scr[pl.ds(pl.multiple_of(h * K, K), K), :] = e / jnp.sum(e, axis=0, keepdims=True)
        return 0

    lax.fori_loop(0, PEER_HEADS, stage2, 0)
    idx_ref[...] = pick_scr[...].T
    gate_ref[...] = gsel_scr[...].T


def _route(x1, peer_w_query, peer_sub_keys, after):
    T = x1.shape[0]
    tm = ROUTE_TM
    wq = peer_w_query.reshape(D_MODEL, 2 * PEER_HEADS * PEER_HALF).astype(BF16)
    keys = peer_sub_keys.reshape(2 * PEER_HEADS, PEER_N_KEYS, PEER_HALF).astype(BF16)
    picks = pl.BlockSpec((tm, PEER_PICKS), lambda i: (i, 0))
    return pl.pallas_call(
        _route_kernel,
        grid=(T // tm,),
        in_specs=[pl.BlockSpec((tm, D_MODEL), lambda i: (i, 0)),
                  pl.BlockSpec(wq.shape, lambda i: (0, 0)),
                  pl.BlockSpec(keys.shape, lambda i: (0, 0, 0)),
                  pl.BlockSpec((SUBLANES, LANES), lambda i: (0, 0))],
        out_specs=[picks, picks],
        out_shape=[jax.ShapeDtypeStruct((T, PEER_PICKS), jnp.int32),
                   jax.ShapeDtypeStruct((T, PEER_PICKS), F32)],
        scratch_shapes=[pltpu.VMEM((tm, 2 * PEER_HEADS * PEER_HALF), F32),
                        pltpu.VMEM((2 * PEER_HEADS, PEER_TOPK, tm), F32),
                        pltpu.VMEM((2 * PEER_HEADS, PEER_TOPK, tm), jnp.int32),
                        pltpu.VMEM((PEER_TOPK, tm), F32),
                        pltpu.VMEM((PEER_PICKS, tm), jnp.int32),
                        pltpu.VMEM((PEER_PICKS, tm), F32)],
        compiler_params=pltpu.CompilerParams(dimension_semantics=("arbitrary",), vmem_limit_bytes=VMEM_LIMIT),
        name="peer_route",
    )(x1, wq, keys, after)


SC_LANES = 16
SC_ROWS = 64
SC_CHUNKS = PEER_PICKS // SC_ROWS
SC_DCOLS = 256
SC_DVREGS = SC_DCOLS // SC_LANES
SC_PART = (D_MODEL // SC_DCOLS) * SC_LANES


def _sc_mesh():
    info = plsc.get_sparse_core_info()
    mesh = plsc.VectorSubcoreMesh(core_axis_name="c", subcore_axis_name="s")
    return mesh, info.num_cores, info.num_cores * info.num_subcores


def _sc_token_pipeline(tab_hbm, idx_hbm, vec_hbm, out_hbm, idx_v, vec_v, rows_v, out_v,
                       sem_rows, sem_tok, sem_out, n_cores, tpw, compute_chunk):
    wid = lax.axis_index("s") * n_cores + lax.axis_index("c")
    base = wid * tpw

    def gather(s, c):
        return pltpu.make_async_copy(tab_hbm.at[idx_v.at[s, pl.ds(c * SC_ROWS, SC_ROWS)]],
                                     rows_v.at[c % 2], sem_rows.at[c % 2])

    def tok_fetch(s, tok):
        return (pltpu.make_async_copy(idx_hbm.at[tok], idx_v.at[s], sem_tok.at[0]),
                pltpu.make_async_copy(vec_hbm.at[tok], vec_v.at[s], sem_tok.at[1]))

    def out_copy(s, tok):
        return pltpu.make_async_copy(out_v.at[s], out_hbm.at[tok], sem_out.at[s])

    for d in tok_fetch(0, base):
        d.start()
    for d in tok_fetch(0, base):
        d.wait()
    gather(0, 0).start()

    def pair_body(tp, _):
        for s in range(2):
            t = tp * 2 + s
            tok = base + t
            has_next = t + 1 < tpw

            @pl.when(has_next)
            def _():
                for d in tok_fetch(1 - s, tok + 1):
                    d.start()

            @pl.when(t >= 2)
            def _():
                out_copy(s, tok - 2).wait()

            for c in range(SC_CHUNKS):
                if c + 1 < SC_CHUNKS:
                    gather(s, c + 1).start()
                else:
                    @pl.when(has_next)
                    def _():
                        for d in tok_fetch(1 - s, tok + 1):
                            d.wait()
                        gather(1 - s, 0).start()
                gather(s, c).wait()
                compute_chunk(s, c)
            out_copy(s, tok).start()
        return 0

    lax.fori_loop(0, tpw // 2, pair_body, 0)
    for s in range(2):
        out_copy(s, base + tpw - 2 + s).wait()


def _sc_scratch(vec_len, out_len):
    return [pltpu.VMEM((2, PEER_PICKS), jnp.int32),
            pltpu.VMEM((2, vec_len), F32),
            pltpu.VMEM((2, SC_ROWS, D_MODEL // 2), jnp.int32),
            pltpu.VMEM((2, out_len), F32),
            pltpu.SemaphoreType.DMA((2,)),
            pltpu.SemaphoreType.DMA((2,)),
            pltpu.SemaphoreType.DMA((2,))]


def _pack_table(tab):
    n, d = tab.shape
    bits = lax.bitcast_convert_type(tab.astype(BF16), jnp.uint16).astype(jnp.uint32).reshape(n, d // 32, 2, SC_LANES)
    words = bits[:, :, 0, :] | (bits[:, :, 1, :] << 16)
    return lax.bitcast_convert_type(words, jnp.int32).reshape(n, d // 2)


def _unpack_words(x):
    return plsc.bitcast(x << 16, F32), plsc.bitcast(x & jnp.int32(-65536), F32)


def _sc_dot(u_packed, idx, h):
    T = h.shape[0]
    mesh, n_cores, n_workers = _sc_mesh()
    tpw = T // n_workers
    L = SC_LANES

    @functools.partial(
        pl.kernel, mesh=mesh, out_type=jax.ShapeDtypeStruct((T, PEER_PICKS), F32),
        scratch_types=_sc_scratch(D_MODEL, PEER_PICKS) + [pltpu.VMEM((SC_ROWS, SC_PART), F32)],
        compiler_params=pltpu.CompilerParams(needs_layout_passes=False), name="peer_sc_dot")
    def k(u_hbm, idx_hbm, h_hbm, act_hbm, idx_v, h_v, rows_v, act_v, sem_rows, sem_tok, sem_out, part_v):
        lane = lax.iota(jnp.int32, L)

        def compute_chunk(s, c):
            b = c % 2
            for dc in range(D_MODEL // SC_DCOLS):
                hs = [h_v[s, pl.ds(dc * SC_DCOLS + j * L, L)] for j in range(SC_DVREGS)]

                @plsc.parallel_loop(0, SC_ROWS, unroll=2)
                def _(r, dc=dc, hs=hs):
                    prods = []
                    for j in range(SC_DVREGS // 2):
                        lo, hi = _unpack_words(rows_v[b, r, pl.ds(dc * (SC_DCOLS // 2) + j * L, L)])
                        prods += [lo * hs[2 * j], hi * hs[2 * j + 1]]
                    while len(prods) > 1:
                        prods = [prods[i] + prods[i + 1] for i in range(0, len(prods), 2)]
                    part_v[r, pl.ds(dc * L, L)] = prods[0]
            for g in range(SC_ROWS // L):
                rows16 = lane + g * L

                def col_body(col, tots):
                    rot = (lane + col) & (L - 1)
                    return tuple(t + plsc.load_gather(part_v, [rows16, rot + q * L]) for q, t in enumerate(tots))

                tots = lax.fori_loop(0, L, col_body, tuple(jnp.zeros((L,), F32) for _ in range(SC_PART // L)))
                act_v[s, pl.ds(c * SC_ROWS + g * L, L)] = (tots[0] + tots[1]) + (tots[2] + tots[3])

        _sc_token_pipeline(u_hbm, idx_hbm, h_hbm, act_hbm, idx_v, h_v, rows_v, act_v,
                           sem_rows, sem_tok, sem_out, n_cores, tpw, compute_chunk)

    return k(u_packed, idx, h)


def _sc_acc(v_packed, idx, w):
    T = w.shape[0]
    mesh, n_cores, n_workers = _sc_mesh()
    tpw = T // n_workers
    L = SC_LANES

    @functools.partial(
        pl.kernel, mesh=mesh, out_type=jax.ShapeDtypeStruct((T, D_MODEL), F32),
        scratch_types=_sc_scratch(PEER_PICKS, D_MODEL),
        compiler_params=pltpu.CompilerParams(needs_layout_passes=False), name="peer_sc_acc")
    def k(v_hbm, idx_hbm, w_hbm, out_hbm, idx_v, w_v, rows_v, out_v, sem_rows, sem_tok, sem_out):
        def compute_chunk(s, c):
            b = c % 2
            for dc in range(D_MODEL // SC_DCOLS):
                cols = [pl.ds(dc * SC_DCOLS + j * L, L) for j in range(SC_DVREGS)]
                if c == 0:
                    accs = tuple(jnp.zeros((L,), F32) for _ in cols)
                else:
                    accs = tuple(out_v[s, cs] for cs in cols)

                def row_body(r, accs, dc=dc):
                    wb = plsc.load_gather(w_v.at[s], [jnp.full((L,), c * SC_ROWS, jnp.int32) + r])
                    new = []
                    for j in range(SC_DVREGS // 2):
                        lo, hi = _unpack_words(rows_v[b, r, pl.ds(dc * (SC_DCOLS // 2) + j * L, L)])
                        new += [accs[2 * j] + wb * lo, accs[2 * j + 1] + wb * hi]
                    return tuple(new)

                accs = lax.fori_loop(0, SC_ROWS, row_body, accs)
                for a, cs in zip(accs, cols):
                    out_v[s, cs] = a

        _sc_token_pipeline(v_hbm, idx_hbm, w_hbm, out_hbm, idx_v, w_v, rows_v, out_v,
                           sem_rows, sem_tok, sem_out, n_cores, tpw, compute_chunk)

    return k(v_packed, idx, w)


def _gelu_gate_kernel(a_ref, g_ref, o_ref):
    a = a_ref[...]
    o_ref[...] = 0.5 * a * (1.0 + lax.erf(a * (2.0 ** -0.5))) * g_ref[...]


def _ln2_kernel(h_ref, f_ref, g_ref, b_ref, o_ref):
    o_ref[...] = _layer_norm(ALPHA * h_ref[...] + f_ref[...], g_ref[...], b_ref[...])


MIX_TM = 512
_MIX_PARAMS = pltpu.CompilerParams(dimension_semantics=("arbitrary",))


def _mix_row(width):
    return pl.BlockSpec((MIX_TM, width), lambda i: (i, 0))


def _gelu_gate_call(act, gates):
    T = act.shape[0]
    return pl.pallas_call(
        _gelu_gate_kernel, grid=(T // MIX_TM,), in_specs=[_mix_row(PEER_PICKS)] * 2, out_specs=_mix_row(PEER_PICKS),
        out_shape=jax.ShapeDtypeStruct((T, PEER_PICKS), F32), compiler_params=_MIX_PARAMS, name="peer_gelu_gate",
    )(act, gates)


def _mix_finish(x1, idx, w, v_packed, ln_gain, ln_bias):
    T = x1.shape[0]
    const = pl.BlockSpec((1, D_MODEL), lambda i: (0, 0))
    ffn = _sc_acc(v_packed, idx, w)
    out = pl.pallas_call(
        _ln2_kernel, grid=(T // MIX_TM,), in_specs=[_mix_row(D_MODEL)] * 2 + [const, const],
        out_specs=_mix_row(D_MODEL), out_shape=jax.ShapeDtypeStruct((T, D_MODEL), F32),
        compiler_params=_MIX_PARAMS, name="peer_ln2",
    )(x1, ffn, ln_gain.reshape(1, D_MODEL), ln_bias.reshape(1, D_MODEL))
    return out, ffn


BATCH_CHUNKS = 16
SC_PASS_LAG = 2


def kernel(x, w_in, fox_f_bias, gla_gate_up, gla_gate_bias, gla_norm_gain, w_out_fox, w_out_gla, w_out,
           ln1_gain, ln1_bias, peer_w_query, peer_sub_keys, peer_expert_u, peer_expert_v, ln2_gain, ln2_bias):
    B, S, D = x.shape
    assert D == D_MODEL and S % max(ROUTE_TM, FOX_TQ) == 0
    assert DEPTH == 1 and w_in.shape[0] == 1, "the chunk pipeline below is written for the single-layer block"
    u_packed, v_packed = _pack_table(peer_expert_u[0]), _pack_table(peer_expert_v[0])
    n_chunks = BATCH_CHUNKS if B % BATCH_CHUNKS == 0 else 1
    bc = B // n_chunks
    outs, ffns = [], []
    for ci in range(n_chunks):
        h = x[ci * bc:(ci + 1) * bc].reshape(bc * S, D)
        fqkv, gqk, gv, gg, mg, small = _in_proj(h, w_in[0])
        c_t = _fox_gate(small, fox_f_bias[0], bc, S)
        attn = _fox_attn(fqkv, c_t, bc, S)
        og = _gla(gqk, gv, small, gg, gla_gate_up[0], gla_gate_bias[0], gla_norm_gain[0], bc, S)
        x1 = _merge(attn, og, mg, h, w_out_fox[0], w_out_gla[0], w_out[0], ln1_gain[0], ln1_bias[0])
        after = ffns[ci - SC_PASS_LAG][:SUBLANES, :LANES] if ci >= SC_PASS_LAG else jnp.zeros((SUBLANES, LANES), F32)
        idx, gates = _route(x1, peer_w_query[0], peer_sub_keys[0], after)
        w = _gelu_gate_call(_sc_dot(u_packed, idx, x1), gates)
        out, ffn = _mix_finish(x1, idx, w, v_packed, ln2_gain[0], ln2_bias[0])
        outs.append(out)
        ffns.append(ffn)
    return jnp.concatenate(outs, axis=0).reshape(B, S, D)
```

```python
import functools
import math

import jax
import jax.numpy as jnp
from jax import lax
from jax.experimental import pallas as pl
from jax.experimental.pallas import tpu as pltpu
from jax.experimental.pallas import tpu_sc as plsc

F32 = jnp.float32
BF16 = jnp.bfloat16

D_MODEL = 1024
FOX_HEADS = 8
FOX_HEAD_DIM = 64
FOX_WIDTH = FOX_HEADS * FOX_HEAD_DIM
GLA_HEADS = 4
GLA_KEY_DIM = 128
GLA_VAL_DIM = 256
GLA_QK_WIDTH = GLA_HEADS * GLA_KEY_DIM
GLA_V_WIDTH = GLA_HEADS * GLA_VAL_DIM
GLA_GATE_RANK = 16
GLA_GATE_TEMP = 16.0
GLA_NORM_EPS = 1e-5
GLA_CHUNK = 64
PEER_HEADS = 8
PEER_N_KEYS = 128
PEER_HALF = 128
PEER_TOPK = 16
PEER_PICKS = PEER_HEADS * PEER_TOPK
DEPTH = 1
ALPHA = (2.0 * DEPTH) ** 0.25
LN_EPS = 1e-5

LANES = 128
SUBLANES = 8
VMEM_LIMIT = 52 * 1024 * 1024

IN_SPLIT_SIZES = (FOX_WIDTH, FOX_WIDTH, FOX_WIDTH, FOX_HEADS,
                  GLA_QK_WIDTH, GLA_QK_WIDTH, GLA_V_WIDTH, GLA_V_WIDTH, GLA_GATE_RANK,
                  D_MODEL, D_MODEL)
FF_COL = 0
GLR_COL = FOX_HEADS


def _dot(a, b, **kw):
    return jnp.dot(a, b, preferred_element_type=F32, **kw)


def _dot_nt(a, b):
    return lax.dot_general(a, b, (((1,), (1,)), ((), ())), preferred_element_type=F32)


def _dot_tn(a, b):
    return lax.dot_general(a, b, (((0,), (0,)), ((), ())), preferred_element_type=F32)


def _layer_norm(y, gain, bias):
    mu = jnp.mean(y, axis=-1, keepdims=True)
    yc = y - mu
    var = jnp.mean(yc * yc, axis=-1, keepdims=True)
    return yc * lax.rsqrt(var + LN_EPS) * gain + bias


def _in_proj_kernel(x_ref, wf_ref, wgqk_ref, wgv_ref, wgg_ref, wm_ref, ws_ref,
                    f_ref, gqk_ref, gv_ref, gg_ref, m_ref, s_ref):
    xb = x_ref[...].astype(BF16)
    f_ref[...] = _dot(xb, wf_ref[...]).astype(BF16)
    gqk_ref[...] = _dot(xb, wgqk_ref[...]).astype(BF16)
    gv_ref[...] = _dot(xb, wgv_ref[...]).astype(BF16)
    gg_ref[...] = _dot(xb, wgg_ref[...])
    m_ref[...] = _dot(xb, wm_ref[...])
    s_ref[...] = _dot(xb, ws_ref[...])


def _in_proj(x2, w_in):
    T = x2.shape[0]
    tm = 256
    pts = [0]
    for s in IN_SPLIT_SIZES:
        pts.append(pts[-1] + s)
    col = lambda i, j: w_in[:, pts[i]:pts[j]]
    wf = col(0, 3).astype(BF16)
    wgqk = col(4, 6).astype(BF16)
    wgv = col(6, 7).astype(BF16)
    wgg = col(7, 8).astype(BF16)
    wm = col(9, 11).astype(BF16)
    ws = jnp.concatenate([col(3, 4), col(8, 9)], axis=1)
    ws = jnp.pad(ws, ((0, 0), (0, LANES - ws.shape[1]))).astype(BF16)
    ws_list = [wf, wgqk, wgv, wgg, wm, ws]
    out_dtypes = [BF16, BF16, BF16, F32, F32, F32]
    const = lambda w: pl.BlockSpec(w.shape, lambda i: (0, 0))
    return pl.pallas_call(
        _in_proj_kernel,
        grid=(T // tm,),
        in_specs=[pl.BlockSpec((tm, D_MODEL), lambda i: (i, 0))] + [const(w) for w in ws_list],
        out_specs=[pl.BlockSpec((tm, w.shape[1]), lambda i: (i, 0)) for w in ws_list],
        out_shape=[jax.ShapeDtypeStruct((T, w.shape[1]), dt) for w, dt in zip(ws_list, out_dtypes)],
        compiler_params=pltpu.CompilerParams(dimension_semantics=("arbitrary",), vmem_limit_bytes=VMEM_LIMIT),
        name="in_proj",
    )(x2, *ws_list)


def _fox_gate_kernel(s_ref, bias_ref, c_ref):
    S = s_ref.shape[0]
    ff_t = s_ref[...].T[FF_COL:FF_COL + FOX_HEADS, :]
    log_f = jax.nn.log_sigmoid(ff_t + bias_ref[...])
    r = lax.broadcasted_iota(jnp.int32, (LANES, LANES), 0)
    c = lax.broadcasted_iota(jnp.int32, (LANES, LANES), 1)
    tri = (r <= c).astype(F32)
    carry = jnp.zeros((FOX_HEADS, 1), F32)
    for j in range(S // LANES):
        blk = log_f[:, j * LANES:(j + 1) * LANES]
        cs = _dot(blk, tri, precision=lax.Precision.HIGHEST) + carry
        c_ref[0, :, j * LANES:(j + 1) * LANES] = cs
        carry = cs[:, LANES - 1:LANES]


def _fox_gate(small, fox_f_bias, B, S):
    return pl.pallas_call(
        _fox_gate_kernel,
        grid=(B,),
        in_specs=[pl.BlockSpec((S, LANES), lambda b: (b, 0)),
                  pl.BlockSpec((FOX_HEADS, 1), lambda b: (0, 0))],
        out_specs=pl.BlockSpec((1, FOX_HEADS, S), lambda b: (b, 0, 0)),
        out_shape=jax.ShapeDtypeStruct((B, FOX_HEADS, S), F32),
        compiler_params=pltpu.CompilerParams(dimension_semantics=("arbitrary",)),
        name="fox_gate",
    )(small, fox_f_bias.reshape(FOX_HEADS, 1))


FOX_TQ = 256
FOX_TK = FOX_TQ


def _fox_attn_kernel(q_ref, k_ref, v_ref, c_ref, o_ref):
    qi = pl.program_id(2)
    tq, tk, dh = FOX_TQ, FOX_TK, FOX_HEAD_DIM
    n_h = LANES // dh
    qs = [q_ref[:, hh * dh:(hh + 1) * dh] * (dh ** -0.5) for hh in range(n_h)]

    def step(j, carry, masked):
        ks = pl.ds(pl.multiple_of(j * tk, tk), tk)
        k2 = k_ref[ks, :]
        v2 = v_ref[ks, :]
        out = []
        for hh in range(n_h):
            m, l, acc = carry[hh]
            s = _dot_nt(qs[hh], k2[:, hh * dh:(hh + 1) * dh]) - c_ref[hh, :, ks]
            if masked:
                r = lax.broadcasted_iota(jnp.int32, (tq, tk), 0)
                c = lax.broadcasted_iota(jnp.int32, (tq, tk), 1)
                s = jnp.where(c <= r, s, -jnp.inf)
            m_new = jnp.maximum(m, jnp.max(s, axis=1, keepdims=True))
            p = jnp.exp(s - m_new)
            a = jnp.exp(m - m_new)
            l = a * l + jnp.sum(p, axis=1, keepdims=True)
            acc = a * acc + _dot(p.astype(BF16), v2[:, hh * dh:(hh + 1) * dh])
            out.append((m_new, l, acc))
        return tuple(out)

    init = tuple((jnp.full((tq, 1), -jnp.inf, F32), jnp.zeros((tq, 1), F32), jnp.zeros((tq, dh), F32))
                 for _ in range(n_h))
    carry = lax.fori_loop(0, qi, lambda j, c: step(j, c, False), init)
    carry = step(qi, carry, True)
    o_ref[...] = jnp.concatenate([acc / l for _, l, acc in carry], axis=1).astype(BF16)


def _fox_attn(fqkv, c_t, B, S):
    T = B * S
    nq = S // FOX_TQ
    n_hp = FOX_WIDTH // LANES
    return pl.pallas_call(
        _fox_attn_kernel,
        grid=(B, n_hp, nq),
        in_specs=[pl.BlockSpec((FOX_TQ, LANES), lambda b, h, i: (b * nq + i, h)),
                  pl.BlockSpec((S, LANES), lambda b, h, i: (b, n_hp + h)),
                  pl.BlockSpec((S, LANES), lambda b, h, i: (b, 2 * n_hp + h)),
                  pl.BlockSpec((LANES // FOX_HEAD_DIM, 1, S), lambda b, h, i: (b * n_hp + h, 0, 0))],
        out_specs=pl.BlockSpec((FOX_TQ, LANES), lambda b, h, i: (b * nq + i, h)),
        out_shape=jax.ShapeDtypeStruct((T, FOX_WIDTH), BF16),
        compiler_params=pltpu.CompilerParams(dimension_semantics=("arbitrary", "arbitrary", "arbitrary")),
        name="fox_attn",
    )(fqkv, fqkv, fqkv, c_t.reshape(B * FOX_HEADS, 1, S))


def _gla_kernel(q_ref, k_ref, v_ref, s_ref, up_ref, gb_ref, gg_ref, gain_ref, o_ref):
    S = q_ref.shape[0]
    C = GLA_CHUNK
    r = lax.broadcasted_iota(jnp.int32, (C, C), 0)
    c = lax.broadcasted_iota(jnp.int32, (C, C), 1)
    tril = (r >= c).astype(F32)

    def body(ci, st_t):
        rows = pl.ds(pl.multiple_of(ci * C, C), C)
        z = _dot(s_ref[rows, :].astype(BF16), up_ref[...]) + gb_ref[...]
        la = jax.nn.log_sigmoid(z) * (1.0 / GLA_GATE_TEMP)
        cum = _dot(tril, la, precision=lax.Precision.HIGHEST)
        tot = cum[C - 1:C, :]
        kd = (k_ref[rows, :].astype(F32) * jnp.exp(tot - cum)).astype(BF16)
        st_t = st_t * jnp.exp(tot) + _dot_tn(v_ref[rows, :], kd)
        o = _dot_nt(q_ref[rows, :], st_t.astype(BF16)) * (GLA_KEY_DIM ** -0.5)
        o = o * lax.rsqrt(jnp.mean(o * o, axis=-1, keepdims=True) + GLA_NORM_EPS) * gain_ref[...]
        o = o * jax.nn.silu(gg_ref[rows, :])
        o_ref[rows, :] = o.astype(BF16)
        return st_t

    lax.fori_loop(0, S // C, body, jnp.zeros((GLA_VAL_DIM, GLA_KEY_DIM), F32))


def _gla(gqk, gv, small, gg, gla_gate_up, gla_gate_bias, gla_norm_gain, B, S):
    T = B * S
    up = jnp.zeros((LANES, GLA_QK_WIDTH), F32).at[GLR_COL:GLR_COL + GLA_GATE_RANK].set(gla_gate_up).astype(BF16)
    gb = gla_gate_bias.reshape(1, GLA_QK_WIDTH)
    gain = gla_norm_gain.reshape(1, GLA_V_WIDTH)
    H = GLA_HEADS
    return pl.pallas_call(
        _gla_kernel,
        grid=(B, H),
        in_specs=[pl.BlockSpec((S, GLA_KEY_DIM), lambda b, h: (b, h)),
                  pl.BlockSpec((S, GLA_KEY_DIM), lambda b, h: (b, H + h)),
                  pl.BlockSpec((S, GLA_VAL_DIM), lambda b, h: (b, h)),
                  pl.BlockSpec((S, LANES), lambda b, h: (b, 0)),
                  pl.BlockSpec((LANES, GLA_KEY_DIM), lambda b, h: (0, h)),
                  pl.BlockSpec((1, GLA_KEY_DIM), lambda b, h: (0, h)),
                  pl.BlockSpec((S, GLA_VAL_DIM), lambda b, h: (b, h)),
                  pl.BlockSpec((1, GLA_VAL_DIM), lambda b, h: (0, h))],
        out_specs=pl.BlockSpec((S, GLA_VAL_DIM), lambda b, h: (b, h)),
        out_shape=jax.ShapeDtypeStruct((T, GLA_V_WIDTH), BF16),
        compiler_params=pltpu.CompilerParams(dimension_semantics=("arbitrary", "arbitrary")),
        name="gla",
    )(gqk, gqk, gv, small, up, gb, gg, gain)


def _merge_kernel(a_ref, og_ref, m_ref, x_ref, wf_ref, wg_ref, wo_ref, g_ref, b_ref, o_ref):
    y_fox = _dot(a_ref[...], wf_ref[...])
    y_gla = _dot(og_ref[...], wg_ref[...])
    merged = (jax.nn.sigmoid(m_ref[:, :D_MODEL]) * y_fox + jax.nn.sigmoid(m_ref[:, D_MODEL:]) * y_gla)
    mix = _dot(merged.astype(BF16), wo_ref[...])
    o_ref[...] = _layer_norm(ALPHA * x_ref[...] + mix, g_ref[...], b_ref[...])


def _merge(attn, og, mg, x2, w_out_fox, w_out_gla, w_out, ln_gain, ln_bias):
    T = x2.shape[0]
    tm = 256
    row = lambda w: pl.BlockSpec((tm, w), lambda i: (i, 0))
    const = lambda a: pl.BlockSpec(a.shape, lambda i: (0, 0))
    ws = [w_out_fox.astype(BF16), w_out_gla.astype(BF16), w_out.astype(BF16),
          ln_gain.reshape(1, D_MODEL), ln_bias.reshape(1, D_MODEL)]
    return pl.pallas_call(
        _merge_kernel,
        grid=(T // tm,),
        in_specs=[row(FOX_WIDTH), row(GLA_V_WIDTH), row(2 * D_MODEL), row(D_MODEL)] + [const(w) for w in ws],
        out_specs=row(D_MODEL),
        out_shape=jax.ShapeDtypeStruct((T, D_MODEL), F32),
        compiler_params=pltpu.CompilerParams(dimension_semantics=("arbitrary",), vmem_limit_bytes=VMEM_LIMIT),
        name="merge",
    )(attn, og, mg, x2, *ws)


ROUTE_TM = 256
CAND_BLOCKS = 10


def _sublane_all(x, op):
    for shift in (4, 2, 1):
        x = op(x, pltpu.roll(x, shift, axis=0))
    return x


def _take_max(s3, iota3, sentinel):
    m8 = _sublane_all(jnp.max(s3, axis=0), jnp.maximum)
    idx8 = _sublane_all(jnp.min(jnp.where(s3 == m8[None], iota3, sentinel), axis=0), jnp.minimum)
    return m8, idx8, iota3 == idx8[None]


def _row_iota(groups, n):
    shape = (groups, SUBLANES, n)
    return lax.broadcasted_iota(jnp.int32, shape, 0) * SUBLANES + lax.broadcasted_iota(jnp.int32, shape, 1)


def _route_kernel(x_ref, wq_ref, keys_ref, after_ref, idx_ref, gate_ref,
                  q_scr, st_scr, it_scr, best_scr, pick_scr, gsel_scr):
    del after_ref
    tm = ROUTE_TM
    K = PEER_TOPK
    q_scr[...] = _dot(x_ref[...].astype(BF16), wq_ref[...])
    key_iota = _row_iota(PEER_N_KEYS // SUBLANES, tm)

    def stage1(hp, _):
        q = q_scr[:, pl.ds(pl.multiple_of(hp * PEER_HALF, PEER_HALF), PEER_HALF)].astype(BF16)
        s = _dot_nt(keys_ref[hp], q).reshape(PEER_N_KEYS // SUBLANES, SUBLANES, tm)
        for i in range(K):
            m8, idx8, hit = _take_max(s, key_iota, PEER_N_KEYS)
            st_scr[hp, i:i + 1, :] = m8[0:1]
            it_scr[hp, i:i + 1, :] = idx8[0:1]
            s = jnp.where(hit, -jnp.inf, s)
        return 0

    lax.fori_loop(0, 2 * PEER_HEADS, stage1, 0)

    cand_iota = _row_iota(CAND_BLOCKS, tm)

    def stage2(h, _):
        s0, s1 = st_scr[2 * h], st_scr[2 * h + 1]
        i0, i1 = it_scr[2 * h], it_scr[2 * h + 1]
        lo, hi = slice(0, SUBLANES), slice(SUBLANES, 2 * SUBLANES)
        cs = [s0[0:1] + s1[lo], s0[0:1] + s1[hi]]
        ci = [i0[0:1] * PEER_N_KEYS + i1[lo], i0[0:1] * PEER_N_KEYS + i1[hi]]
        for a in range(1, SUBLANES):
            cs.append(s0[a:a + 1] + s1[lo])
            ci.append(i0[a:a + 1] * PEER_N_KEYS + i1[lo])
        cs.append(s0[hi] + s1[0:1])
        ci.append(i0[hi] * PEER_N_KEYS + i1[0:1])
        cand = jnp.stack(cs, axis=0)
        cidx = jnp.stack(ci, axis=0)
        for i in range(K):
            m8, _, hit = _take_max(cand, cand_iota, CAND_BLOCKS * SUBLANES)
            pick8 = _sublane_all(jnp.max(jnp.where(hit, cidx, -1), axis=0), jnp.maximum)
            best_scr[i:i + 1, :] = m8[0:1]
            pick_scr[pl.ds(h * K + i, 1), :] = pick8[0:1]
            cand = jnp.where(hit, -jnp.inf, cand)
        best = best_scr[...]
        e = jnp.exp(best - best[0:1])
        gsel_scr[pl.ds(pl.multiple_of(h * K, K), K), :] = e / jnp.sum(e, axis=0, keepdims=True)
        return 0

    lax.fori_loop(0, PEER_HEADS, stage2, 0)
    idx_ref[...] = pick_scr[...].T
    gate_ref[...] = gsel_scr[...].T


def _route(x1, peer_w_query, peer_sub_keys, after):
    T = x1.shape[0]
    tm = ROUTE_TM
    wq = peer_w_query.reshape(D_MODEL, 2 * PEER_HEADS * PEER_HALF).astype(BF16)
    keys = peer_sub_keys.reshape(2 * PEER_HEADS, PEER_N_KEYS, PEER_HALF).astype(BF16)
    picks = pl.BlockSpec((tm, PEER_PICKS), lambda i: (i, 0))
    return pl.pallas_call(
        _route_kernel,
        grid=(T // tm,),
        in_specs=[pl.BlockSpec((tm, D_MODEL), lambda i: (i, 0)),
                  pl.BlockSpec(wq.shape, lambda i: (0, 0)),
                  pl.BlockSpec(keys.shape, lambda i: (0, 0, 0)),
                  pl.BlockSpec((SUBLANES, LANES), lambda i: (0, 0))],
        out_specs=[picks, picks],
        out_shape=[jax.ShapeDtypeStruct((T, PEER_PICKS), jnp.int32),
                   jax.ShapeDtypeStruct((T, PEER_PICKS), F32)],
        scratch_shapes=[pltpu.VMEM((tm, 2 * PEER_HEADS * PEER_HALF), F32),
                        pltpu.VMEM((2 * PEER_HEADS, PEER_TOPK, tm), F32),
                        pltpu.VMEM((2 * PEER_HEADS, PEER_TOPK, tm), jnp.int32),
                        pltpu.VMEM((PEER_TOPK, tm), F32),
                        pltpu.VMEM((PEER_PICKS, tm), jnp.int32),
                        pltpu.VMEM((PEER_PICKS, tm), F32)],
        compiler_params=pltpu.CompilerParams(dimension_semantics=("arbitrary",), vmem_limit_bytes=VMEM_LIMIT),
        name="peer_route",
    )(x1, wq, keys, after)


SC_LANES = 16
SC_ROWS = 64
SC_CHUNKS = PEER_PICKS // SC_ROWS
SC_DCOLS = 256
SC_DVREGS = SC_DCOLS // SC_LANES
SC_DOT_COLS = 512
SC_PART = (D_MODEL // SC_DOT_COLS) * SC_LANES


def _sc_mesh():
    info = plsc.get_sparse_core_info()
    mesh = plsc.VectorSubcoreMesh(core_axis_name="c", subcore_axis_name="s")
    return mesh, info.num_cores, info.num_cores * info.num_subcores


def _sc_token_pipeline(tab_hbm, idx_hbm, vec_hbm, out_hbm, idx_v, vec_v, rows_v, out_v,
                       sem_rows, sem_tok, sem_out, n_cores, tpw, compute_chunk):
    wid = lax.axis_index("s") * n_cores + lax.axis_index("c")
    base = wid * tpw

    def gather(s, c):
        return pltpu.make_async_copy(tab_hbm.at[idx_v.at[s, pl.ds(c * SC_ROWS, SC_ROWS)]],
                                     rows_v.at[c % 2], sem_rows.at[c % 2])

    def tok_fetch(s, tok):
        return (pltpu.make_async_copy(idx_hbm.at[tok], idx_v.at[s], sem_tok.at[0]),
                pltpu.make_async_copy(vec_hbm.at[tok], vec_v.at[s], sem_tok.at[1]))

    def out_copy(s, tok):
        return pltpu.make_async_copy(out_v.at[s], out_hbm.at[tok], sem_out.at[s])

    for d in tok_fetch(0, base):
        d.start()
    for d in tok_fetch(0, base):
        d.wait()
    gather(0, 0).start()

    def pair_body(tp, _):
        for s in range(2):
            t = tp * 2 + s
            tok = base + t
            has_next = t + 1 < tpw

            @pl.when(has_next)
            def _():
                for d in tok_fetch(1 - s, tok + 1):
                    d.start()

            @pl.when(t >= 2)
            def _():
                out_copy(s, tok - 2).wait()

            for c in range(SC_CHUNKS):
                if c + 1 < SC_CHUNKS:
                    gather(s, c + 1).start()
                else:
                    @pl.when(has_next)
                    def _():
                        for d in tok_fetch(1 - s, tok + 1):
                            d.wait()
                        gather(1 - s, 0).start()
                gather(s, c).wait()
                compute_chunk(s, c)
            out_copy(s, tok).start()
        return 0

    lax.fori_loop(0, tpw // 2, pair_body, 0)
    for s in range(2):
        out_copy(s, base + tpw - 2 + s).wait()


def _sc_scratch(vec_len, out_len):
    return [pltpu.VMEM((2, PEER_PICKS), jnp.int32),
            pltpu.VMEM((2, vec_len), F32),
            pltpu.VMEM((2, SC_ROWS, D_MODEL // 2), jnp.int32),
            pltpu.VMEM((2, out_len), F32),
            pltpu.SemaphoreType.DMA((2,)),
            pltpu.SemaphoreType.DMA((2,)),
            pltpu.SemaphoreType.DMA((2,))]


def _pack_table(tab):
    n, d = tab.shape
    bits = lax.bitcast_convert_type(tab.astype(BF16), jnp.uint16).astype(jnp.uint32).reshape(n, d // 32, 2, SC_LANES)
    words = bits[:, :, 0, :] | (bits[:, :, 1, :] << 16)
    return lax.bitcast_convert_type(words, jnp.int32).reshape(n, d // 2)


def _unpack_pairs(x):
    return list(plsc.unpack(x, format=plsc.PackFormat.INTERLEAVED))


def _sc_dot(u_packed, idx, h):
    T = h.shape[0]
    mesh, n_cores, n_workers = _sc_mesh()
    tpw = T // n_workers
    L = SC_LANES

    @functools.partial(
        pl.kernel, mesh=mesh, out_type=jax.ShapeDtypeStruct((T, PEER_PICKS), F32),
        scratch_types=_sc_scratch(D_MODEL, PEER_PICKS) + [pltpu.VMEM((SC_ROWS, SC_PART), F32)],
        compiler_params=pltpu.CompilerParams(needs_layout_passes=False), name="peer_sc_dot")
    def k(u_hbm, idx_hbm, h_hbm, act_hbm, idx_v, h_v, rows_v, act_v, sem_rows, sem_tok, sem_out, part_v):
        lane = lax.iota(jnp.int32, L)

        def compute_chunk(s, c):
            b = c % 2
            for dc in range(D_MODEL // SC_DOT_COLS):
                hq = [plsc.pack(h_v[s, pl.ds(dc * SC_DOT_COLS + 2 * L * j, L)],
                                h_v[s, pl.ds(dc * SC_DOT_COLS + 2 * L * j + L, L)],
                                format=plsc.PackFormat.INTERLEAVED) for j in range(SC_DOT_COLS // (2 * L))]

                @plsc.parallel_loop(0, SC_ROWS, unroll=2)
                def _(r, dc=dc, hq=hq):
                    prods = []
                    for j, hj in enumerate(hq):
                        words = rows_v[b, r, pl.ds(dc * (SC_DOT_COLS // 2) + j * L, L)]
                        prods += _unpack_pairs(plsc.bitcast(words, BF16) * hj)
                    while len(prods) > 1:
                        prods = [prods[i] + prods[i + 1] for i in range(0, len(prods), 2)]
                    part_v[r, pl.ds(dc * L, L)] = prods[0]
            for g in range(SC_ROWS // L):
                rows16 = lane + g * L

                def col_body(col, tots):
                    rot = (lane + col) & (L - 1)
                    return tuple(t + plsc.load_gather(part_v, [rows16, rot + q * L]) for q, t in enumerate(tots))

                tots = lax.fori_loop(0, L, col_body, tuple(jnp.zeros((L,), F32) for _ in range(SC_PART // L)))
                act_v[s, pl.ds(c * SC_ROWS + g * L, L)] = functools.reduce(jnp.add, tots)

        _sc_token_pipeline(u_hbm, idx_hbm, h_hbm, act_hbm, idx_v, h_v, rows_v, act_v,
                           sem_rows, sem_tok, sem_out, n_cores, tpw, compute_chunk)

    return k(u_packed, idx, h)


def _sc_acc(v_packed, idx, w):
    T = w.shape[0]
    mesh, n_cores, n_workers = _sc_mesh()
    tpw = T // n_workers
    L = SC_LANES

    @functools.partial(
        pl.kernel, mesh=mesh, out_type=jax.ShapeDtypeStruct((T, D_MODEL), F32),
        scratch_types=_sc_scratch(PEER_PICKS, D_MODEL),
        compiler_params=pltpu.CompilerParams(needs_layout_passes=False), name="peer_sc_acc")
    def k(v_hbm, idx_hbm, w_hbm, out_hbm, idx_v, w_v, rows_v, out_v, sem_rows, sem_tok, sem_out):
        def compute_chunk(s, c):
            b = c % 2
            for dc in range(D_MODEL // SC_DCOLS):
                cols = [pl.ds(dc * SC_DCOLS + j * L, L) for j in range(SC_DVREGS)]
                if c == 0:
                    accs = tuple(jnp.zeros((L,), F32) for _ in cols)
                else:
                    accs = tuple(out_v[s, cs] for cs in cols)

                def row_body(r, accs, dc=dc):
                    wb = plsc.load_gather(w_v.at[s], [jnp.full((L,), c * SC_ROWS, jnp.int32) + r])
                    new = []
                    for j in range(SC_DVREGS // 2):
                        words = rows_v[b, r, pl.ds(dc * (SC_DCOLS // 2) + j * L, L)]
                        lo, hi = _unpack_pairs(plsc.bitcast(words, BF16))
                        new += [accs[2 * j] + wb * lo, accs[2 * j + 1] + wb * hi]
                    return tuple(new)

                accs = lax.fori_loop(0, SC_ROWS, row_body, accs)
                for a, cs in zip(accs, cols):
                    out_v[s, cs] = a

        _sc_token_pipeline(v_hbm, idx_hbm, w_hbm, out_hbm, idx_v, w_v, rows_v, out_v,
                           sem_rows, sem_tok, sem_out, n_cores, tpw, compute_chunk)

    return k(v_packed, idx, w)


def _gelu_gate_kernel(a_ref, g_ref, o_ref):
    a = a_ref[...]
    o_ref[...] = 0.5 * a * (1.0 + lax.erf(a * (2.0 ** -0.5))) * g_ref[...]


def _ln2_kernel(h_ref, f_ref, g_ref, b_ref, o_ref):
    o_ref[...] = _layer_norm(ALPHA * h_ref[...] + f_ref[...], g_ref[...], b_ref[...])


MIX_TM = 512
_MIX_PARAMS = pltpu.CompilerParams(dimension_semantics=("arbitrary",))


def _mix_row(width):
    return pl.BlockSpec((MIX_TM, width), lambda i: (i, 0))


def _gelu_gate_call(act, gates):
    T = act.shape[0]
    return pl.pallas_call(
        _gelu_gate_kernel, grid=(T // MIX_TM,), in_specs=[_mix_row(PEER_PICKS)] * 2, out_specs=_mix_row(PEER_PICKS),
        out_shape=jax.ShapeDtypeStruct((T, PEER_PICKS), F32), compiler_params=_MIX_PARAMS, name="peer_gelu_gate",
    )(act, gates)


def _mix_finish(x1, idx, w, v_packed, ln_gain, ln_bias):
    T = x1.shape[0]
    const = pl.BlockSpec((1, D_MODEL), lambda i: (0, 0))
    ffn = _sc_acc(v_packed, idx, w)
    out = pl.pallas_call(
        _ln2_kernel, grid=(T // MIX_TM,), in_specs=[_mix_row(D_MODEL)] * 2 + [const, const],
        out_specs=_mix_row(D_MODEL), out_shape=jax.ShapeDtypeStruct((T, D_MODEL), F32),
        compiler_params=_MIX_PARAMS, name="peer_ln2",
    )(x1, ffn, ln_gain.reshape(1, D_MODEL), ln_bias.reshape(1, D_MODEL))
    return out, ffn


BATCH_CHUNKS = 16
SC_PASS_LAG = 2


def kernel(x, w_in, fox_f_bias, gla_gate_up, gla_gate_bias, gla_norm_gain, w_out_fox, w_out_gla, w_out,
           ln1_gain, ln1_bias, peer_w_query, peer_sub_keys, peer_expert_u, peer_expert_v, ln2_gain, ln2_bias):
    B, S, D = x.shape
    assert D == D_MODEL and S % max(ROUTE_TM, FOX_TQ) == 0
    assert DEPTH == 1 and w_in.shape[0] == 1, "the chunk pipeline below is written for the single-layer block"
    u_packed, v_packed = _pack_table(peer_expert_u[0]), _pack_table(peer_expert_v[0])
    n_chunks = BATCH_CHUNKS if B % BATCH_CHUNKS == 0 else 1
    bc = B // n_chunks
    outs, ffns = [], []
    for ci in range(n_chunks):
        h = x[ci * bc:(ci + 1) * bc].reshape(bc * S, D)
        fqkv, gqk, gv, gg, mg, small = _in_proj(h, w_in[0])
        c_t = _fox_gate(small, fox_f_bias[0], bc, S)
        attn = _fox_attn(fqkv, c_t, bc, S)
        og = _gla(gqk, gv, small, gg, gla_gate_up[0], gla_gate_bias[0], gla_norm_gain[0], bc, S)
        x1 = _merge(attn, og, mg, h, w_out_fox[0], w_out_gla[0], w_out[0], ln1_gain[0], ln1_bias[0])
        after = ffns[ci - SC_PASS_LAG][:SUBLANES, :LANES] if ci >= SC_PASS_LAG else jnp.zeros((SUBLANES, LANES), F32)
        idx, gates = _route(x1, peer_w_query[0], peer_sub_keys[0], after)
        w = _gelu_gate_call(_sc_dot(u_packed, idx, x1), gates)
        out, ffn = _mix_finish(x1, idx, w, v_packed, ln2_gain[0], ln2_bias[0])
        outs.append(out)
        ffns.append(ffn)
    return jnp.concatenate(outs, axis=0).reshape(B, S, D)
```

```python
import functools
import math

import jax
import jax.numpy as jnp
from jax import lax
from jax.experimental import pallas as pl
from jax.experimental.pallas import tpu as pltpu
from jax.experimental.pallas import tpu_sc as plsc

F32 = jnp.float32
BF16 = jnp.bfloat16

D_MODEL = 1024
FOX_HEADS = 8
FOX_HEAD_DIM = 64
FOX_WIDTH = FOX_HEADS * FOX_HEAD_DIM
GLA_HEADS = 4
GLA_KEY_DIM = 128
GLA_VAL_DIM = 256
GLA_QK_WIDTH = GLA_HEADS * GLA_KEY_DIM
GLA_V_WIDTH = GLA_HEADS * GLA_VAL_DIM
GLA_GATE_RANK = 16
GLA_GATE_TEMP = 16.0
GLA_NORM_EPS = 1e-5
GLA_CHUNK = 64
PEER_HEADS = 8
PEER_N_KEYS = 128
PEER_HALF = 128
PEER_TOPK = 16
PEER_PICKS = PEER_HEADS * PEER_TOPK
DEPTH = 1
ALPHA = (2.0 * DEPTH) ** 0.25
LN_EPS = 1e-5

LANES = 128
SUBLANES = 8
VMEM_LIMIT = 52 * 1024 * 1024

IN_SPLIT_SIZES = (FOX_WIDTH, FOX_WIDTH, FOX_WIDTH, FOX_HEADS,
                  GLA_QK_WIDTH, GLA_QK_WIDTH, GLA_V_WIDTH, GLA_V_WIDTH, GLA_GATE_RANK,
                  D_MODEL, D_MODEL)
FF_COL = 0
GLR_COL = FOX_HEADS


def _dot(a, b, **kw):
    return jnp.dot(a, b, preferred_element_type=F32, **kw)


def _dot_nt(a, b):
    return lax.dot_general(a, b, (((1,), (1,)), ((), ())), preferred_element_type=F32)


def _dot_tn(a, b):
    return lax.dot_general(a, b, (((0,), (0,)), ((), ())), preferred_element_type=F32)


def _layer_norm(y, gain, bias):
    mu = jnp.mean(y, axis=-1, keepdims=True)
    yc = y - mu
    var = jnp.mean(yc * yc, axis=-1, keepdims=True)
    return yc * lax.rsqrt(var + LN_EPS) * gain + bias


def _in_proj_kernel(x_ref, wf_ref, wgqk_ref, wgv_ref, wgg_ref, wm_ref, ws_ref,
                    f_ref, gqk_ref, gv_ref, gg_ref, m_ref, s_ref):
    xb = x_ref[...].astype(BF16)
    f_ref[...] = _dot(xb, wf_ref[...]).astype(BF16)
    gqk_ref[...] = _dot(xb, wgqk_ref[...]).astype(BF16)
    gv_ref[...] = _dot(xb, wgv_ref[...]).astype(BF16)
    gg_ref[...] = _dot(xb, wgg_ref[...])
    m_ref[...] = _dot(xb, wm_ref[...])
    s_ref[...] = _dot(xb, ws_ref[...])


def _in_proj(x2, w_in):
    T = x2.shape[0]
    tm = 256
    pts = [0]
    for s in IN_SPLIT_SIZES:
        pts.append(pts[-1] + s)
    col = lambda i, j: w_in[:, pts[i]:pts[j]]
    wf = col(0, 3).astype(BF16)
    wgqk = col(4, 6).astype(BF16)
    wgv = col(6, 7).astype(BF16)
    wgg = col(7, 8).astype(BF16)
    wm = col(9, 11).astype(BF16)
    ws = jnp.concatenate([col(3, 4), col(8, 9)], axis=1)
    ws = jnp.pad(ws, ((0, 0), (0, LANES - ws.shape[1]))).astype(BF16)
    ws_list = [wf, wgqk, wgv, wgg, wm, ws]
    out_dtypes = [BF16, BF16, BF16, F32, F32, F32]
    const = lambda w: pl.BlockSpec(w.shape, lambda i: (0, 0))
    return pl.pallas_call(
        _in_proj_kernel,
        grid=(T // tm,),
        in_specs=[pl.BlockSpec((tm, D_MODEL), lambda i: (i, 0))] + [const(w) for w in ws_list],
        out_specs=[pl.BlockSpec((tm, w.shape[1]), lambda i: (i, 0)) for w in ws_list],
        out_shape=[jax.ShapeDtypeStruct((T, w.shape[1]), dt) for w, dt in zip(ws_list, out_dtypes)],
        compiler_params=pltpu.CompilerParams(dimension_semantics=("arbitrary",), vmem_limit_bytes=VMEM_LIMIT),
        name="in_proj",
    )(x2, *ws_list)


def _fox_gate_kernel(s_ref, bias_ref, c_ref):
    S = s_ref.shape[0]
    ff_t = s_ref[...].T[FF_COL:FF_COL + FOX_HEADS, :]
    log_f = jax.nn.log_sigmoid(ff_t + bias_ref[...])
    r = lax.broadcasted_iota(jnp.int32, (LANES, LANES), 0)
    c = lax.broadcasted_iota(jnp.int32, (LANES, LANES), 1)
    tri = (r <= c).astype(F32)
    carry = jnp.zeros((FOX_HEADS, 1), F32)
    for j in range(S // LANES):
        blk = log_f[:, j * LANES:(j + 1) * LANES]
        cs = _dot(blk, tri, precision=lax.Precision.HIGHEST) + carry
        c_ref[0, :, j * LANES:(j + 1) * LANES] = cs
        carry = cs[:, LANES - 1:LANES]


def _fox_gate(small, fox_f_bias, B, S):
    return pl.pallas_call(
        _fox_gate_kernel,
        grid=(B,),
        in_specs=[pl.BlockSpec((S, LANES), lambda b: (b, 0)),
                  pl.BlockSpec((FOX_HEADS, 1), lambda b: (0, 0))],
        out_specs=pl.BlockSpec((1, FOX_HEADS, S), lambda b: (b, 0, 0)),
        out_shape=jax.ShapeDtypeStruct((B, FOX_HEADS, S), F32),
        compiler_params=pltpu.CompilerParams(dimension_semantics=("arbitrary",)),
        name="fox_gate",
    )(small, fox_f_bias.reshape(FOX_HEADS, 1))


FOX_TQ = 256
FOX_TK = FOX_TQ


def _fox_attn_kernel(q_ref, k_ref, v_ref, c_ref, o_ref):
    qi = pl.program_id(2)
    tq, tk, dh = FOX_TQ, FOX_TK, FOX_HEAD_DIM
    n_h = LANES // dh
    qs = [q_ref[:, hh * dh:(hh + 1) * dh] * (dh ** -0.5) for hh in range(n_h)]

    def step(j, carry, masked):
        ks = pl.ds(pl.multiple_of(j * tk, tk), tk)
        k2 = k_ref[ks, :]
        v2 = v_ref[ks, :]
        out = []
        for hh in range(n_h):
            m, l, acc = carry[hh]
            s = _dot_nt(qs[hh], k2[:, hh * dh:(hh + 1) * dh]) - c_ref[hh, :, ks]
            if masked:
                r = lax.broadcasted_iota(jnp.int32, (tq, tk), 0)
                c = lax.broadcasted_iota(jnp.int32, (tq, tk), 1)
                s = jnp.where(c <= r, s, -jnp.inf)
            m_new = jnp.maximum(m, jnp.max(s, axis=1, keepdims=True))
            p = jnp.exp(s - m_new)
            a = jnp.exp(m - m_new)
            l = a * l + jnp.sum(p, axis=1, keepdims=True)
            acc = a * acc + _dot(p.astype(BF16), v2[:, hh * dh:(hh + 1) * dh])
            out.append((m_new, l, acc))
        return tuple(out)

    init = tuple((jnp.full((tq, 1), -jnp.inf, F32), jnp.zeros((tq, 1), F32), jnp.zeros((tq, dh), F32))
                 for _ in range(n_h))
    carry = lax.fori_loop(0, qi, lambda j, c: step(j, c, False), init)
    carry = step(qi, carry, True)
    o_ref[...] = jnp.concatenate([acc / l for _, l, acc in carry], axis=1).astype(BF16)


def _fox_attn(fqkv, c_t, B, S):
    T = B * S
    nq = S // FOX_TQ
    n_hp = FOX_WIDTH // LANES
    return pl.pallas_call(
        _fox_attn_kernel,
        grid=(B, n_hp, nq),
        in_specs=[pl.BlockSpec((FOX_TQ, LANES), lambda b, h, i: (b * nq + i, h)),
                  pl.BlockSpec((S, LANES), lambda b, h, i: (b, n_hp + h)),
                  pl.BlockSpec((S, LANES), lambda b, h, i: (b, 2 * n_hp + h)),
                  pl.BlockSpec((LANES // FOX_HEAD_DIM, 1, S), lambda b, h, i: (b * n_hp + h, 0, 0))],
        out_specs=pl.BlockSpec((FOX_TQ, LANES), lambda b, h, i: (b * nq + i, h)),
        out_shape=jax.ShapeDtypeStruct((T, FOX_WIDTH), BF16),
        compiler_params=pltpu.CompilerParams(dimension_semantics=("arbitrary", "arbitrary", "arbitrary")),
        name="fox_attn",
    )(fqkv, fqkv, fqkv, c_t.reshape(B * FOX_HEADS, 1, S))


def _gla_kernel(q_ref, k_ref, v_ref, s_ref, up_ref, gb_ref, gg_ref, gain_ref, o_ref):
    S = q_ref.shape[0]
    C = GLA_CHUNK
    r = lax.broadcasted_iota(jnp.int32, (C, C), 0)
    c = lax.broadcasted_iota(jnp.int32, (C, C), 1)
    tril = (r >= c).astype(F32)

    def body(ci, st_t):
        rows = pl.ds(pl.multiple_of(ci * C, C), C)
        z = _dot(s_ref[rows, :].astype(BF16), up_ref[...]) + gb_ref[...]
        la = jax.nn.log_sigmoid(z) * (1.0 / GLA_GATE_TEMP)
        cum = _dot(tril, la, precision=lax.Precision.HIGHEST)
        tot = cum[C - 1:C, :]
        kd = (k_ref[rows, :].astype(F32) * jnp.exp(tot - cum)).astype(BF16)
        st_t = st_t * jnp.exp(tot) + _dot_tn(v_ref[rows, :], kd)
        o = _dot_nt(q_ref[rows, :], st_t.astype(BF16)) * (GLA_KEY_DIM ** -0.5)
        o = o * lax.rsqrt(jnp.mean(o * o, axis=-1, keepdims=True) + GLA_NORM_EPS) * gain_ref[...]
        o = o * jax.nn.silu(gg_ref[rows, :])
        o_ref[rows, :] = o.astype(BF16)
        return st_t

    lax.fori_loop(0, S // C, body, jnp.zeros((GLA_VAL_DIM, GLA_KEY_DIM), F32))


def _gla(gqk, gv, small, gg, gla_gate_up, gla_gate_bias, gla_norm_gain, B, S):
    T = B * S
    up = jnp.zeros((LANES, GLA_QK_WIDTH), F32).at[GLR_COL:GLR_COL + GLA_GATE_RANK].set(gla_gate_up).astype(BF16)
    gb = gla_gate_bias.reshape(1, GLA_QK_WIDTH)
    gain = gla_norm_gain.reshape(1, GLA_V_WIDTH)
    H = GLA_HEADS
    return pl.pallas_call(
        _gla_kernel,
        grid=(B, H),
        in_specs=[pl.BlockSpec((S, GLA_KEY_DIM), lambda b, h: (b, h)),
                  pl.BlockSpec((S, GLA_KEY_DIM), lambda b, h: (b, H + h)),
                  pl.BlockSpec((S, GLA_VAL_DIM), lambda b, h: (b, h)),
                  pl.BlockSpec((S, LANES), lambda b, h: (b, 0)),
                  pl.BlockSpec((LANES, GLA_KEY_DIM), lambda b, h: (0, h)),
                  pl.BlockSpec((1, GLA_KEY_DIM), lambda b, h: (0, h)),
                  pl.BlockSpec((S, GLA_VAL_DIM), lambda b, h: (b, h)),
                  pl.BlockSpec((1, GLA_VAL_DIM), lambda b, h: (0, h))],
        out_specs=pl.BlockSpec((S, GLA_VAL_DIM), lambda b, h: (b, h)),
        out_shape=jax.ShapeDtypeStruct((T, GLA_V_WIDTH), BF16),
        compiler_params=pltpu.CompilerParams(dimension_semantics=("arbitrary", "arbitrary")),
        name="gla",
    )(gqk, gqk, gv, small, up, gb, gg, gain)


def _merge_kernel(a_ref, og_ref, m_ref, x_ref, wf_ref, wg_ref, wo_ref, g_ref, b_ref, o_ref):
    y_fox = _dot(a_ref[...], wf_ref[...])
    y_gla = _dot(og_ref[...], wg_ref[...])
    merged = (jax.nn.sigmoid(m_ref[:, :D_MODEL]) * y_fox + jax.nn.sigmoid(m_ref[:, D_MODEL:]) * y_gla)
    mix = _dot(merged.astype(BF16), wo_ref[...])
    o_ref[...] = _layer_norm(ALPHA * x_ref[...] + mix, g_ref[...], b_ref[...])


def _merge(attn, og, mg, x2, w_out_fox, w_out_gla, w_out, ln_gain, ln_bias):
    T = x2.shape[0]
    tm = 256
    row = lambda w: pl.BlockSpec((tm, w), lambda i: (i, 0))
    const = lambda a: pl.BlockSpec(a.shape, lambda i: (0, 0))
    ws = [w_out_fox.astype(BF16), w_out_gla.astype(BF16), w_out.astype(BF16),
          ln_gain.reshape(1, D_MODEL), ln_bias.reshape(1, D_MODEL)]
    return pl.pallas_call(
        _merge_kernel,
        grid=(T // tm,),
        in_specs=[row(FOX_WIDTH), row(GLA_V_WIDTH), row(2 * D_MODEL), row(D_MODEL)] + [const(w) for w in ws],
        out_specs=row(D_MODEL),
        out_shape=jax.ShapeDtypeStruct((T, D_MODEL), F32),
        compiler_params=pltpu.CompilerParams(dimension_semantics=("arbitrary",), vmem_limit_bytes=VMEM_LIMIT),
        name="merge",
    )(attn, og, mg, x2, *ws)


ROUTE_TM = 256
CAND_BLOCKS = 10


def _sublane_all(x, op):
    for shift in (4, 2, 1):
        x = op(x, pltpu.roll(x, shift, axis=0))
    return x


def _take_max(s3, iota3, sentinel):
    m8 = _sublane_all(jnp.max(s3, axis=0), jnp.maximum)
    idx8 = _sublane_all(jnp.min(jnp.where(s3 == m8[None], iota3, sentinel), axis=0), jnp.minimum)
    return m8, idx8, iota3 == idx8[None]


def _row_iota(groups, n):
    shape = (groups, SUBLANES, n)
    return lax.broadcasted_iota(jnp.int32, shape, 0) * SUBLANES + lax.broadcasted_iota(jnp.int32, shape, 1)


def _route_kernel(x_ref, wq_ref, keys_ref, after_a_ref, after_b_ref, idx_ref, gate_ref,
                  q_scr, st_scr, it_scr, best_scr, pick_scr, gsel_scr):
    del after_a_ref, after_b_ref
    tm = ROUTE_TM
    K = PEER_TOPK
    q_scr[...] = _dot(x_ref[...].astype(BF16), wq_ref[...])
    key_iota = _row_iota(PEER_N_KEYS // SUBLANES, tm)

    def stage1(hp, _):
        q = q_scr[:, pl.ds(pl.multiple_of(hp * PEER_HALF, PEER_HALF), PEER_HALF)].astype(BF16)
        s = _dot_nt(keys_ref[hp], q).reshape(PEER_N_KEYS // SUBLANES, SUBLANES, tm)
        for i in range(K):
            m8, idx8, hit = _take_max(s, key_iota, PEER_N_KEYS)
            st_scr[hp, i:i + 1, :] = m8[0:1]
            it_scr[hp, i:i + 1, :] = idx8[0:1]
            s = jnp.where(hit, -jnp.inf, s)
        return 0

    lax.fori_loop(0, 2 * PEER_HEADS, stage1, 0)

    cand_iota = _row_iota(CAND_BLOCKS, tm)

    def stage2(h, _):
        s0, s1 = st_scr[2 * h], st_scr[2 * h + 1]
        i0, i1 = it_scr[2 * h], it_scr[2 * h + 1]
        lo, hi = slice(0, SUBLANES), slice(SUBLANES, 2 * SUBLANES)
        cs = [s0[0:1] + s1[lo], s0[0:1] + s1[hi]]
        ci = [i0[0:1] * PEER_N_KEYS + i1[lo], i0[0:1] * PEER_N_KEYS + i1[hi]]
        for a in range(1, SUBLANES):
            cs.append(s0[a:a + 1] + s1[lo])
            ci.append(i0[a:a + 1] * PEER_N_KEYS + i1[lo])
        cs.append(s0[hi] + s1[0:1])
        ci.append(i0[hi] * PEER_N_KEYS + i1[0:1])
        cand = jnp.stack(cs, axis=0)
        cidx = jnp.stack(ci, axis=0)
        for i in range(K):
            m8, _, hit = _take_max(cand, cand_iota, CAND_BLOCKS * SUBLANES)
            pick8 = _sublane_all(jnp.max(jnp.where(hit, cidx, -1), axis=0), jnp.maximum)
            best_scr[i:i + 1, :] = m8[0:1]
            pick_scr[pl.ds(h * K + i, 1), :] = pick8[0:1]
            cand = jnp.where(hit, -jnp.inf, cand)
        best = best_scr[...]
        e = jnp.exp(best - best[0:1])
        gsel_scr[pl.ds(pl.multiple_of(h * K, K), K), :] = e / jnp.sum(e, axis=0, keepdims=True)
        return 0

    lax.fori_loop(0, PEER_HEADS, stage2, 0)
    idx_ref[...] = pick_scr[...].T
    gate_ref[...] = gsel_scr[...].T


def _route(x1, peer_w_query, peer_sub_keys, after, after_b):
    T = x1.shape[0]
    tm = ROUTE_TM
    wq = peer_w_query.reshape(D_MODEL, 2 * PEER_HEADS * PEER_HALF).astype(BF16)
    keys = peer_sub_keys.reshape(2 * PEER_HEADS, PEER_N_KEYS, PEER_HALF).astype(BF16)
    picks = pl.BlockSpec((tm, PEER_PICKS), lambda i: (i, 0))
    return pl.pallas_call(
        _route_kernel,
        grid=(T // tm,),
        in_specs=[pl.BlockSpec((tm, D_MODEL), lambda i: (i, 0)),
                  pl.BlockSpec(wq.shape, lambda i: (0, 0)),
                  pl.BlockSpec(keys.shape, lambda i: (0, 0, 0)),
                  pl.BlockSpec((SUBLANES, LANES), lambda i: (0, 0)),
                  pl.BlockSpec((SUBLANES, LANES), lambda i: (0, 0))],
        out_specs=[picks, picks],
        out_shape=[jax.ShapeDtypeStruct((T, PEER_PICKS), jnp.int32),
                   jax.ShapeDtypeStruct((T, PEER_PICKS), F32)],
        scratch_shapes=[pltpu.VMEM((tm, 2 * PEER_HEADS * PEER_HALF), F32),
                        pltpu.VMEM((2 * PEER_HEADS, PEER_TOPK, tm), F32),
                        pltpu.VMEM((2 * PEER_HEADS, PEER_TOPK, tm), jnp.int32),
                        pltpu.VMEM((PEER_TOPK, tm), F32),
                        pltpu.VMEM((PEER_PICKS, tm), jnp.int32),
                        pltpu.VMEM((PEER_PICKS, tm), F32)],
        compiler_params=pltpu.CompilerParams(dimension_semantics=("arbitrary",), vmem_limit_bytes=VMEM_LIMIT),
        name="peer_route",
    )(x1, wq, keys, after, after_b)


SC_LANES = 16
SC_ROWS = 64
SC_CHUNKS = PEER_PICKS // SC_ROWS
SC_DCOLS = 256
SC_DVREGS = SC_DCOLS // SC_LANES
SC_DOT_COLS = 512
SC_PART = (D_MODEL // SC_DOT_COLS) * SC_LANES


def _sc_mesh():
    info = plsc.get_sparse_core_info()
    mesh = plsc.VectorSubcoreMesh(core_axis_name="c", subcore_axis_name="s")
    return mesh, info.num_cores, info.num_cores * info.num_subcores


def _sc_token_pipeline(tab_hbm, idx_hbm, vec_hbm, out_hbm, idx_v, vec_v, rows_v, out_v,
                       sem_rows, sem_tok, sem_out, n_cores, tpw, compute_chunk):
    wid = lax.axis_index("s") * n_cores + lax.axis_index("c")
    base = wid * tpw

    def gather(s, c):
        return pltpu.make_async_copy(tab_hbm.at[idx_v.at[s, pl.ds(c * SC_ROWS, SC_ROWS)]],
                                     rows_v.at[c % 2], sem_rows.at[c % 2])

    def tok_fetch(s, tok):
        return (pltpu.make_async_copy(idx_hbm.at[tok], idx_v.at[s], sem_tok.at[0]),
                pltpu.make_async_copy(vec_hbm.at[tok], vec_v.at[s], sem_tok.at[1]))

    def out_copy(s, tok):
        return pltpu.make_async_copy(out_v.at[s], out_hbm.at[tok], sem_out.at[s])

    for d in tok_fetch(0, base):
        d.start()
    for d in tok_fetch(0, base):
        d.wait()
    gather(0, 0).start()

    def pair_body(tp, _):
        for s in range(2):
            t = tp * 2 + s
            tok = base + t
            has_next = t + 1 < tpw

            @pl.when(has_next)
            def _():
                for d in tok_fetch(1 - s, tok + 1):
                    d.start()

            @pl.when(t >= 2)
            def _():
                out_copy(s, tok - 2).wait()

            for c in range(SC_CHUNKS):
                if c + 1 < SC_CHUNKS:
                    gather(s, c + 1).start()
                else:
                    @pl.when(has_next)
                    def _():
                        for d in tok_fetch(1 - s, tok + 1):
                            d.wait()
                        gather(1 - s, 0).start()
                gather(s, c).wait()
                compute_chunk(s, c)
            out_copy(s, tok).start()
        return 0

    lax.fori_loop(0, tpw // 2, pair_body, 0)
    for s in range(2):
        out_copy(s, base + tpw - 2 + s).wait()


def _sc_scratch(vec_len, out_len):
    return [pltpu.VMEM((2, PEER_PICKS), jnp.int32),
            pltpu.VMEM((2, vec_len), F32),
            pltpu.VMEM((2, SC_ROWS, D_MODEL // 2), jnp.int32),
            pltpu.VMEM((2, out_len), F32),
            pltpu.SemaphoreType.DMA((2,)),
            pltpu.SemaphoreType.DMA((2,)),
            pltpu.SemaphoreType.DMA((2,))]


def _pack_table(tab):
    n, d = tab.shape
    bits = lax.bitcast_convert_type(tab.astype(BF16), jnp.uint16).astype(jnp.uint32).reshape(n, d // 32, 2, SC_LANES)
    words = bits[:, :, 0, :] | (bits[:, :, 1, :] << 16)
    return lax.bitcast_convert_type(words, jnp.int32).reshape(n, d // 2)


def _unpack_pairs(x):
    return list(plsc.unpack(x, format=plsc.PackFormat.INTERLEAVED))


def _sc_dot(u_packed, idx, h):
    T = h.shape[0]
    mesh, n_cores, n_workers = _sc_mesh()
    tpw = T // n_workers
    L = SC_LANES

    @functools.partial(
        pl.kernel, mesh=mesh, out_type=jax.ShapeDtypeStruct((T, PEER_PICKS), F32),
        scratch_types=_sc_scratch(D_MODEL, PEER_PICKS) + [pltpu.VMEM((SC_ROWS, SC_PART), F32)],
        compiler_params=pltpu.CompilerParams(needs_layout_passes=False), name="peer_sc_dot")
    def k(u_hbm, idx_hbm, h_hbm, act_hbm, idx_v, h_v, rows_v, act_v, sem_rows, sem_tok, sem_out, part_v):
        lane = lax.iota(jnp.int32, L)

        def compute_chunk(s, c):
            b = c % 2
            for dc in range(D_MODEL // SC_DOT_COLS):
                hq = [plsc.pack(h_v[s, pl.ds(dc * SC_DOT_COLS + 2 * L * j, L)],
                                h_v[s, pl.ds(dc * SC_DOT_COLS + 2 * L * j + L, L)],
                                format=plsc.PackFormat.INTERLEAVED) for j in range(SC_DOT_COLS // (2 * L))]

                @plsc.parallel_loop(0, SC_ROWS, unroll=2)
                def _(r, dc=dc, hq=hq):
                    prods = []
                    for j, hj in enumerate(hq):
                        words = rows_v[b, r, pl.ds(dc * (SC_DOT_COLS // 2) + j * L, L)]
                        prods += _unpack_pairs(plsc.bitcast(words, BF16) * hj)
                    while len(prods) > 1:
                        prods = [prods[i] + prods[i + 1] for i in range(0, len(prods), 2)]
                    part_v[r, pl.ds(dc * L, L)] = prods[0]
            for g in range(SC_ROWS // L):
                rows16 = lane + g * L

                def col_body(col, tots):
                    rot = (lane + col) & (L - 1)
                    return tuple(t + plsc.load_gather(part_v, [rows16, rot + q * L]) for q, t in enumerate(tots))

                tots = lax.fori_loop(0, L, col_body, tuple(jnp.zeros((L,), F32) for _ in range(SC_PART // L)))
                act_v[s, pl.ds(c * SC_ROWS + g * L, L)] = functools.reduce(jnp.add, tots)

        _sc_token_pipeline(u_hbm, idx_hbm, h_hbm, act_hbm, idx_v, h_v, rows_v, act_v,
                           sem_rows, sem_tok, sem_out, n_cores, tpw, compute_chunk)

    return k(u_packed, idx, h)


def _sc_acc(v_packed, idx, w):
    T = w.shape[0]
    mesh, n_cores, n_workers = _sc_mesh()
    tpw = T // n_workers
    L = SC_LANES

    @functools.partial(
        pl.kernel, mesh=mesh, out_type=jax.ShapeDtypeStruct((T, D_MODEL), F32),
        scratch_types=_sc_scratch(PEER_PICKS, D_MODEL),
        compiler_params=pltpu.CompilerParams(needs_layout_passes=False), name="peer_sc_acc")
    def k(v_hbm, idx_hbm, w_hbm, out_hbm, idx_v, w_v, rows_v, out_v, sem_rows, sem_tok, sem_out):
        def compute_chunk(s, c):
            b = c % 2
            for dc in range(D_MODEL // SC_DCOLS):
                cols = [pl.ds(dc * SC_DCOLS + j * L, L) for j in range(SC_DVREGS)]
                if c == 0:
                    accs = tuple(jnp.zeros((L,), F32) for _ in cols)
                else:
                    accs = tuple(out_v[s, cs] for cs in cols)

                def row_body(r, accs, dc=dc):
                    wb = plsc.load_gather(w_v.at[s], [jnp.full((L,), c * SC_ROWS, jnp.int32) + r])
                    new = []
                    for j in range(SC_DVREGS // 2):
                        words = rows_v[b, r, pl.ds(dc * (SC_DCOLS // 2) + j * L, L)]
                        lo, hi = _unpack_pairs(plsc.bitcast(words, BF16))
                        new += [accs[2 * j] + wb * lo, accs[2 * j + 1] + wb * hi]
                    return tuple(new)

                accs = lax.fori_loop(0, SC_ROWS, row_body, accs)
                for a, cs in zip(accs, cols):
                    out_v[s, cs] = a

        _sc_token_pipeline(v_hbm, idx_hbm, w_hbm, out_hbm, idx_v, w_v, rows_v, out_v,
                           sem_rows, sem_tok, sem_out, n_cores, tpw, compute_chunk)

    return k(v_packed, idx, w)


DENSE_TM = 512
DENSE_TN = 2048


def _dense_act_kernel(x_ref, u_ref, o_ref):
    o_ref[...] = _dot_nt(x_ref[...].astype(BF16), u_ref[...])


def _dense_act(x1, u_bf16):
    T, N = x1.shape[0], u_bf16.shape[0]
    return pl.pallas_call(
        _dense_act_kernel, grid=(T // DENSE_TM, N // DENSE_TN),
        in_specs=[pl.BlockSpec((DENSE_TM, D_MODEL), lambda i, j: (i, 0)),
                  pl.BlockSpec((DENSE_TN, D_MODEL), lambda i, j: (j, 0))],
        out_specs=pl.BlockSpec((DENSE_TM, DENSE_TN), lambda i, j: (i, j)),
        out_shape=jax.ShapeDtypeStruct((T, N), F32),
        compiler_params=pltpu.CompilerParams(dimension_semantics=("arbitrary", "arbitrary"),
                                             vmem_limit_bytes=VMEM_LIMIT),
        name="peer_dense_act",
    )(x1, u_bf16)


PICK_TG = SUBLANES
PICK_NQ = 4096


def _sc_pick(dense, idx):
    T, N = dense.shape
    mesh, n_cores, n_workers = _sc_mesh()
    gpw = T // PICK_TG // n_workers
    n_q = N // PICK_NQ
    L = SC_LANES

    @functools.partial(
        pl.kernel, mesh=mesh, out_type=jax.ShapeDtypeStruct((T, PEER_PICKS), F32),
        scratch_types=[pltpu.VMEM((2, PICK_TG, PEER_PICKS), jnp.int32), pltpu.VMEM((2, PICK_TG, PICK_NQ), F32),
                       pltpu.VMEM((2, PICK_TG, PEER_PICKS), F32),
                       pltpu.SemaphoreType.DMA((2,)), pltpu.SemaphoreType.DMA((2,)), pltpu.SemaphoreType.DMA((2,))],
        compiler_params=pltpu.CompilerParams(needs_layout_passes=False), name="peer_sc_pick")
    def k(d_hbm, idx_hbm, act_hbm, idx_v, buf, act_v, sem_idx, sem_buf, sem_out):
        g0 = (lax.axis_index("s") * n_cores + lax.axis_index("c")) * gpw

        def idx_fetch(gs, g):
            return pltpu.make_async_copy(idx_hbm.at[pl.ds(g * PICK_TG, PICK_TG)], idx_v.at[gs], sem_idx.at[gs])

        def blk_fetch(g, q):
            return pltpu.make_async_copy(d_hbm.at[pl.ds(g * PICK_TG, PICK_TG), pl.ds(q * PICK_NQ, PICK_NQ)],
                                         buf.at[q % 2], sem_buf.at[q % 2])

        def out_copy(gs, g):
            return pltpu.make_async_copy(act_v.at[gs], act_hbm.at[pl.ds(g * PICK_TG, PICK_TG)], sem_out.at[gs])

        idx_fetch(0, g0).start()
        blk_fetch(g0, 0).start()

        def pair_body(gp, _):
            for gs in range(2):
                gi = gp * 2 + gs
                g = g0 + gi
                has_next = gi + 1 < gpw

                @pl.when(has_next)
                def _():
                    idx_fetch(1 - gs, g + 1).start()

                idx_fetch(gs, g).wait()

                @pl.when(gi >= 2)
                def _():
                    out_copy(gs, g - 2).wait()

                for q in range(n_q):
                    if q + 1 < n_q:
                        blk_fetch(g, q + 1).start()
                    else:
                        @pl.when(has_next)
                        def _():
                            blk_fetch(g + 1, 0).start()
                    blk_fetch(g, q).wait()
                    for t in range(PICK_TG):
                        for v in range(PEER_PICKS // L):
                            picks = pl.ds(v * L, L)
                            local = idx_v[gs, t, picks] - q * PICK_NQ
                            inside = (local >= 0) & (local < PICK_NQ)
                            val = plsc.load_gather(buf, [jnp.full((L,), q % 2, jnp.int32), jnp.full((L,), t, jnp.int32),
                                                         jnp.where(inside, local, 0)])
                            act_v[gs, t, picks] = val if q == 0 else jnp.where(inside, val, act_v[gs, t, picks])
                out_copy(gs, g).start()
            return 0

        lax.fori_loop(0, gpw // 2, pair_body, 0)
        for gs in range(2):
            out_copy(gs, g0 + gpw - 2 + gs).wait()

    return k(dense, idx)


def _gelu_gate_kernel(a_ref, g_ref, o_ref):
    a = a_ref[...]
    o_ref[...] = 0.5 * a * (1.0 + lax.erf(a * (2.0 ** -0.5))) * g_ref[...]


def _ln2_kernel(h_ref, f_ref, g_ref, b_ref, o_ref):
    o_ref[...] = _layer_norm(ALPHA * h_ref[...] + f_ref[...], g_ref[...], b_ref[...])


MIX_TM = 512
_MIX_PARAMS = pltpu.CompilerParams(dimension_semantics=("arbitrary",))


def _mix_row(width):
    return pl.BlockSpec((MIX_TM, width), lambda i: (i, 0))


def _gelu_gate_call(act, gates):
    T = act.shape[0]
    return pl.pallas_call(
        _gelu_gate_kernel, grid=(T // MIX_TM,), in_specs=[_mix_row(PEER_PICKS)] * 2, out_specs=_mix_row(PEER_PICKS),
        out_shape=jax.ShapeDtypeStruct((T, PEER_PICKS), F32), compiler_params=_MIX_PARAMS, name="peer_gelu_gate",
    )(act, gates)


def _mix_finish(x1, idx, w, v_packed, ln_gain, ln_bias):
    T = x1.shape[0]
    const = pl.BlockSpec((1, D_MODEL), lambda i: (0, 0))
    ffn = _sc_acc(v_packed, idx, w)
    out = pl.pallas_call(
        _ln2_kernel, grid=(T // MIX_TM,), in_specs=[_mix_row(D_MODEL)] * 2 + [const, const],
        out_specs=_mix_row(D_MODEL), out_shape=jax.ShapeDtypeStruct((T, D_MODEL), F32),
        compiler_params=_MIX_PARAMS, name="peer_ln2",
    )(x1, ffn, ln_gain.reshape(1, D_MODEL), ln_bias.reshape(1, D_MODEL))
    return out, ffn


BATCH_CHUNKS = 16
SC_PASS_LAG = 3
DENSE_PERIOD = 2


def kernel(x, w_in, fox_f_bias, gla_gate_up, gla_gate_bias, gla_norm_gain, w_out_fox, w_out_gla, w_out,
           ln1_gain, ln1_bias, peer_w_query, peer_sub_keys, peer_expert_u, peer_expert_v, ln2_gain, ln2_bias):
    B, S, D = x.shape
    assert D == D_MODEL and S % max(ROUTE_TM, FOX_TQ) == 0
    assert DEPTH == 1 and w_in.shape[0] == 1, "the chunk pipeline below is written for the single-layer block"
    u_packed, v_packed = _pack_table(peer_expert_u[0]), _pack_table(peer_expert_v[0])
    u_bf16 = peer_expert_u[0].astype(BF16)
    n_chunks = BATCH_CHUNKS if B % BATCH_CHUNKS == 0 else 1
    bc = B // n_chunks
    outs, ffns, ws = [], [], []
    for ci in range(n_chunks):
        h = x[ci * bc:(ci + 1) * bc].reshape(bc * S, D)
        fqkv, gqk, gv, gg, mg, small = _in_proj(h, w_in[0])
        c_t = _fox_gate(small, fox_f_bias[0], bc, S)
        attn = _fox_attn(fqkv, c_t, bc, S)
        og = _gla(gqk, gv, small, gg, gla_gate_up[0], gla_gate_bias[0], gla_norm_gain[0], bc, S)
        x1 = _merge(attn, og, mg, h, w_out_fox[0], w_out_gla[0], w_out[0], ln1_gain[0], ln1_bias[0])
        zero = jnp.zeros((SUBLANES, LANES), F32)
        after = ws[ci - 1][:SUBLANES, :LANES] if ci >= 1 else zero
        after_b = ffns[ci - SC_PASS_LAG][:SUBLANES, :LANES] if ci >= SC_PASS_LAG else zero
        idx, gates = _route(x1, peer_w_query[0], peer_sub_keys[0], after, after_b)
        if ci % DENSE_PERIOD == DENSE_PERIOD - 1:
            act = _sc_pick(_dense_act(x1, u_bf16), idx)
        else:
            act = _sc_dot(u_packed, idx, x1)
        w = _gelu_gate_call(act, gates)
        out, ffn = _mix_finish(x1, idx, w, v_packed, ln2_gain[0], ln2_bias[0])
        outs.append(out)
        ffns.append(ffn)
        ws.append(w)
    return jnp.concatenate(outs, axis=0).reshape(B, S, D)
```

```python
import functools
import math

import jax
import jax.numpy as jnp
from jax import lax
from jax.experimental import pallas as pl
from jax.experimental.pallas import tpu as pltpu
from jax.experimental.pallas import tpu_sc as plsc

F32 = jnp.float32
BF16 = jnp.bfloat16

D_MODEL = 1024
FOX_HEADS = 8
FOX_HEAD_DIM = 64
FOX_WIDTH = FOX_HEADS * FOX_HEAD_DIM
GLA_HEADS = 4
GLA_KEY_DIM = 128
GLA_VAL_DIM = 256
GLA_QK_WIDTH = GLA_HEADS * GLA_KEY_DIM
GLA_V_WIDTH = GLA_HEADS * GLA_VAL_DIM
GLA_GATE_RANK = 16
GLA_GATE_TEMP = 16.0
GLA_NORM_EPS = 1e-5
GLA_CHUNK = 64
PEER_HEADS = 8
PEER_N_KEYS = 128
PEER_HALF = 128
PEER_TOPK = 16
PEER_PICKS = PEER_HEADS * PEER_TOPK
DEPTH = 1
ALPHA = (2.0 * DEPTH) ** 0.25
LN_EPS = 1e-5

LANES = 128
SUBLANES = 8
VMEM_LIMIT = 52 * 1024 * 1024

IN_SPLIT_SIZES = (FOX_WIDTH, FOX_WIDTH, FOX_WIDTH, FOX_HEADS,
                  GLA_QK_WIDTH, GLA_QK_WIDTH, GLA_V_WIDTH, GLA_V_WIDTH, GLA_GATE_RANK,
                  D_MODEL, D_MODEL)
FF_COL = 0
GLR_COL = FOX_HEADS


def _dot(a, b, **kw):
    return jnp.dot(a, b, preferred_element_type=F32, **kw)


def _dot_nt(a, b):
    return lax.dot_general(a, b, (((1,), (1,)), ((), ())), preferred_element_type=F32)


def _dot_tn(a, b):
    return lax.dot_general(a, b, (((0,), (0,)), ((), ())), preferred_element_type=F32)


def _layer_norm(y, gain, bias):
    mu = jnp.mean(y, axis=-1, keepdims=True)
    yc = y - mu
    var = jnp.mean(yc * yc, axis=-1, keepdims=True)
    return yc * lax.rsqrt(var + LN_EPS) * gain + bias


def _in_proj_kernel(x_ref, wf_ref, wgqk_ref, wgv_ref, wgg_ref, wm_ref, ws_ref,
                    f_ref, gqk_ref, gv_ref, gg_ref, m_ref, s_ref):
    xb = x_ref[...].astype(BF16)
    f_ref[...] = _dot(xb, wf_ref[...]).astype(BF16)
    gqk_ref[...] = _dot(xb, wgqk_ref[...]).astype(BF16)
    gv_ref[...] = _dot(xb, wgv_ref[...]).astype(BF16)
    gg_ref[...] = _dot(xb, wgg_ref[...])
    m_ref[...] = _dot(xb, wm_ref[...])
    s_ref[...] = _dot(xb, ws_ref[...])


def _in_proj(x2, w_in):
    T = x2.shape[0]
    tm = 256
    pts = [0]
    for s in IN_SPLIT_SIZES:
        pts.append(pts[-1] + s)
    col = lambda i, j: w_in[:, pts[i]:pts[j]]
    wf = col(0, 3).astype(BF16)
    wgqk = col(4, 6).astype(BF16)
    wgv = col(6, 7).astype(BF16)
    wgg = col(7, 8).astype(BF16)
    wm = col(9, 11).astype(BF16)
    ws = jnp.concatenate([col(3, 4), col(8, 9)], axis=1)
    ws = jnp.pad(ws, ((0, 0), (0, LANES - ws.shape[1]))).astype(BF16)
    ws_list = [wf, wgqk, wgv, wgg, wm, ws]
    out_dtypes = [BF16, BF16, BF16, F32, F32, F32]
    const = lambda w: pl.BlockSpec(w.shape, lambda i: (0, 0))
    return pl.pallas_call(
        _in_proj_kernel,
        grid=(T // tm,),
        in_specs=[pl.BlockSpec((tm, D_MODEL), lambda i: (i, 0))] + [const(w) for w in ws_list],
        out_specs=[pl.BlockSpec((tm, w.shape[1]), lambda i: (i, 0)) for w in ws_list],
        out_shape=[jax.ShapeDtypeStruct((T, w.shape[1]), dt) for w, dt in zip(ws_list, out_dtypes)],
        compiler_params=pltpu.CompilerParams(dimension_semantics=("arbitrary",), vmem_limit_bytes=VMEM_LIMIT),
        name="in_proj",
    )(x2, *ws_list)


def _fox_gate_kernel(s_ref, bias_ref, c_ref):
    S = s_ref.shape[0]
    ff_t = s_ref[...].T[FF_COL:FF_COL + FOX_HEADS, :]
    log_f = jax.nn.log_sigmoid(ff_t + bias_ref[...])
    r = lax.broadcasted_iota(jnp.int32, (LANES, LANES), 0)
    c = lax.broadcasted_iota(jnp.int32, (LANES, LANES), 1)
    tri = (r <= c).astype(F32)
    carry = jnp.zeros((FOX_HEADS, 1), F32)
    for j in range(S // LANES):
        blk = log_f[:, j * LANES:(j + 1) * LANES]
        cs = _dot(blk, tri, precision=lax.Precision.HIGHEST) + carry
        c_ref[0, :, j * LANES:(j + 1) * LANES] = cs
        carry = cs[:, LANES - 1:LANES]


def _fox_gate(small, fox_f_bias, B, S):
    return pl.pallas_call(
        _fox_gate_kernel,
        grid=(B,),
        in_specs=[pl.BlockSpec((S, LANES), lambda b: (b, 0)),
                  pl.BlockSpec((FOX_HEADS, 1), lambda b: (0, 0))],
        out_specs=pl.BlockSpec((1, FOX_HEADS, S), lambda b: (b, 0, 0)),
        out_shape=jax.ShapeDtypeStruct((B, FOX_HEADS, S), F32),
        compiler_params=pltpu.CompilerParams(dimension_semantics=("arbitrary",)),
        name="fox_gate",
    )(small, fox_f_bias.reshape(FOX_HEADS, 1))


FOX_TQ = 256
FOX_TK = FOX_TQ
FOX_COLS = 256


def _fox_attn_kernel(q_ref, k_ref, v_ref, c_ref, o_ref):
    qi = pl.program_id(2)
    tq, tk, dh = FOX_TQ, FOX_TK, FOX_HEAD_DIM
    n_h = FOX_COLS // dh
    qs = [q_ref[:, hh * dh:(hh + 1) * dh] * (dh ** -0.5) for hh in range(n_h)]

    def step(j, carry, masked):
        ks = pl.ds(pl.multiple_of(j * tk, tk), tk)
        k2 = k_ref[ks, :]
        v2 = v_ref[ks, :]
        out = []
        for hh in range(n_h):
            m, l, acc = carry[hh]
            s = _dot_nt(qs[hh], k2[:, hh * dh:(hh + 1) * dh]) - c_ref[hh, :, ks]
            if masked:
                r = lax.broadcasted_iota(jnp.int32, (tq, tk), 0)
                c = lax.broadcasted_iota(jnp.int32, (tq, tk), 1)
                s = jnp.where(c <= r, s, -jnp.inf)
            m_new = jnp.maximum(m, jnp.max(s, axis=1, keepdims=True))
            p = jnp.exp(s - m_new)
            a = jnp.exp(m - m_new)
            l = a * l + jnp.sum(p, axis=1, keepdims=True)
            acc = a * acc + _dot(p.astype(BF16), v2[:, hh * dh:(hh + 1) * dh])
            out.append((m_new, l, acc))
        return tuple(out)

    init = tuple((jnp.full((tq, 1), -jnp.inf, F32), jnp.zeros((tq, 1), F32), jnp.zeros((tq, dh), F32))
                 for _ in range(n_h))
    carry = lax.fori_loop(0, qi, lambda j, c: step(j, c, False), init)
    carry = step(qi, carry, True)
    o_ref[...] = jnp.concatenate([acc / l for _, l, acc in carry], axis=1).astype(BF16)


def _fox_attn(fqkv, c_t, B, S):
    T = B * S
    nq = S // FOX_TQ
    n_hp = FOX_WIDTH // FOX_COLS
    return pl.pallas_call(
        _fox_attn_kernel,
        grid=(B, n_hp, nq),
        in_specs=[pl.BlockSpec((FOX_TQ, FOX_COLS), lambda b, h, i: (b * nq + i, h)),
                  pl.BlockSpec((S, FOX_COLS), lambda b, h, i: (b, n_hp + h)),
                  pl.BlockSpec((S, FOX_COLS), lambda b, h, i: (b, 2 * n_hp + h)),
                  pl.BlockSpec((FOX_COLS // FOX_HEAD_DIM, 1, S), lambda b, h, i: (b * n_hp + h, 0, 0))],
        out_specs=pl.BlockSpec((FOX_TQ, FOX_COLS), lambda b, h, i: (b * nq + i, h)),
        out_shape=jax.ShapeDtypeStruct((T, FOX_WIDTH), BF16),
        compiler_params=pltpu.CompilerParams(dimension_semantics=("arbitrary", "arbitrary", "arbitrary")),
        name="fox_attn",
    )(fqkv, fqkv, fqkv, c_t.reshape(B * FOX_HEADS, 1, S))


GLA_GROUP = 4


def _gla_kernel(q_ref, k_ref, v_ref, s_ref, up_ref, gb_ref, gg_ref, gain_ref, o_ref):
    S = q_ref.shape[0]
    C = GLA_CHUNK
    dk, dv = GLA_KEY_DIM, GLA_VAL_DIM
    r = lax.broadcasted_iota(jnp.int32, (C, C), 0)
    c = lax.broadcasted_iota(jnp.int32, (C, C), 1)
    tril = (r >= c).astype(F32)

    def body(ci, states):
        rows = pl.ds(pl.multiple_of(ci * C, C), C)
        small = s_ref[rows, :].astype(BF16)
        new_states = []
        for g, st_t in enumerate(states):
            kc, vc = slice(g * dk, (g + 1) * dk), slice(g * dv, (g + 1) * dv)
            z = _dot(small, up_ref[:, kc]) + gb_ref[:, kc]
            la = jax.nn.log_sigmoid(z) * (1.0 / GLA_GATE_TEMP)
            cum = _dot(tril, la, precision=lax.Precision.HIGHEST)
            tot = cum[C - 1:C, :]
            kd = (k_ref[rows, kc].astype(F32) * jnp.exp(tot - cum)).astype(BF16)
            st_t = st_t * jnp.exp(tot) + _dot_tn(v_ref[rows, vc], kd)
            o = _dot_nt(q_ref[rows, kc], st_t.astype(BF16)) * (dk ** -0.5)
            o = o * lax.rsqrt(jnp.mean(o * o, axis=-1, keepdims=True) + GLA_NORM_EPS) * gain_ref[:, vc]
            o = o * jax.nn.silu(gg_ref[rows, vc])
            o_ref[rows, vc] = o.astype(BF16)
            new_states.append(st_t)
        return tuple(new_states)

    lax.fori_loop(0, S // C, body, tuple(jnp.zeros((dv, dk), F32) for _ in range(GLA_GROUP)))


def _gla(gqk, gv, small, gg, gla_gate_up, gla_gate_bias, gla_norm_gain, B, S):
    T = B * S
    up = jnp.zeros((LANES, GLA_QK_WIDTH), F32).at[GLR_COL:GLR_COL + GLA_GATE_RANK].set(gla_gate_up).astype(BF16)
    gb = gla_gate_bias.reshape(1, GLA_QK_WIDTH)
    gain = gla_norm_gain.reshape(1, GLA_V_WIDTH)
    n_groups = GLA_HEADS // GLA_GROUP
    kw, vw = GLA_GROUP * GLA_KEY_DIM, GLA_GROUP * GLA_VAL_DIM
    return pl.pallas_call(
        _gla_kernel,
        grid=(B, n_groups),
        in_specs=[pl.BlockSpec((S, kw), lambda b, h: (b, h)),
                  pl.BlockSpec((S, kw), lambda b, h: (b, n_groups + h)),
                  pl.BlockSpec((S, vw), lambda b, h: (b, h)),
                  pl.BlockSpec((S, LANES), lambda b, h: (b, 0)),
                  pl.BlockSpec((LANES, kw), lambda b, h: (0, h)),
                  pl.BlockSpec((1, kw), lambda b, h: (0, h)),
                  pl.BlockSpec((S, vw), lambda b, h: (b, h)),
                  pl.BlockSpec((1, vw), lambda b, h: (0, h))],
        out_specs=pl.BlockSpec((S, vw), lambda b, h: (b, h)),
        out_shape=jax.ShapeDtypeStruct((T, GLA_V_WIDTH), BF16),
        compiler_params=pltpu.CompilerParams(dimension_semantics=("arbitrary", "arbitrary")),
        name="gla",
    )(gqk, gqk, gv, small, up, gb, gg, gain)


def _merge_kernel(a_ref, og_ref, m_ref, x_ref, wf_ref, wg_ref, wo_ref, g_ref, b_ref, o_ref):
    y_fox = _dot(a_ref[...], wf_ref[...])
    y_gla = _dot(og_ref[...], wg_ref[...])
    merged = (jax.nn.sigmoid(m_ref[:, :D_MODEL]) * y_fox + jax.nn.sigmoid(m_ref[:, D_MODEL:]) * y_gla)
    mix = _dot(merged.astype(BF16), wo_ref[...])
    o_ref[...] = _layer_norm(ALPHA * x_ref[...] + mix, g_ref[...], b_ref[...])


def _merge(attn, og, mg, x2, w_out_fox, w_out_gla, w_out, ln_gain, ln_bias):
    T = x2.shape[0]
    tm = 256
    row = lambda w: pl.BlockSpec((tm, w), lambda i: (i, 0))
    const = lambda a: pl.BlockSpec(a.shape, lambda i: (0, 0))
    ws = [w_out_fox.astype(BF16), w_out_gla.astype(BF16), w_out.astype(BF16),
          ln_gain.reshape(1, D_MODEL), ln_bias.reshape(1, D_MODEL)]
    return pl.pallas_call(
        _merge_kernel,
        grid=(T // tm,),
        in_specs=[row(FOX_WIDTH), row(GLA_V_WIDTH), row(2 * D_MODEL), row(D_MODEL)] + [const(w) for w in ws],
        out_specs=row(D_MODEL),
        out_shape=jax.ShapeDtypeStruct((T, D_MODEL), F32),
        compiler_params=pltpu.CompilerParams(dimension_semantics=("arbitrary",), vmem_limit_bytes=VMEM_LIMIT),
        name="merge",
    )(attn, og, mg, x2, *ws)


ROUTE_TM = 256
CAND_BLOCKS = 10


def _sublane_all(x, op):
    for shift in (4, 2, 1):
        x = op(x, pltpu.roll(x, shift, axis=0))
    return x


def _take_max(s3, iota3, sentinel):
    m8 = _sublane_all(jnp.max(s3, axis=0), jnp.maximum)
    idx8 = _sublane_all(jnp.min(jnp.where(s3 == m8[None], iota3, sentinel), axis=0), jnp.minimum)
    return m8, idx8, iota3 == idx8[None]


def _row_iota(groups, n):
    shape = (groups, SUBLANES, n)
    return lax.broadcasted_iota(jnp.int32, shape, 0) * SUBLANES + lax.broadcasted_iota(jnp.int32, shape, 1)


def _route_kernel(x_ref, wq_ref, keys_ref, after_a_ref, after_b_ref, idx_ref, gate_ref,
                  q_scr, st_scr, it_scr, best_scr, pick_scr, gsel_scr):
    del after_a_ref, after_b_ref
    tm = ROUTE_TM
    K = PEER_TOPK
    q_scr[...] = _dot(x_ref[...].astype(BF16), wq_ref[...])
    key_iota = _row_iota(PEER_N_KEYS // SUBLANES, tm)

    def stage1(pair, _):
        hps = (2 * pair, 2 * pair + 1)
        ss = []
        for hp in hps:
            q = q_scr[:, pl.ds(pl.multiple_of(hp * PEER_HALF, PEER_HALF), PEER_HALF)].astype(BF16)
            ss.append(_dot_nt(keys_ref[hp], q).reshape(PEER_N_KEYS // SUBLANES, SUBLANES, tm))
        for i in range(K):
            for n, hp in enumerate(hps):
                m8, idx8, hit = _take_max(ss[n], key_iota, PEER_N_KEYS)
                st_scr[hp, i:i + 1, :] = m8[0:1]
                it_scr[hp, i:i + 1, :] = idx8[0:1]
                ss[n] = jnp.where(hit, -jnp.inf, ss[n])
        return 0

    lax.fori_loop(0, PEER_HEADS, stage1, 0)

    cand_iota = _row_iota(CAND_BLOCKS, tm)

    def stage2(h, _):
        s0, s1 = st_scr[2 * h], st_scr[2 * h + 1]
        i0, i1 = it_scr[2 * h], it_scr[2 * h + 1]
        lo, hi = slice(0, SUBLANES), slice(SUBLANES, 2 * SUBLANES)
        cs = [s0[0:1] + s1[lo], s0[0:1] + s1[hi]]
        ci = [i0[0:1] * PEER_N_KEYS + i1[lo], i0[0:1] * PEER_N_KEYS + i1[hi]]
        for a in range(1, SUBLANES):
            cs.append(s0[a:a + 1] + s1[lo])
            ci.append(i0[a:a + 1] * PEER_N_KEYS + i1[lo])
        cs.append(s0[hi] + s1[0:1])
        ci.append(i0[hi] * PEER_N_KEYS + i1[0:1])
        cand = jnp.stack(cs, axis=0)
        cidx = jnp.stack(ci, axis=0)
        for i in range(K):
            m8, _, hit = _take_max(cand, cand_iota, CAND_BLOCKS * SUBLANES)
            pick8 = _sublane_all(jnp.max(jnp.where(hit, cidx, -1), axis=0), jnp.maximum)
            best_scr[i:i + 1, :] = m8[0:1]
            pick_scr[pl.ds(h * K + i, 1), :] = pick8[0:1]
            cand = jnp.where(hit, -jnp.inf, cand)
        best = best_scr[...]
        e = jnp.exp(best - best[0:1])
        gsel_scr[pl.ds(pl.multiple_of(h * K, K), K), :] = e / jnp.sum(e, axis=0, keepdims=True)
        return 0

    lax.fori_loop(0, PEER_HEADS, stage2, 0)
    idx_ref[...] = pick_scr[...].T
    gate_ref[...] = gsel_scr[...].T


def _route(x1, peer_w_query, peer_sub_keys, after, after_b):
    T = x1.shape[0]
    tm = ROUTE_TM
    wq = peer_w_query.reshape(D_MODEL, 2 * PEER_HEADS * PEER_HALF).astype(BF16)
    keys = peer_sub_keys.reshape(2 * PEER_HEADS, PEER_N_KEYS, PEER_HALF).astype(BF16)
    picks = pl.BlockSpec((tm, PEER_PICKS), lambda i: (i, 0))
    return pl.pallas_call(
        _route_kernel,
        grid=(T // tm,),
        in_specs=[pl.BlockSpec((tm, D_MODEL), lambda i: (i, 0)),
                  pl.BlockSpec(wq.shape, lambda i: (0, 0)),
                  pl.BlockSpec(keys.shape, lambda i: (0, 0, 0)),
                  pl.BlockSpec((SUBLANES, LANES), lambda i: (0, 0)),
                  pl.BlockSpec((SUBLANES, LANES), lambda i: (0, 0))],
        out_specs=[picks, picks],
        out_shape=[jax.ShapeDtypeStruct((T, PEER_PICKS), jnp.int32),
                   jax.ShapeDtypeStruct((T, PEER_PICKS), F32)],
        scratch_shapes=[pltpu.VMEM((tm, 2 * PEER_HEADS * PEER_HALF), F32),
                        pltpu.VMEM((2 * PEER_HEADS, PEER_TOPK, tm), F32),
                        pltpu.VMEM((2 * PEER_HEADS, PEER_TOPK, tm), jnp.int32),
                        pltpu.VMEM((PEER_TOPK, tm), F32),
                        pltpu.VMEM((PEER_PICKS, tm), jnp.int32),
                        pltpu.VMEM((PEER_PICKS, tm), F32)],
        compiler_params=pltpu.CompilerParams(dimension_semantics=("arbitrary",), vmem_limit_bytes=VMEM_LIMIT),
        name="peer_route",
    )(x1, wq, keys, after, after_b)


SC_LANES = 16
SC_ROWS = 64
SC_CHUNKS = PEER_PICKS // SC_ROWS
SC_DCOLS = 256
SC_DVREGS = SC_DCOLS // SC_LANES
SC_DOT_COLS = 512
SC_PART = (D_MODEL // SC_DOT_COLS) * SC_LANES


def _sc_mesh():
    info = plsc.get_sparse_core_info()
    mesh = plsc.VectorSubcoreMesh(core_axis_name="c", subcore_axis_name="s")
    return mesh, info.num_cores, info.num_cores * info.num_subcores


def _sc_token_pipeline(tab_hbm, idx_hbm, vec_hbm, out_hbm, idx_v, vec_v, rows_v, out_v,
                       sem_rows, sem_tok, sem_out, n_cores, tpw, compute_chunk):
    wid = lax.axis_index("s") * n_cores + lax.axis_index("c")
    base = wid * tpw

    def gather(s, c):
        return pltpu.make_async_copy(tab_hbm.at[idx_v.at[s, pl.ds(c * SC_ROWS, SC_ROWS)]],
                                     rows_v.at[c % 2], sem_rows.at[c % 2])

    def tok_fetch(s, tok):
        return (pltpu.make_async_copy(idx_hbm.at[tok], idx_v.at[s], sem_tok.at[0]),
                pltpu.make_async_copy(vec_hbm.at[tok], vec_v.at[s], sem_tok.at[1]))

    def out_copy(s, tok):
        return pltpu.make_async_copy(out_v.at[s], out_hbm.at[tok], sem_out.at[s])

    for d in tok_fetch(0, base):
        d.start()
    for d in tok_fetch(0, base):
        d.wait()
    gather(0, 0).start()

    def pair_body(tp, _):
        for s in range(2):
            t = tp * 2 + s
            tok = base + t
            has_next = t + 1 < tpw

            @pl.when(has_next)
            def _():
                for d in tok_fetch(1 - s, tok + 1):
                    d.start()

            @pl.when(t >= 2)
            def _():
                out_copy(s, tok - 2).wait()

            for c in range(SC_CHUNKS):
                if c + 1 < SC_CHUNKS:
                    gather(s, c + 1).start()
                else:
                    @pl.when(has_next)
                    def _():
                        for d in tok_fetch(1 - s, tok + 1):
                            d.wait()
                        gather(1 - s, 0).start()
                gather(s, c).wait()
                compute_chunk(s, c)
            out_copy(s, tok).start()
        return 0

    lax.fori_loop(0, tpw // 2, pair_body, 0)
    for s in range(2):
        out_copy(s, base + tpw - 2 + s).wait()


def _sc_scratch(vec_len, out_len):
    return [pltpu.VMEM((2, PEER_PICKS), jnp.int32),
            pltpu.VMEM((2, vec_len), F32),
            pltpu.VMEM((2, SC_ROWS, D_MODEL // 2), jnp.int32),
            pltpu.VMEM((2, out_len), F32),
            pltpu.SemaphoreType.DMA((2,)),
            pltpu.SemaphoreType.DMA((2,)),
            pltpu.SemaphoreType.DMA((2,))]


def _pack_table(tab):
    n, d = tab.shape
    bits = lax.bitcast_convert_type(tab.astype(BF16), jnp.uint16).astype(jnp.uint32).reshape(n, d // 32, 2, SC_LANES)
    words = bits[:, :, 0, :] | (bits[:, :, 1, :] << 16)
    return lax.bitcast_convert_type(words, jnp.int32).reshape(n, d // 2)


def _unpack_pairs(x):
    return list(plsc.unpack(x, format=plsc.PackFormat.INTERLEAVED))


def _sc_dot(u_packed, idx, h):
    T = h.shape[0]
    mesh, n_cores, n_workers = _sc_mesh()
    tpw = T // n_workers
    L = SC_LANES

    @functools.partial(
        pl.kernel, mesh=mesh, out_type=jax.ShapeDtypeStruct((T, PEER_PICKS), F32),
        scratch_types=_sc_scratch(D_MODEL, PEER_PICKS) + [pltpu.VMEM((SC_ROWS, SC_PART), F32)],
        compiler_params=pltpu.CompilerParams(needs_layout_passes=False), name="peer_sc_dot")
    def k(u_hbm, idx_hbm, h_hbm, act_hbm, idx_v, h_v, rows_v, act_v, sem_rows, sem_tok, sem_out, part_v):
        lane = lax.iota(jnp.int32, L)

        def compute_chunk(s, c):
            b = c % 2
            for dc in range(D_MODEL // SC_DOT_COLS):
                hq = [plsc.pack(h_v[s, pl.ds(dc * SC_DOT_COLS + 2 * L * j, L)],
                                h_v[s, pl.ds(dc * SC_DOT_COLS + 2 * L * j + L, L)],
                                format=plsc.PackFormat.INTERLEAVED) for j in range(SC_DOT_COLS // (2 * L))]

                @plsc.parallel_loop(0, SC_ROWS, unroll=2)
                def _(r, dc=dc, hq=hq):
                    prods = []
                    for j, hj in enumerate(hq):
                        words = rows_v[b, r, pl.ds(dc * (SC_DOT_COLS // 2) + j * L, L)]
                        prods += _unpack_pairs(plsc.bitcast(words, BF16) * hj)
                    while len(prods) > 1:
                        prods = [prods[i] + prods[i + 1] for i in range(0, len(prods), 2)]
                    part_v[r, pl.ds(dc * L, L)] = prods[0]
            for g in range(SC_ROWS // L):
                rows16 = lane + g * L

                def col_body(col, tots):
                    rot = (lane + col) & (L - 1)
                    return tuple(t + plsc.load_gather(part_v, [rows16, rot + q * L]) for q, t in enumerate(tots))

                tots = lax.fori_loop(0, L, col_body, tuple(jnp.zeros((L,), F32) for _ in range(SC_PART // L)))
                act_v[s, pl.ds(c * SC_ROWS + g * L, L)] = functools.reduce(jnp.add, tots)

        _sc_token_pipeline(u_hbm, idx_hbm, h_hbm, act_hbm, idx_v, h_v, rows_v, act_v,
                           sem_rows, sem_tok, sem_out, n_cores, tpw, compute_chunk)

    return k(u_packed, idx, h)


def _sc_acc(v_packed, idx, w):
    T = w.shape[0]
    mesh, n_cores, n_workers = _sc_mesh()
    tpw = T // n_workers
    L = SC_LANES

    @functools.partial(
        pl.kernel, mesh=mesh, out_type=jax.ShapeDtypeStruct((T, D_MODEL), F32),
        scratch_types=_sc_scratch(PEER_PICKS, D_MODEL),
        compiler_params=pltpu.CompilerParams(needs_layout_passes=False), name="peer_sc_acc")
    def k(v_hbm, idx_hbm, w_hbm, out_hbm, idx_v, w_v, rows_v, out_v, sem_rows, sem_tok, sem_out):
        def compute_chunk(s, c):
            b = c % 2
            for dc in range(D_MODEL // SC_DCOLS):
                cols = [pl.ds(dc * SC_DCOLS + j * L, L) for j in range(SC_DVREGS)]
                if c == 0:
                    accs = tuple(jnp.zeros((L,), F32) for _ in cols)
                else:
                    accs = tuple(out_v[s, cs] for cs in cols)

                def row_body(r, accs, dc=dc):
                    wb = plsc.load_gather(w_v.at[s], [jnp.full((L,), c * SC_ROWS, jnp.int32) + r])
                    new = []
                    for j in range(SC_DVREGS // 2):
                        words = rows_v[b, r, pl.ds(dc * (SC_DCOLS // 2) + j * L, L)]
                        lo, hi = _unpack_pairs(plsc.bitcast(words, BF16))
                        new += [accs[2 * j] + wb * lo, accs[2 * j + 1] + wb * hi]
                    return tuple(new)

                accs = lax.fori_loop(0, SC_ROWS, row_body, accs)
                for a, cs in zip(accs, cols):
                    out_v[s, cs] = a

        _sc_token_pipeline(v_hbm, idx_hbm, w_hbm, out_hbm, idx_v, w_v, rows_v, out_v,
                           sem_rows, sem_tok, sem_out, n_cores, tpw, compute_chunk)

    return k(v_packed, idx, w)


DENSE_TM = 2048
DENSE_TN = 1024


def _dense_act_kernel(x_ref, u_ref, o_ref):
    o_ref[...] = _dot_nt(x_ref[...].astype(BF16), u_ref[...])


def _dense_act(x1, u_bf16):
    T, N = x1.shape[0], u_bf16.shape[0]
    return pl.pallas_call(
        _dense_act_kernel, grid=(T // DENSE_TM, N // DENSE_TN),
        in_specs=[pl.BlockSpec((DENSE_TM, D_MODEL), lambda i, j: (i, 0)),
                  pl.BlockSpec((DENSE_TN, D_MODEL), lambda i, j: (j, 0))],
        out_specs=pl.BlockSpec((DENSE_TM, DENSE_TN), lambda i, j: (i, j)),
        out_shape=jax.ShapeDtypeStruct((T, N), F32),
        compiler_params=pltpu.CompilerParams(dimension_semantics=("arbitrary", "arbitrary"),
                                             vmem_limit_bytes=VMEM_LIMIT),
        name="peer_dense_act",
    )(x1, u_bf16)


PICK_TG = SUBLANES
PICK_NQ = 4096


def _sc_pick(dense, idx):
    T, N = dense.shape
    mesh, n_cores, n_workers = _sc_mesh()
    gpw = T // PICK_TG // n_workers
    n_q = N // PICK_NQ
    L = SC_LANES

    @functools.partial(
        pl.kernel, mesh=mesh, out_type=jax.ShapeDtypeStruct((T, PEER_PICKS), F32),
        scratch_types=[pltpu.VMEM((2, PICK_TG, PEER_PICKS), jnp.int32), pltpu.VMEM((2, PICK_TG, PICK_NQ), F32),
                       pltpu.VMEM((2, PICK_TG, PEER_PICKS), F32),
                       pltpu.SemaphoreType.DMA((2,)), pltpu.SemaphoreType.DMA((2,)), pltpu.SemaphoreType.DMA((2,))],
        compiler_params=pltpu.CompilerParams(needs_layout_passes=False), name="peer_sc_pick")
    def k(d_hbm, idx_hbm, act_hbm, idx_v, buf, act_v, sem_idx, sem_buf, sem_out):
        g0 = (lax.axis_index("s") * n_cores + lax.axis_index("c")) * gpw

        def idx_fetch(gs, g):
            return pltpu.make_async_copy(idx_hbm.at[pl.ds(g * PICK_TG, PICK_TG)], idx_v.at[gs], sem_idx.at[gs])

        def blk_fetch(g, q):
            return pltpu.make_async_copy(d_hbm.at[pl.ds(g * PICK_TG, PICK_TG), pl.ds(q * PICK_NQ, PICK_NQ)],
                                         buf.at[q % 2], sem_buf.at[q % 2])

        def out_copy(gs, g):
            return pltpu.make_async_copy(act_v.at[gs], act_hbm.at[pl.ds(g * PICK_TG, PICK_TG)], sem_out.at[gs])

        idx_fetch(0, g0).start()
        blk_fetch(g0, 0).start()

        def pair_body(gp, _):
            for gs in range(2):
                gi = gp * 2 + gs
                g = g0 + gi
                has_next = gi + 1 < gpw

                @pl.when(has_next)
                def _():
                    idx_fetch(1 - gs, g + 1).start()

                idx_fetch(gs, g).wait()

                @pl.when(gi >= 2)
                def _():
                    out_copy(gs, g - 2).wait()

                for q in range(n_q):
                    if q + 1 < n_q:
                        blk_fetch(g, q + 1).start()
                    else:
                        @pl.when(has_next)
                        def _():
                            blk_fetch(g + 1, 0).start()
                    blk_fetch(g, q).wait()
                    for t in range(PICK_TG):
                        for v in range(PEER_PICKS // L):
                            picks = pl.ds(v * L, L)
                            local = idx_v[gs, t, picks] - q * PICK_NQ
                            inside = (local >= 0) & (local < PICK_NQ)
                            val = plsc.load_gather(buf, [jnp.full((L,), q % 2, jnp.int32), jnp.full((L,), t, jnp.int32),
                                                         jnp.where(inside, local, 0)])
                            act_v[gs, t, picks] = val if q == 0 else jnp.where(inside, val, act_v[gs, t, picks])
                out_copy(gs, g).start()
            return 0

        lax.fori_loop(0, gpw // 2, pair_body, 0)
        for gs in range(2):
            out_copy(gs, g0 + gpw - 2 + gs).wait()

    return k(dense, idx)


def _gelu_gate_kernel(a_ref, g_ref, o_ref):
    a = a_ref[...]
    o_ref[...] = 0.5 * a * (1.0 + lax.erf(a * (2.0 ** -0.5))) * g_ref[...]


def _ln2_kernel(h_ref, f_ref, g_ref, b_ref, o_ref):
    o_ref[...] = _layer_norm(ALPHA * h_ref[...] + f_ref[...], g_ref[...], b_ref[...])


MIX_TM = 512
_MIX_PARAMS = pltpu.CompilerParams(dimension_semantics=("arbitrary",))


def _mix_row(width):
    return pl.BlockSpec((MIX_TM, width), lambda i: (i, 0))


def _gelu_gate_call(act, gates):
    T = act.shape[0]
    return pl.pallas_call(
        _gelu_gate_kernel, grid=(T // MIX_TM,), in_specs=[_mix_row(PEER_PICKS)] * 2, out_specs=_mix_row(PEER_PICKS),
        out_shape=jax.ShapeDtypeStruct((T, PEER_PICKS), F32), compiler_params=_MIX_PARAMS, name="peer_gelu_gate",
    )(act, gates)


def _mix_finish(x1, idx, w, v_packed, ln_gain, ln_bias):
    T = x1.shape[0]
    const = pl.BlockSpec((1, D_MODEL), lambda i: (0, 0))
    ffn = _sc_acc(v_packed, idx, w)
    out = pl.pallas_call(
        _ln2_kernel, grid=(T // MIX_TM,), in_specs=[_mix_row(D_MODEL)] * 2 + [const, const],
        out_specs=_mix_row(D_MODEL), out_shape=jax.ShapeDtypeStruct((T, D_MODEL), F32),
        compiler_params=_MIX_PARAMS, name="peer_ln2",
    )(x1, ffn, ln_gain.reshape(1, D_MODEL), ln_bias.reshape(1, D_MODEL))
    return out, ffn


BATCH_CHUNKS = 16
SC_PASS_LAG = 3
DENSE_PATTERN = (False, True, True, True)


def kernel(x, w_in, fox_f_bias, gla_gate_up, gla_gate_bias, gla_norm_gain, w_out_fox, w_out_gla, w_out,
           ln1_gain, ln1_bias, peer_w_query, peer_sub_keys, peer_expert_u, peer_expert_v, ln2_gain, ln2_bias):
    B, S, D = x.shape
    assert D == D_MODEL and S % max(ROUTE_TM, FOX_TQ) == 0
    assert DEPTH == 1 and w_in.shape[0] == 1, "the chunk pipeline below is written for the single-layer block"
    u_packed, v_packed = _pack_table(peer_expert_u[0]), _pack_table(peer_expert_v[0])
    u_bf16 = peer_expert_u[0].astype(BF16)
    n_chunks = BATCH_CHUNKS if B % BATCH_CHUNKS == 0 else 1
    bc = B // n_chunks
    outs, ffns, ws = [], [], []
    for ci in range(n_chunks):
        h = x[ci * bc:(ci + 1) * bc].reshape(bc * S, D)
        fqkv, gqk, gv, gg, mg, small = _in_proj(h, w_in[0])
        c_t = _fox_gate(small, fox_f_bias[0], bc, S)
        attn = _fox_attn(fqkv, c_t, bc, S)
        og = _gla(gqk, gv, small, gg, gla_gate_up[0], gla_gate_bias[0], gla_norm_gain[0], bc, S)
        x1 = _merge(attn, og, mg, h, w_out_fox[0], w_out_gla[0], w_out[0], ln1_gain[0], ln1_bias[0])
        zero = jnp.zeros((SUBLANES, LANES), F32)
        after = ws[ci - 1][:SUBLANES, :LANES] if ci >= 1 else zero
        after_b = ffns[ci - SC_PASS_LAG][:SUBLANES, :LANES] if ci >= SC_PASS_LAG else zero
        idx, gates = _route(x1, peer_w_query[0], peer_sub_keys[0], after, after_b)
        if DENSE_PATTERN[ci % len(DENSE_PATTERN)]:
            act = _sc_pick(_dense_act(x1, u_bf16), idx)
        else:
            act = _sc_dot(u_packed, idx, x1)
        w = _gelu_gate_call(act, gates)
        out, ffn = _mix_finish(x1, idx, w, v_packed, ln2_gain[0], ln2_bias[0])
        outs.append(out)
        ffns.append(ffn)
        ws.append(w)
    return jnp.concatenate(outs, axis=0).reshape(B, S, D)
```

```python
import functools
import math

import jax
import jax.numpy as jnp
from jax import lax
from jax.experimental import pallas as pl
from jax.experimental.pallas import tpu as pltpu
from jax.experimental.pallas import tpu_sc as plsc

F32 = jnp.float32
BF16 = jnp.bfloat16

D_MODEL = 1024
FOX_HEADS = 8
FOX_HEAD_DIM = 64
FOX_WIDTH = FOX_HEADS * FOX_HEAD_DIM
GLA_HEADS = 4
GLA_KEY_DIM = 128
GLA_VAL_DIM = 256
GLA_QK_WIDTH = GLA_HEADS * GLA_KEY_DIM
GLA_V_WIDTH = GLA_HEADS * GLA_VAL_DIM
GLA_GATE_RANK = 16
GLA_GATE_TEMP = 16.0
GLA_NORM_EPS = 1e-5
GLA_CHUNK = 64
PEER_HEADS = 8
PEER_N_KEYS = 128
PEER_HALF = 128
PEER_TOPK = 16
PEER_PICKS = PEER_HEADS * PEER_TOPK
DEPTH = 1
ALPHA = (2.0 * DEPTH) ** 0.25
LN_EPS = 1e-5

LANES = 128
SUBLANES = 8
VMEM_LIMIT = 52 * 1024 * 1024

IN_SPLIT_SIZES = (FOX_WIDTH, FOX_WIDTH, FOX_WIDTH, FOX_HEADS,
                  GLA_QK_WIDTH, GLA_QK_WIDTH, GLA_V_WIDTH, GLA_V_WIDTH, GLA_GATE_RANK,
                  D_MODEL, D_MODEL)
FF_COL = 0
GLR_COL = FOX_HEADS


def _dot(a, b, **kw):
    return jnp.dot(a, b, preferred_element_type=F32, **kw)


def _dot_nt(a, b):
    return lax.dot_general(a, b, (((1,), (1,)), ((), ())), preferred_element_type=F32)


def _dot_tn(a, b):
    return lax.dot_general(a, b, (((0,), (0,)), ((), ())), preferred_element_type=F32)


def _layer_norm(y, gain, bias):
    mu = jnp.mean(y, axis=-1, keepdims=True)
    yc = y - mu
    var = jnp.mean(yc * yc, axis=-1, keepdims=True)
    return yc * lax.rsqrt(var + LN_EPS) * gain + bias


def _in_proj_kernel(x_ref, wf_ref, wgqk_ref, wgv_ref, wgg_ref, wm_ref, ws_ref,
                    f_ref, gqk_ref, gv_ref, gg_ref, m_ref, s_ref):
    xb = x_ref[...].astype(BF16)
    f_ref[...] = _dot(xb, wf_ref[...]).astype(BF16)
    gqk_ref[...] = _dot(xb, wgqk_ref[...]).astype(BF16)
    gv_ref[...] = _dot(xb, wgv_ref[...]).astype(BF16)
    gg_ref[...] = _dot(xb, wgg_ref[...])
    m_ref[...] = _dot(xb, wm_ref[...])
    s_ref[...] = _dot(xb, ws_ref[...])


def _in_proj(x2, w_in):
    T = x2.shape[0]
    tm = 256
    pts = [0]
    for s in IN_SPLIT_SIZES:
        pts.append(pts[-1] + s)
    col = lambda i, j: w_in[:, pts[i]:pts[j]]
    wf = col(0, 3).astype(BF16)
    wgqk = col(4, 6).astype(BF16)
    wgv = col(6, 7).astype(BF16)
    wgg = col(7, 8).astype(BF16)
    wm = col(9, 11).astype(BF16)
    ws = jnp.concatenate([col(3, 4), col(8, 9)], axis=1)
    ws = jnp.pad(ws, ((0, 0), (0, LANES - ws.shape[1]))).astype(BF16)
    ws_list = [wf, wgqk, wgv, wgg, wm, ws]
    out_dtypes = [BF16, BF16, BF16, F32, F32, F32]
    const = lambda w: pl.BlockSpec(w.shape, lambda i: (0, 0))
    return pl.pallas_call(
        _in_proj_kernel,
        grid=(T // tm,),
        in_specs=[pl.BlockSpec((tm, D_MODEL), lambda i: (i, 0))] + [const(w) for w in ws_list],
        out_specs=[pl.BlockSpec((tm, w.shape[1]), lambda i: (i, 0)) for w in ws_list],
        out_shape=[jax.ShapeDtypeStruct((T, w.shape[1]), dt) for w, dt in zip(ws_list, out_dtypes)],
        compiler_params=pltpu.CompilerParams(dimension_semantics=("arbitrary",), vmem_limit_bytes=VMEM_LIMIT),
        name="in_proj",
    )(x2, *ws_list)


def _fox_gate_kernel(s_ref, bias_ref, c_ref):
    S = s_ref.shape[0]
    ff_t = s_ref[...].T[FF_COL:FF_COL + FOX_HEADS, :]
    log_f = jax.nn.log_sigmoid(ff_t + bias_ref[...])
    r = lax.broadcasted_iota(jnp.int32, (LANES, LANES), 0)
    c = lax.broadcasted_iota(jnp.int32, (LANES, LANES), 1)
    tri = (r <= c).astype(F32)
    carry = jnp.zeros((FOX_HEADS, 1), F32)
    for j in range(S // LANES):
        blk = log_f[:, j * LANES:(j + 1) * LANES]
        cs = _dot(blk, tri, precision=lax.Precision.HIGHEST) + carry
        c_ref[0, :, j * LANES:(j + 1) * LANES] = cs
        carry = cs[:, LANES - 1:LANES]


def _fox_gate(small, fox_f_bias, B, S):
    return pl.pallas_call(
        _fox_gate_kernel,
        grid=(B,),
        in_specs=[pl.BlockSpec((S, LANES), lambda b: (b, 0)),
                  pl.BlockSpec((FOX_HEADS, 1), lambda b: (0, 0))],
        out_specs=pl.BlockSpec((1, FOX_HEADS, S), lambda b: (b, 0, 0)),
        out_shape=jax.ShapeDtypeStruct((B, FOX_HEADS, S), F32),
        compiler_params=pltpu.CompilerParams(dimension_semantics=("arbitrary",)),
        name="fox_gate",
    )(small, fox_f_bias.reshape(FOX_HEADS, 1))


FOX_TQ = 256
FOX_TK = FOX_TQ
FOX_COLS = 256


def _fox_attn_kernel(q_ref, k_ref, v_ref, c_ref, o_ref):
    qi = pl.program_id(2)
    tq, tk, dh = FOX_TQ, FOX_TK, FOX_HEAD_DIM
    n_h = FOX_COLS // dh
    qs = [q_ref[:, hh * dh:(hh + 1) * dh] * (dh ** -0.5) for hh in range(n_h)]

    def step(j, carry, masked):
        ks = pl.ds(pl.multiple_of(j * tk, tk), tk)
        k2 = k_ref[ks, :]
        v2 = v_ref[ks, :]
        out = []
        for hh in range(n_h):
            m, l, acc = carry[hh]
            s = _dot_nt(qs[hh], k2[:, hh * dh:(hh + 1) * dh]) - c_ref[hh, :, ks]
            if masked:
                r = lax.broadcasted_iota(jnp.int32, (tq, tk), 0)
                c = lax.broadcasted_iota(jnp.int32, (tq, tk), 1)
                s = jnp.where(c <= r, s, -jnp.inf)
            m_new = jnp.maximum(m, jnp.max(s, axis=1, keepdims=True))
            p = jnp.exp(s - m_new)
            a = jnp.exp(m - m_new)
            l = a * l + jnp.sum(p, axis=1, keepdims=True)
            acc = a * acc + _dot(p.astype(BF16), v2[:, hh * dh:(hh + 1) * dh])
            out.append((m_new, l, acc))
        return tuple(out)

    init = tuple((jnp.full((tq, 1), -jnp.inf, F32), jnp.zeros((tq, 1), F32), jnp.zeros((tq, dh), F32))
                 for _ in range(n_h))
    carry = lax.fori_loop(0, qi, lambda j, c: step(j, c, False), init)
    carry = step(qi, carry, True)
    o_ref[...] = jnp.concatenate([acc / l for _, l, acc in carry], axis=1).astype(BF16)


def _fox_attn(fqkv, c_t, B, S):
    T = B * S
    nq = S // FOX_TQ
    n_hp = FOX_WIDTH // FOX_COLS
    return pl.pallas_call(
        _fox_attn_kernel,
        grid=(B, n_hp, nq),
        in_specs=[pl.BlockSpec((FOX_TQ, FOX_COLS), lambda b, h, i: (b * nq + i, h)),
                  pl.BlockSpec((S, FOX_COLS), lambda b, h, i: (b, n_hp + h)),
                  pl.BlockSpec((S, FOX_COLS), lambda b, h, i: (b, 2 * n_hp + h)),
                  pl.BlockSpec((FOX_COLS // FOX_HEAD_DIM, 1, S), lambda b, h, i: (b * n_hp + h, 0, 0))],
        out_specs=pl.BlockSpec((FOX_TQ, FOX_COLS), lambda b, h, i: (b * nq + i, h)),
        out_shape=jax.ShapeDtypeStruct((T, FOX_WIDTH), BF16),
        compiler_params=pltpu.CompilerParams(dimension_semantics=("arbitrary", "arbitrary", "arbitrary")),
        name="fox_attn",
    )(fqkv, fqkv, fqkv, c_t.reshape(B * FOX_HEADS, 1, S))


GLA_GROUP = 4


def _gla_kernel(q_ref, k_ref, v_ref, s_ref, up_ref, gb_ref, gg_ref, gain_ref, o_ref):
    S = q_ref.shape[0]
    C = GLA_CHUNK
    dk, dv = GLA_KEY_DIM, GLA_VAL_DIM
    r = lax.broadcasted_iota(jnp.int32, (C, C), 0)
    c = lax.broadcasted_iota(jnp.int32, (C, C), 1)
    tril = (r >= c).astype(F32)

    def body(ci, states):
        rows = pl.ds(pl.multiple_of(ci * C, C), C)
        z = _dot(s_ref[rows, :].astype(BF16), up_ref[...]) + gb_ref[...]
        la = jax.nn.log_sigmoid(z) * (1.0 / GLA_GATE_TEMP)
        cum = _dot(tril, la, precision=lax.Precision.HIGHEST)
        tot = cum[C - 1:C, :]
        kd_all = (k_ref[rows, :].astype(F32) * jnp.exp(tot - cum)).astype(BF16)
        decay = jnp.exp(tot)
        new_states = []
        for g, st_t in enumerate(states):
            kc, vc = slice(g * dk, (g + 1) * dk), slice(g * dv, (g + 1) * dv)
            st_t = st_t * decay[:, kc] + _dot_tn(v_ref[rows, vc], kd_all[:, kc])
            o = _dot_nt(q_ref[rows, kc], st_t.astype(BF16)) * (dk ** -0.5)
            o = o * lax.rsqrt(jnp.mean(o * o, axis=-1, keepdims=True) + GLA_NORM_EPS) * gain_ref[:, vc]
            o = o * jax.nn.silu(gg_ref[rows, vc])
            o_ref[rows, vc] = o.astype(BF16)
            new_states.append(st_t)
        return tuple(new_states)

    lax.fori_loop(0, S // C, body, tuple(jnp.zeros((dv, dk), F32) for _ in range(GLA_GROUP)))


def _gla(gqk, gv, small, gg, gla_gate_up, gla_gate_bias, gla_norm_gain, B, S):
    T = B * S
    up = jnp.zeros((LANES, GLA_QK_WIDTH), F32).at[GLR_COL:GLR_COL + GLA_GATE_RANK].set(gla_gate_up).astype(BF16)
    gb = gla_gate_bias.reshape(1, GLA_QK_WIDTH)
    gain = gla_norm_gain.reshape(1, GLA_V_WIDTH)
    n_groups = GLA_HEADS // GLA_GROUP
    kw, vw = GLA_GROUP * GLA_KEY_DIM, GLA_GROUP * GLA_VAL_DIM
    return pl.pallas_call(
        _gla_kernel,
        grid=(B, n_groups),
        in_specs=[pl.BlockSpec((S, kw), lambda b, h: (b, h)),
                  pl.BlockSpec((S, kw), lambda b, h: (b, n_groups + h)),
                  pl.BlockSpec((S, vw), lambda b, h: (b, h)),
                  pl.BlockSpec((S, LANES), lambda b, h: (b, 0)),
                  pl.BlockSpec((LANES, kw), lambda b, h: (0, h)),
                  pl.BlockSpec((1, kw), lambda b, h: (0, h)),
                  pl.BlockSpec((S, vw), lambda b, h: (b, h)),
                  pl.BlockSpec((1, vw), lambda b, h: (0, h))],
        out_specs=pl.BlockSpec((S, vw), lambda b, h: (b, h)),
        out_shape=jax.ShapeDtypeStruct((T, GLA_V_WIDTH), BF16),
        compiler_params=pltpu.CompilerParams(dimension_semantics=("arbitrary", "arbitrary")),
        name="gla",
    )(gqk, gqk, gv, small, up, gb, gg, gain)


def _merge_kernel(a_ref, og_ref, m_ref, x_ref, wf_ref, wg_ref, wo_ref, g_ref, b_ref, o_ref):
    y_fox = _dot(a_ref[...], wf_ref[...])
    y_gla = _dot(og_ref[...], wg_ref[...])
    merged = (jax.nn.sigmoid(m_ref[:, :D_MODEL]) * y_fox + jax.nn.sigmoid(m_ref[:, D_MODEL:]) * y_gla)
    mix = _dot(merged.astype(BF16), wo_ref[...])
    o_ref[...] = _layer_norm(ALPHA * x_ref[...] + mix, g_ref[...], b_ref[...])


def _merge(attn, og, mg, x2, w_out_fox, w_out_gla, w_out, ln_gain, ln_bias):
    T = x2.shape[0]
    tm = 512
    row = lambda w: pl.BlockSpec((tm, w), lambda i: (i, 0))
    const = lambda a: pl.BlockSpec(a.shape, lambda i: (0, 0))
    ws = [w_out_fox.astype(BF16), w_out_gla.astype(BF16), w_out.astype(BF16),
          ln_gain.reshape(1, D_MODEL), ln_bias.reshape(1, D_MODEL)]
    return pl.pallas_call(
        _merge_kernel,
        grid=(T // tm,),
        in_specs=[row(FOX_WIDTH), row(GLA_V_WIDTH), row(2 * D_MODEL), row(D_MODEL)] + [const(w) for w in ws],
        out_specs=row(D_MODEL),
        out_shape=jax.ShapeDtypeStruct((T, D_MODEL), F32),
        compiler_params=pltpu.CompilerParams(dimension_semantics=("arbitrary",), vmem_limit_bytes=VMEM_LIMIT),
        name="merge",
    )(attn, og, mg, x2, *ws)


ROUTE_TM = 256
CAND_BLOCKS = 10


def _sublane_all(x, op):
    for shift in (4, 2, 1):
        x = op(x, pltpu.roll(x, shift, axis=0))
    return x


def _take_max(s3, iota3, sentinel):
    m8 = _sublane_all(jnp.max(s3, axis=0), jnp.maximum)
    idx8 = _sublane_all(jnp.min(jnp.where(s3 == m8[None], iota3, sentinel), axis=0), jnp.minimum)
    return m8, idx8, iota3 == idx8[None]


def _row_iota(groups, n):
    shape = (groups, SUBLANES, n)
    return lax.broadcasted_iota(jnp.int32, shape, 0) * SUBLANES + lax.broadcasted_iota(jnp.int32, shape, 1)


def _route_kernel(x_ref, wq_ref, keys_ref, after_a_ref, after_b_ref, idx_ref, gate_ref,
                  q_scr, st_scr, it_scr, best_scr, pick_scr, gsel_scr):
    del after_a_ref, after_b_ref
    tm = ROUTE_TM
    K = PEER_TOPK
    q_scr[...] = _dot(x_ref[...].astype(BF16), wq_ref[...])
    key_iota = _row_iota(PEER_N_KEYS // SUBLANES, tm)

    def stage1(pair, _):
        hps = (2 * pair, 2 * pair + 1)
        ss = []
        for hp in hps:
            q = q_scr[:, pl.ds(pl.multiple_of(hp * PEER_HALF, PEER_HALF), PEER_HALF)].astype(BF16)
            ss.append(_dot_nt(keys_ref[hp], q).reshape(PEER_N_KEYS // SUBLANES, SUBLANES, tm))
        for i in range(K):
            for n, hp in enumerate(hps):
                m8, idx8, hit = _take_max(ss[n], key_iota, PEER_N_KEYS)
                st_scr[hp, i:i + 1, :] = m8[0:1]
                it_scr[hp, i:i + 1, :] = idx8[0:1]
                ss[n] = jnp.where(hit, -jnp.inf, ss[n])
        return 0

    lax.fori_loop(0, PEER_HEADS, stage1, 0)

    cand_iota = _row_iota(CAND_BLOCKS, tm)

    def stage2(h, _):
        s0, s1 = st_scr[2 * h], st_scr[2 * h + 1]
        i0, i1 = it_scr[2 * h], it_scr[2 * h + 1]
        lo, hi = slice(0, SUBLANES), slice(SUBLANES, 2 * SUBLANES)
        cs = [s0[0:1] + s1[lo], s0[0:1] + s1[hi]]
        ci = [i0[0:1] * PEER_N_KEYS + i1[lo], i0[0:1] * PEER_N_KEYS + i1[hi]]
        for a in range(1, SUBLANES):
            cs.append(s0[a:a + 1] + s1[lo])
            ci.append(i0[a:a + 1] * PEER_N_KEYS + i1[lo])
        cs.append(s0[hi] + s1[0:1])
        ci.append(i0[hi] * PEER_N_KEYS + i1[0:1])
        cand = jnp.stack(cs, axis=0)
        cidx = jnp.stack(ci, axis=0)
        for i in range(K):
            m8, _, hit = _take_max(cand, cand_iota, CAND_BLOCKS * SUBLANES)
            pick8 = _sublane_all(jnp.max(jnp.where(hit, cidx, -1), axis=0), jnp.maximum)
            best_scr[i:i + 1, :] = m8[0:1]
            pick_scr[pl.ds(h * K + i, 1), :] = pick8[0:1]
            cand = jnp.where(hit, -jnp.inf, cand)
        best = best_scr[...]
        e = jnp.exp(best - best[0:1])
        gsel_scr[pl.ds(pl.multiple_of(h * K, K), K), :] = e / jnp.sum(e, axis=0, keepdims=True)
        return 0

    lax.fori_loop(0, PEER_HEADS, stage2, 0)
    idx_ref[...] = pick_scr[...].T
    gate_ref[...] = gsel_scr[...].T


def _route(x1, peer_w_query, peer_sub_keys, after, after_b):
    T = x1.shape[0]
    tm = ROUTE_TM
    wq = peer_w_query.reshape(D_MODEL, 2 * PEER_HEADS * PEER_HALF).astype(BF16)
    keys = peer_sub_keys.reshape(2 * PEER_HEADS, PEER_N_KEYS, PEER_HALF).astype(BF16)
    picks = pl.BlockSpec((tm, PEER_PICKS), lambda i: (i, 0))
    return pl.pallas_call(
        _route_kernel,
        grid=(T // tm,),
        in_specs=[pl.BlockSpec((tm, D_MODEL), lambda i: (i, 0)),
                  pl.BlockSpec(wq.shape, lambda i: (0, 0)),
                  pl.BlockSpec(keys.shape, lambda i: (0, 0, 0)),
                  pl.BlockSpec((SUBLANES, LANES), lambda i: (0, 0)),
                  pl.BlockSpec((SUBLANES, LANES), lambda i: (0, 0))],
        out_specs=[picks, picks],
        out_shape=[jax.ShapeDtypeStruct((T, PEER_PICKS), jnp.int32),
                   jax.ShapeDtypeStruct((T, PEER_PICKS), F32)],
        scratch_shapes=[pltpu.VMEM((tm, 2 * PEER_HEADS * PEER_HALF), F32),
                        pltpu.VMEM((2 * PEER_HEADS, PEER_TOPK, tm), F32),
                        pltpu.VMEM((2 * PEER_HEADS, PEER_TOPK, tm), jnp.int32),
                        pltpu.VMEM((PEER_TOPK, tm), F32),
                        pltpu.VMEM((PEER_PICKS, tm), jnp.int32),
                        pltpu.VMEM((PEER_PICKS, tm), F32)],
        compiler_params=pltpu.CompilerParams(dimension_semantics=("arbitrary",), vmem_limit_bytes=VMEM_LIMIT),
        name="peer_route",
    )(x1, wq, keys, after, after_b)


SC_LANES = 16
SC_ROWS = 64
SC_CHUNKS = PEER_PICKS // SC_ROWS
SC_DCOLS = 256
SC_DVREGS = SC_DCOLS // SC_LANES
SC_DOT_COLS = 512
SC_PART = (D_MODEL // SC_DOT_COLS) * SC_LANES


def _sc_mesh():
    info = plsc.get_sparse_core_info()
    mesh = plsc.VectorSubcoreMesh(core_axis_name="c", subcore_axis_name="s")
    return mesh, info.num_cores, info.num_cores * info.num_subcores


def _sc_token_pipeline(tab_hbm, idx_hbm, vec_hbm, out_hbm, idx_v, vec_v, rows_v, out_v,
                       sem_rows, sem_tok, sem_out, n_cores, tpw, compute_chunk):
    wid = lax.axis_index("s") * n_cores + lax.axis_index("c")
    base = wid * tpw

    def gather(s, c):
        return pltpu.make_async_copy(tab_hbm.at[idx_v.at[s, pl.ds(c * SC_ROWS, SC_ROWS)]],
                                     rows_v.at[c % 2], sem_rows.at[c % 2])

    def tok_fetch(s, tok):
        return (pltpu.make_async_copy(idx_hbm.at[tok], idx_v.at[s], sem_tok.at[0]),
                pltpu.make_async_copy(vec_hbm.at[tok], vec_v.at[s], sem_tok.at[1]))

    def out_copy(s, tok):
        return pltpu.make_async_copy(out_v.at[s], out_hbm.at[tok], sem_out.at[s])

    for d in tok_fetch(0, base):
        d.start()
    for d in tok_fetch(0, base):
        d.wait()
    gather(0, 0).start()

    def pair_body(tp, _):
        for s in range(2):
            t = tp * 2 + s
            tok = base + t
            has_next = t + 1 < tpw

            @pl.when(has_next)
            def _():
                for d in tok_fetch(1 - s, tok + 1):
                    d.start()

            @pl.when(t >= 2)
            def _():
                out_copy(s, tok - 2).wait()

            for c in range(SC_CHUNKS):
                if c + 1 < SC_CHUNKS:
                    gather(s, c + 1).start()
                else:
                    @pl.when(has_next)
                    def _():
                        for d in tok_fetch(1 - s, tok + 1):
                            d.wait()
                        gather(1 - s, 0).start()
                gather(s, c).wait()
                compute_chunk(s, c)
            out_copy(s, tok).start()
        return 0

    lax.fori_loop(0, tpw // 2, pair_body, 0)
    for s in range(2):
        out_copy(s, base + tpw - 2 + s).wait()


def _sc_scratch(vec_len, out_len):
    return [pltpu.VMEM((2, PEER_PICKS), jnp.int32),
            pltpu.VMEM((2, vec_len), F32),
            pltpu.VMEM((2, SC_ROWS, D_MODEL // 2), jnp.int32),
            pltpu.VMEM((2, out_len), F32),
            pltpu.SemaphoreType.DMA((2,)),
            pltpu.SemaphoreType.DMA((2,)),
            pltpu.SemaphoreType.DMA((2,))]


def _pack_table(tab):
    n, d = tab.shape
    bits = lax.bitcast_convert_type(tab.astype(BF16), jnp.uint16).astype(jnp.uint32).reshape(n, d // 32, 2, SC_LANES)
    words = bits[:, :, 0, :] | (bits[:, :, 1, :] << 16)
    return lax.bitcast_convert_type(words, jnp.int32).reshape(n, d // 2)


def _unpack_pairs(x):
    return list(plsc.unpack(x, format=plsc.PackFormat.INTERLEAVED))


def _sc_dot(u_packed, idx, h):
    T = h.shape[0]
    mesh, n_cores, n_workers = _sc_mesh()
    tpw = T // n_workers
    L = SC_LANES

    @functools.partial(
        pl.kernel, mesh=mesh, out_type=jax.ShapeDtypeStruct((T, PEER_PICKS), F32),
        scratch_types=_sc_scratch(D_MODEL, PEER_PICKS) + [pltpu.VMEM((SC_ROWS, SC_PART), F32)],
        compiler_params=pltpu.CompilerParams(needs_layout_passes=False), name="peer_sc_dot")
    def k(u_hbm, idx_hbm, h_hbm, act_hbm, idx_v, h_v, rows_v, act_v, sem_rows, sem_tok, sem_out, part_v):
        lane = lax.iota(jnp.int32, L)

        def compute_chunk(s, c):
            b = c % 2
            for dc in range(D_MODEL // SC_DOT_COLS):
                hq = [plsc.pack(h_v[s, pl.ds(dc * SC_DOT_COLS + 2 * L * j, L)],
                                h_v[s, pl.ds(dc * SC_DOT_COLS + 2 * L * j + L, L)],
                                format=plsc.PackFormat.INTERLEAVED) for j in range(SC_DOT_COLS // (2 * L))]

                @plsc.parallel_loop(0, SC_ROWS, unroll=2)
                def _(r, dc=dc, hq=hq):
                    prods = []
                    for j, hj in enumerate(hq):
                        words = rows_v[b, r, pl.ds(dc * (SC_DOT_COLS // 2) + j * L, L)]
                        prods += _unpack_pairs(plsc.bitcast(words, BF16) * hj)
                    while len(prods) > 1:
                        prods = [prods[i] + prods[i + 1] for i in range(0, len(prods), 2)]
                    part_v[r, pl.ds(dc * L, L)] = prods[0]
            for g in range(SC_ROWS // L):
                rows16 = lane + g * L

                def col_body(col, tots):
                    rot = (lane + col) & (L - 1)
                    return tuple(t + plsc.load_gather(part_v, [rows16, rot + q * L]) for q, t in enumerate(tots))

                tots = lax.fori_loop(0, L, col_body, tuple(jnp.zeros((L,), F32) for _ in range(SC_PART // L)))
                act_v[s, pl.ds(c * SC_ROWS + g * L, L)] = functools.reduce(jnp.add, tots)

        _sc_token_pipeline(u_hbm, idx_hbm, h_hbm, act_hbm, idx_v, h_v, rows_v, act_v,
                           sem_rows, sem_tok, sem_out, n_cores, tpw, compute_chunk)

    return k(u_packed, idx, h)


def _sc_acc(v_packed, idx, w):
    T = w.shape[0]
    mesh, n_cores, n_workers = _sc_mesh()
    tpw = T // n_workers
    L = SC_LANES

    @functools.partial(
        pl.kernel, mesh=mesh, out_type=jax.ShapeDtypeStruct((T, D_MODEL), F32),
        scratch_types=_sc_scratch(PEER_PICKS, D_MODEL),
        compiler_params=pltpu.CompilerParams(needs_layout_passes=False), name="peer_sc_acc")
    def k(v_hbm, idx_hbm, w_hbm, out_hbm, idx_v, w_v, rows_v, out_v, sem_rows, sem_tok, sem_out):
        def compute_chunk(s, c):
            b = c % 2
            for dc in range(D_MODEL // SC_DCOLS):
                cols = [pl.ds(dc * SC_DCOLS + j * L, L) for j in range(SC_DVREGS)]
                if c == 0:
                    accs = tuple(jnp.zeros((L,), F32) for _ in cols)
                else:
                    accs = tuple(out_v[s, cs] for cs in cols)

                def row_body(r, accs, dc=dc):
                    wb = plsc.load_gather(w_v.at[s], [jnp.full((L,), c * SC_ROWS, jnp.int32) + r])
                    new = []
                    for j in range(SC_DVREGS // 2):
                        words = rows_v[b, r, pl.ds(dc * (SC_DCOLS // 2) + j * L, L)]
                        lo, hi = _unpack_pairs(plsc.bitcast(words, BF16))
                        new += [accs[2 * j] + wb * lo, accs[2 * j + 1] + wb * hi]
                    return tuple(new)

                accs = lax.fori_loop(0, SC_ROWS, row_body, accs)
                for a, cs in zip(accs, cols):
                    out_v[s, cs] = a

        _sc_token_pipeline(v_hbm, idx_hbm, w_hbm, out_hbm, idx_v, w_v, rows_v, out_v,
                           sem_rows, sem_tok, sem_out, n_cores, tpw, compute_chunk)

    return k(v_packed, idx, w)


DENSE_TM = 2048
DENSE_TN = 1024


def _dense_act_kernel(x_ref, u_ref, o_ref):
    o_ref[...] = _dot_nt(x_ref[...].astype(BF16), u_ref[...])


def _dense_act(x1, u_bf16):
    T, N = x1.shape[0], u_bf16.shape[0]
    return pl.pallas_call(
        _dense_act_kernel, grid=(T // DENSE_TM, N // DENSE_TN),
        in_specs=[pl.BlockSpec((DENSE_TM, D_MODEL), lambda i, j: (i, 0)),
                  pl.BlockSpec((DENSE_TN, D_MODEL), lambda i, j: (j, 0))],
        out_specs=pl.BlockSpec((DENSE_TM, DENSE_TN), lambda i, j: (i, j)),
        out_shape=jax.ShapeDtypeStruct((T, N), F32),
        compiler_params=pltpu.CompilerParams(dimension_semantics=("arbitrary", "arbitrary"),
                                             vmem_limit_bytes=VMEM_LIMIT),
        name="peer_dense_act",
    )(x1, u_bf16)


PICK_TG = SUBLANES
PICK_NQ = 4096


def _sc_pick(dense, idx):
    T, N = dense.shape
    mesh, n_cores, n_workers = _sc_mesh()
    gpw = T // PICK_TG // n_workers
    n_q = N // PICK_NQ
    L = SC_LANES

    @functools.partial(
        pl.kernel, mesh=mesh, out_type=jax.ShapeDtypeStruct((T, PEER_PICKS), F32),
        scratch_types=[pltpu.VMEM((2, PICK_TG, PEER_PICKS), jnp.int32), pltpu.VMEM((2, PICK_TG, PICK_NQ), F32),
                       pltpu.VMEM((2, PICK_TG, PEER_PICKS), F32),
                       pltpu.SemaphoreType.DMA((2,)), pltpu.SemaphoreType.DMA((2,)), pltpu.SemaphoreType.DMA((2,))],
        compiler_params=pltpu.CompilerParams(needs_layout_passes=False), name="peer_sc_pick")
    def k(d_hbm, idx_hbm, act_hbm, idx_v, buf, act_v, sem_idx, sem_buf, sem_out):
        g0 = (lax.axis_index("s") * n_cores + lax.axis_index("c")) * gpw

        def idx_fetch(gs, g):
            return pltpu.make_async_copy(idx_hbm.at[pl.ds(g * PICK_TG, PICK_TG)], idx_v.at[gs], sem_idx.at[gs])

        def blk_fetch(g, q):
            return pltpu.make_async_copy(d_hbm.at[pl.ds(g * PICK_TG, PICK_TG), pl.ds(q * PICK_NQ, PICK_NQ)],
                                         buf.at[q % 2], sem_buf.at[q % 2])

        def out_copy(gs, g):
            return pltpu.make_async_copy(act_v.at[gs], act_hbm.at[pl.ds(g * PICK_TG, PICK_TG)], sem_out.at[gs])

        idx_fetch(0, g0).start()
        blk_fetch(g0, 0).start()

        def pair_body(gp, _):
            for gs in range(2):
                gi = gp * 2 + gs
                g = g0 + gi
                has_next = gi + 1 < gpw

                @pl.when(has_next)
                def _():
                    idx_fetch(1 - gs, g + 1).start()

                idx_fetch(gs, g).wait()

                @pl.when(gi >= 2)
                def _():
                    out_copy(gs, g - 2).wait()

                for q in range(n_q):
                    if q + 1 < n_q:
                        blk_fetch(g, q + 1).start()
                    else:
                        @pl.when(has_next)
                        def _():
                            blk_fetch(g + 1, 0).start()
                    blk_fetch(g, q).wait()
                    for t in range(PICK_TG):
                        for v in range(PEER_PICKS // L):
                            picks = pl.ds(v * L, L)
                            local = idx_v[gs, t, picks] - q * PICK_NQ
                            inside = (local >= 0) & (local < PICK_NQ)
                            val = plsc.load_gather(buf, [jnp.full((L,), q % 2, jnp.int32), jnp.full((L,), t, jnp.int32),
                                                         jnp.where(inside, local, 0)])
                            act_v[gs, t, picks] = val if q == 0 else jnp.where(inside, val, act_v[gs, t, picks])
                out_copy(gs, g).start()
            return 0

        lax.fori_loop(0, gpw // 2, pair_body, 0)
        for gs in range(2):
            out_copy(gs, g0 + gpw - 2 + gs).wait()

    return k(dense, idx)


def _gelu_gate_kernel(a_ref, g_ref, o_ref):
    a = a_ref[...]
    o_ref[...] = 0.5 * a * (1.0 + lax.erf(a * (2.0 ** -0.5))) * g_ref[...]


def _ln2_kernel(h_ref, f_ref, g_ref, b_ref, full_ref, o_ref):
    del full_ref
    o_ref[...] = _layer_norm(ALPHA * h_ref[...] + f_ref[...], g_ref[...], b_ref[...])


MIX_TM = 512
_MIX_PARAMS = pltpu.CompilerParams(dimension_semantics=("arbitrary",))


def _mix_row(width):
    return pl.BlockSpec((MIX_TM, width), lambda i: (i, 0))


def _gelu_gate_call(act, gates):
    T = act.shape[0]
    return pl.pallas_call(
        _gelu_gate_kernel, grid=(T // MIX_TM,), in_specs=[_mix_row(PEER_PICKS)] * 2, out_specs=_mix_row(PEER_PICKS),
        out_shape=jax.ShapeDtypeStruct((T, PEER_PICKS), F32), compiler_params=_MIX_PARAMS, name="peer_gelu_gate",
    )(act, gates)


def _mix_finish(x1, idx, w, v_packed, ln_gain, ln_bias, result, row0):
    T = x1.shape[0]
    const = pl.BlockSpec((1, D_MODEL), lambda i: (0, 0))
    blk0 = row0 // MIX_TM
    ffn = _sc_acc(v_packed, idx, w)
    result = pl.pallas_call(
        _ln2_kernel, grid=(T // MIX_TM,),
        in_specs=[_mix_row(D_MODEL)] * 2 + [const, const, pl.BlockSpec(memory_space=pl.ANY)],
        out_specs=pl.BlockSpec((MIX_TM, D_MODEL), lambda i: (blk0 + i, 0)),
        out_shape=jax.ShapeDtypeStruct(result.shape, F32), input_output_aliases={4: 0},
        compiler_params=_MIX_PARAMS, name="peer_ln2",
    )(x1, ffn, ln_gain.reshape(1, D_MODEL), ln_bias.reshape(1, D_MODEL), result)
    return result, ffn


BATCH_CHUNKS = 16
SC_GATE_LAG = 1
SC_PASS_LAG = 3
DENSE_PATTERN = (False, True, True, True, True, True, True, True)


def kernel(x, w_in, fox_f_bias, gla_gate_up, gla_gate_bias, gla_norm_gain, w_out_fox, w_out_gla, w_out,
           ln1_gain, ln1_bias, peer_w_query, peer_sub_keys, peer_expert_u, peer_expert_v, ln2_gain, ln2_bias):
    B, S, D = x.shape
    assert D == D_MODEL and S % max(ROUTE_TM, FOX_TQ) == 0
    assert DEPTH == 1 and w_in.shape[0] == 1, "the chunk pipeline below is written for the single-layer block"
    u_packed, v_packed = _pack_table(peer_expert_u[0]), _pack_table(peer_expert_v[0])
    u_bf16 = peer_expert_u[0].astype(BF16)
    n_chunks = BATCH_CHUNKS if B % BATCH_CHUNKS == 0 else 1
    bc = B // n_chunks
    ffns, ws = [], []
    result = jnp.zeros((B * S, D), F32)
    for ci in range(n_chunks):
        h = x[ci * bc:(ci + 1) * bc].reshape(bc * S, D)
        fqkv, gqk, gv, gg, mg, small = _in_proj(h, w_in[0])
        c_t = _fox_gate(small, fox_f_bias[0], bc, S)
        attn = _fox_attn(fqkv, c_t, bc, S)
        og = _gla(gqk, gv, small, gg, gla_gate_up[0], gla_gate_bias[0], gla_norm_gain[0], bc, S)
        x1 = _merge(attn, og, mg, h, w_out_fox[0], w_out_gla[0], w_out[0], ln1_gain[0], ln1_bias[0])
        zero = jnp.zeros((SUBLANES, LANES), F32)
        after = ws[ci - SC_GATE_LAG][:SUBLANES, :LANES] if ci >= SC_GATE_LAG else zero
        after_b = ffns[ci - SC_PASS_LAG][:SUBLANES, :LANES] if ci >= SC_PASS_LAG else zero
        idx, gates = _route(x1, peer_w_query[0], peer_sub_keys[0], after, after_b)
        if DENSE_PATTERN[ci % len(DENSE_PATTERN)]:
            act = _sc_pick(_dense_act(x1, u_bf16), idx)
        else:
            act = _sc_dot(u_packed, idx, x1)
        w = _gelu_gate_call(act, gates)
        result, ffn = _mix_finish(x1, idx, w, v_packed, ln2_gain[0], ln2_bias[0], result, ci * bc * S)
        ffns.append(ffn)
        ws.append(w)
    return result.reshape(B, S, D)
```

```python
import functools
import math

import jax
import jax.numpy as jnp
from jax import lax
from jax.experimental import pallas as pl
from jax.experimental.pallas import tpu as pltpu
from jax.experimental.pallas import tpu_sc as plsc

F32 = jnp.float32
BF16 = jnp.bfloat16

D_MODEL = 1024
FOX_HEADS = 8
FOX_HEAD_DIM = 64
FOX_WIDTH = FOX_HEADS * FOX_HEAD_DIM
GLA_HEADS = 4
GLA_KEY_DIM = 128
GLA_VAL_DIM = 256
GLA_QK_WIDTH = GLA_HEADS * GLA_KEY_DIM
GLA_V_WIDTH = GLA_HEADS * GLA_VAL_DIM
GLA_GATE_RANK = 16
GLA_GATE_TEMP = 16.0
GLA_NORM_EPS = 1e-5
GLA_CHUNK = 64
PEER_HEADS = 8
PEER_N_KEYS = 128
PEER_HALF = 128
PEER_TOPK = 16
PEER_PICKS = PEER_HEADS * PEER_TOPK
DEPTH = 1
ALPHA = (2.0 * DEPTH) ** 0.25
LN_EPS = 1e-5

LANES = 128
SUBLANES = 8
VMEM_LIMIT = 52 * 1024 * 1024

IN_SPLIT_SIZES = (FOX_WIDTH, FOX_WIDTH, FOX_WIDTH, FOX_HEADS,
                  GLA_QK_WIDTH, GLA_QK_WIDTH, GLA_V_WIDTH, GLA_V_WIDTH, GLA_GATE_RANK,
                  D_MODEL, D_MODEL)
FF_COL = 0
GLR_COL = FOX_HEADS


def _dot(a, b, **kw):
    return jnp.dot(a, b, preferred_element_type=F32, **kw)


def _dot_nt(a, b):
    return lax.dot_general(a, b, (((1,), (1,)), ((), ())), preferred_element_type=F32)


def _dot_tn(a, b):
    return lax.dot_general(a, b, (((0,), (0,)), ((), ())), preferred_element_type=F32)


def _layer_norm(y, gain, bias):
    mu = jnp.mean(y, axis=-1, keepdims=True)
    yc = y - mu
    var = jnp.mean(yc * yc, axis=-1, keepdims=True)
    return yc * lax.rsqrt(var + LN_EPS) * gain + bias


def _in_proj_kernel(x_ref, wf_ref, wgqk_ref, wgv_ref, wgg_ref, wm_ref, ws_ref,
                    f_ref, gqk_ref, gv_ref, gg_ref, m_ref, s_ref):
    xb = x_ref[...].astype(BF16)
    f_ref[...] = _dot(xb, wf_ref[...]).astype(BF16)
    gqk_ref[...] = _dot(xb, wgqk_ref[...]).astype(BF16)
    gv_ref[...] = _dot(xb, wgv_ref[...]).astype(BF16)
    gg_ref[...] = _dot(xb, wgg_ref[...])
    m_ref[...] = _dot(xb, wm_ref[...])
    s_ref[...] = _dot(xb, ws_ref[...])


def _in_proj(x2, w_in):
    T = x2.shape[0]
    tm = 256
    pts = [0]
    for s in IN_SPLIT_SIZES:
        pts.append(pts[-1] + s)
    col = lambda i, j: w_in[:, pts[i]:pts[j]]
    wf = col(0, 3).astype(BF16)
    wgqk = col(4, 6).astype(BF16)
    wgv = col(6, 7).astype(BF16)
    wgg = col(7, 8).astype(BF16)
    wm = col(9, 11).astype(BF16)
    ws = jnp.concatenate([col(3, 4), col(8, 9)], axis=1)
    ws = jnp.pad(ws, ((0, 0), (0, LANES - ws.shape[1]))).astype(BF16)
    ws_list = [wf, wgqk, wgv, wgg, wm, ws]
    out_dtypes = [BF16, BF16, BF16, F32, F32, F32]
    const = lambda w: pl.BlockSpec(w.shape, lambda i: (0, 0))
    return pl.pallas_call(
        _in_proj_kernel,
        grid=(T // tm,),
        in_specs=[pl.BlockSpec((tm, D_MODEL), lambda i: (i, 0))] + [const(w) for w in ws_list],
        out_specs=[pl.BlockSpec((tm, w.shape[1]), lambda i: (i, 0)) for w in ws_list],
        out_shape=[jax.ShapeDtypeStruct((T, w.shape[1]), dt) for w, dt in zip(ws_list, out_dtypes)],
        compiler_params=pltpu.CompilerParams(dimension_semantics=("arbitrary",), vmem_limit_bytes=VMEM_LIMIT),
        name="in_proj",
    )(x2, *ws_list)


def _fox_gate_kernel(s_ref, bias_ref, c_ref):
    S = s_ref.shape[0]
    ff_t = s_ref[...].T[FF_COL:FF_COL + FOX_HEADS, :]
    log_f = jax.nn.log_sigmoid(ff_t + bias_ref[...])
    r = lax.broadcasted_iota(jnp.int32, (LANES, LANES), 0)
    c = lax.broadcasted_iota(jnp.int32, (LANES, LANES), 1)
    tri = (r <= c).astype(F32)
    carry = jnp.zeros((FOX_HEADS, 1), F32)
    for j in range(S // LANES):
        blk = log_f[:, j * LANES:(j + 1) * LANES]
        cs = _dot(blk, tri, precision=lax.Precision.HIGHEST) + carry
        c_ref[0, :, j * LANES:(j + 1) * LANES] = cs
        carry = cs[:, LANES - 1:LANES]


def _fox_gate(small, fox_f_bias, B, S):
    return pl.pallas_call(
        _fox_gate_kernel,
        grid=(B,),
        in_specs=[pl.BlockSpec((S, LANES), lambda b: (b, 0)),
                  pl.BlockSpec((FOX_HEADS, 1), lambda b: (0, 0))],
        out_specs=pl.BlockSpec((1, FOX_HEADS, S), lambda b: (b, 0, 0)),
        out_shape=jax.ShapeDtypeStruct((B, FOX_HEADS, S), F32),
        compiler_params=pltpu.CompilerParams(dimension_semantics=("arbitrary",)),
        name="fox_gate",
    )(small, fox_f_bias.reshape(FOX_HEADS, 1))


FOX_TQ = 256
FOX_TK = FOX_TQ
FOX_COLS = 256


def _fox_attn_kernel(q_ref, k_ref, v_ref, c_ref, o_ref):
    qi = pl.program_id(2)
    tq, tk, dh = FOX_TQ, FOX_TK, FOX_HEAD_DIM
    n_h = FOX_COLS // dh
    qs = [q_ref[:, hh * dh:(hh + 1) * dh] * (dh ** -0.5) for hh in range(n_h)]

    def step(j, carry, masked):
        ks = pl.ds(pl.multiple_of(j * tk, tk), tk)
        k2 = k_ref[ks, :]
        v2 = v_ref[ks, :]
        out = []
        for hh in range(n_h):
            m, l, acc = carry[hh]
            s = _dot_nt(qs[hh], k2[:, hh * dh:(hh + 1) * dh]) - c_ref[hh, :, ks]
            if masked:
                r = lax.broadcasted_iota(jnp.int32, (tq, tk), 0)
                c = lax.broadcasted_iota(jnp.int32, (tq, tk), 1)
                s = jnp.where(c <= r, s, -jnp.inf)
            m_new = jnp.maximum(m, jnp.max(s, axis=1, keepdims=True))
            p = jnp.exp(s - m_new)
            a = jnp.exp(m - m_new)
            l = a * l + jnp.sum(p, axis=1, keepdims=True)
            acc = a * acc + _dot(p.astype(BF16), v2[:, hh * dh:(hh + 1) * dh])
            out.append((m_new, l, acc))
        return tuple(out)

    init = tuple((jnp.full((tq, 1), -jnp.inf, F32), jnp.zeros((tq, 1), F32), jnp.zeros((tq, dh), F32))
                 for _ in range(n_h))
    carry = lax.fori_loop(0, qi, lambda j, c: step(j, c, False), init)
    carry = step(qi, carry, True)
    o_ref[...] = jnp.concatenate([acc / l for _, l, acc in carry], axis=1).astype(BF16)


def _fox_attn(fqkv, c_t, B, S):
    T = B * S
    nq = S // FOX_TQ
    n_hp = FOX_WIDTH // FOX_COLS
    return pl.pallas_call(
        _fox_attn_kernel,
        grid=(B, n_hp, nq),
        in_specs=[pl.BlockSpec((FOX_TQ, FOX_COLS), lambda b, h, i: (b * nq + i, h)),
                  pl.BlockSpec((S, FOX_COLS), lambda b, h, i: (b, n_hp + h)),
                  pl.BlockSpec((S, FOX_COLS), lambda b, h, i: (b, 2 * n_hp + h)),
                  pl.BlockSpec((FOX_COLS // FOX_HEAD_DIM, 1, S), lambda b, h, i: (b * n_hp + h, 0, 0))],
        out_specs=pl.BlockSpec((FOX_TQ, FOX_COLS), lambda b, h, i: (b * nq + i, h)),
        out_shape=jax.ShapeDtypeStruct((T, FOX_WIDTH), BF16),
        compiler_params=pltpu.CompilerParams(dimension_semantics=("arbitrary", "arbitrary", "arbitrary")),
        name="fox_attn",
    )(fqkv, fqkv, fqkv, c_t.reshape(B * FOX_HEADS, 1, S))


GLA_GROUP = 4


def _gla_kernel(q_ref, k_ref, v_ref, s_ref, up_ref, gb_ref, gg_ref, gain_ref, o_ref):
    S = q_ref.shape[0]
    C = GLA_CHUNK
    dk, dv = GLA_KEY_DIM, GLA_VAL_DIM
    r = lax.broadcasted_iota(jnp.int32, (C, C), 0)
    c = lax.broadcasted_iota(jnp.int32, (C, C), 1)
    tril = (r >= c).astype(F32)

    def body(ci, states):
        rows = pl.ds(pl.multiple_of(ci * C, C), C)
        z = _dot(s_ref[rows, :].astype(BF16), up_ref[...]) + gb_ref[...]
        la = jax.nn.log_sigmoid(z) * (1.0 / GLA_GATE_TEMP)
        cum = _dot(tril, la, precision=lax.Precision.HIGHEST)
        tot = cum[C - 1:C, :]
        kd_all = (k_ref[rows, :].astype(F32) * jnp.exp(tot - cum)).astype(BF16)
        decay = jnp.exp(tot)
        new_states = []
        for g, st_t in enumerate(states):
            kc, vc = slice(g * dk, (g + 1) * dk), slice(g * dv, (g + 1) * dv)
            st_t = st_t * decay[:, kc] + _dot_tn(v_ref[rows, vc], kd_all[:, kc])
            o = _dot_nt(q_ref[rows, kc], st_t.astype(BF16)) * (dk ** -0.5)
            o = o * lax.rsqrt(jnp.mean(o * o, axis=-1, keepdims=True) + GLA_NORM_EPS) * gain_ref[:, vc]
            o = o * jax.nn.silu(gg_ref[rows, vc])
            o_ref[rows, vc] = o.astype(BF16)
            new_states.append(st_t)
        return tuple(new_states)

    lax.fori_loop(0, S // C, body, tuple(jnp.zeros((dv, dk), F32) for _ in range(GLA_GROUP)))


def _gla(gqk, gv, small, gg, gla_gate_up, gla_gate_bias, gla_norm_gain, B, S):
    T = B * S
    up = jnp.zeros((LANES, GLA_QK_WIDTH), F32).at[GLR_COL:GLR_COL + GLA_GATE_RANK].set(gla_gate_up).astype(BF16)
    gb = gla_gate_bias.reshape(1, GLA_QK_WIDTH)
    gain = gla_norm_gain.reshape(1, GLA_V_WIDTH)
    n_groups = GLA_HEADS // GLA_GROUP
    kw, vw = GLA_GROUP * GLA_KEY_DIM, GLA_GROUP * GLA_VAL_DIM
    return pl.pallas_call(
        _gla_kernel,
        grid=(B, n_groups),
        in_specs=[pl.BlockSpec((S, kw), lambda b, h: (b, h)),
                  pl.BlockSpec((S, kw), lambda b, h: (b, n_groups + h)),
                  pl.BlockSpec((S, vw), lambda b, h: (b, h)),
                  pl.BlockSpec((S, LANES), lambda b, h: (b, 0)),
                  pl.BlockSpec((LANES, kw), lambda b, h: (0, h)),
                  pl.BlockSpec((1, kw), lambda b, h: (0, h)),
                  pl.BlockSpec((S, vw), lambda b, h: (b, h)),
                  pl.BlockSpec((1, vw), lambda b, h: (0, h))],
        out_specs=pl.BlockSpec((S, vw), lambda b, h: (b, h)),
        out_shape=jax.ShapeDtypeStruct((T, GLA_V_WIDTH), BF16),
        compiler_params=pltpu.CompilerParams(dimension_semantics=("arbitrary", "arbitrary")),
        name="gla",
    )(gqk, gqk, gv, small, up, gb, gg, gain)


def _merge_kernel(a_ref, og_ref, m_ref, x_ref, wf_ref, wg_ref, wo_ref, g_ref, b_ref, o_ref):
    y_fox = _dot(a_ref[...], wf_ref[...])
    y_gla = _dot(og_ref[...], wg_ref[...])
    merged = (jax.nn.sigmoid(m_ref[:, :D_MODEL]) * y_fox + jax.nn.sigmoid(m_ref[:, D_MODEL:]) * y_gla)
    mix = _dot(merged.astype(BF16), wo_ref[...])
    o_ref[...] = _layer_norm(ALPHA * x_ref[...] + mix, g_ref[...], b_ref[...])


def _merge(attn, og, mg, x2, w_out_fox, w_out_gla, w_out, ln_gain, ln_bias):
    T = x2.shape[0]
    tm = 512
    row = lambda w: pl.BlockSpec((tm, w), lambda i: (i, 0))
    const = lambda a: pl.BlockSpec(a.shape, lambda i: (0, 0))
    ws = [w_out_fox.astype(BF16), w_out_gla.astype(BF16), w_out.astype(BF16),
          ln_gain.reshape(1, D_MODEL), ln_bias.reshape(1, D_MODEL)]
    return pl.pallas_call(
        _merge_kernel,
        grid=(T // tm,),
        in_specs=[row(FOX_WIDTH), row(GLA_V_WIDTH), row(2 * D_MODEL), row(D_MODEL)] + [const(w) for w in ws],
        out_specs=row(D_MODEL),
        out_shape=jax.ShapeDtypeStruct((T, D_MODEL), F32),
        compiler_params=pltpu.CompilerParams(dimension_semantics=("arbitrary",), vmem_limit_bytes=VMEM_LIMIT),
        name="merge",
    )(attn, og, mg, x2, *ws)


ROUTE_TM = 256
CAND_BLOCKS = 10


def _sublane_all(x, op):
    for shift in (4, 2, 1):
        x = op(x, pltpu.roll(x, shift, axis=0))
    return x


def _take_max(s3, iota3, sentinel):
    m8 = _sublane_all(jnp.max(s3, axis=0), jnp.maximum)
    idx8 = _sublane_all(jnp.min(jnp.where(s3 == m8[None], iota3, sentinel), axis=0), jnp.minimum)
    return m8, idx8, iota3 == idx8[None]


def _row_iota(groups, n):
    shape = (groups, SUBLANES, n)
    return lax.broadcasted_iota(jnp.int32, shape, 0) * SUBLANES + lax.broadcasted_iota(jnp.int32, shape, 1)


def _route_kernel(x_ref, wq_ref, keys_ref, after_a_ref, after_b_ref, idx_ref, gate_ref,
                  q_scr, st_scr, it_scr, best_scr, pick_scr, gsel_scr):
    del after_a_ref, after_b_ref
    tm = ROUTE_TM
    K = PEER_TOPK
    q_scr[...] = _dot(x_ref[...].astype(BF16), wq_ref[...])
    key_iota = _row_iota(PEER_N_KEYS // SUBLANES, tm)

    def stage1(pair, _):
        hps = (2 * pair, 2 * pair + 1)
        ss = []
        for hp in hps:
            q = q_scr[:, pl.ds(pl.multiple_of(hp * PEER_HALF, PEER_HALF), PEER_HALF)].astype(BF16)
            ss.append(_dot_nt(keys_ref[hp], q).reshape(PEER_N_KEYS // SUBLANES, SUBLANES, tm))
        for i in range(K):
            for n, hp in enumerate(hps):
                m8, idx8, hit = _take_max(ss[n], key_iota, PEER_N_KEYS)
                st_scr[hp, i:i + 1, :] = m8[0:1]
                it_scr[hp, i:i + 1, :] = idx8[0:1]
                ss[n] = jnp.where(hit, -jnp.inf, ss[n])
        return 0

    lax.fori_loop(0, PEER_HEADS, stage1, 0)

    cand_iota = _row_iota(CAND_BLOCKS, tm)

    def stage2(h, _):
        s0, s1 = st_scr[2 * h], st_scr[2 * h + 1]
        i0, i1 = it_scr[2 * h], it_scr[2 * h + 1]
        lo, hi = slice(0, SUBLANES), slice(SUBLANES, 2 * SUBLANES)
        cs = [s0[0:1] + s1[lo], s0[0:1] + s1[hi]]
        ci = [i0[0:1] * PEER_N_KEYS + i1[lo], i0[0:1] * PEER_N_KEYS + i1[hi]]
        for a in range(1, SUBLANES):
            cs.append(s0[a:a + 1] + s1[lo])
            ci.append(i0[a:a + 1] * PEER_N_KEYS + i1[lo])
        cs.append(s0[hi] + s1[0:1])
        ci.append(i0[hi] * PEER_N_KEYS + i1[0:1])
        cand = jnp.stack(cs, axis=0)
        cidx = jnp.stack(ci, axis=0)
        for i in range(K):
            m8, _, hit = _take_max(cand, cand_iota, CAND_BLOCKS * SUBLANES)
            pick8 = _sublane_all(jnp.max(jnp.where(hit, cidx, -1), axis=0), jnp.maximum)
            best_scr[i:i + 1, :] = m8[0:1]
            pick_scr[pl.ds(h * K + i, 1), :] = pick8[0:1]
            cand = jnp.where(hit, -jnp.inf, cand)
        best = best_scr[...]
        e = jnp.exp(best - best[0:1])
        gsel_scr[pl.ds(pl.multiple_of(h * K, K), K), :] = e / jnp.sum(e, axis=0, keepdims=True)
        return 0

    lax.fori_loop(0, PEER_HEADS, stage2, 0)
    idx_ref[...] = pick_scr[...].T
    gate_ref[...] = gsel_scr[...].T


def _route(x1, peer_w_query, peer_sub_keys, after, after_b):
    T = x1.shape[0]
    tm = ROUTE_TM
    wq = peer_w_query.reshape(D_MODEL, 2 * PEER_HEADS * PEER_HALF).astype(BF16)
    keys = peer_sub_keys.reshape(2 * PEER_HEADS, PEER_N_KEYS, PEER_HALF).astype(BF16)
    picks = pl.BlockSpec((tm, PEER_PICKS), lambda i: (i, 0))
    return pl.pallas_call(
        _route_kernel,
        grid=(T // tm,),
        in_specs=[pl.BlockSpec((tm, D_MODEL), lambda i: (i, 0)),
                  pl.BlockSpec(wq.shape, lambda i: (0, 0)),
                  pl.BlockSpec(keys.shape, lambda i: (0, 0, 0)),
                  pl.BlockSpec((SUBLANES, LANES), lambda i: (0, 0)),
                  pl.BlockSpec((SUBLANES, LANES), lambda i: (0, 0))],
        out_specs=[picks, picks],
        out_shape=[jax.ShapeDtypeStruct((T, PEER_PICKS), jnp.int32),
                   jax.ShapeDtypeStruct((T, PEER_PICKS), F32)],
        scratch_shapes=[pltpu.VMEM((tm, 2 * PEER_HEADS * PEER_HALF), F32),
                        pltpu.VMEM((2 * PEER_HEADS, PEER_TOPK, tm), F32),
                        pltpu.VMEM((2 * PEER_HEADS, PEER_TOPK, tm), jnp.int32),
                        pltpu.VMEM((PEER_TOPK, tm), F32),
                        pltpu.VMEM((PEER_PICKS, tm), jnp.int32),
                        pltpu.VMEM((PEER_PICKS, tm), F32)],
        compiler_params=pltpu.CompilerParams(dimension_semantics=("arbitrary",), vmem_limit_bytes=VMEM_LIMIT),
        name="peer_route",
    )(x1, wq, keys, after, after_b)


SC_LANES = 16
SC_ROWS = 64
SC_CHUNKS = PEER_PICKS // SC_ROWS
SC_DCOLS = 256
SC_DVREGS = SC_DCOLS // SC_LANES


def _sc_mesh():
    info = plsc.get_sparse_core_info()
    mesh = plsc.VectorSubcoreMesh(core_axis_name="c", subcore_axis_name="s")
    return mesh, info.num_cores, info.num_cores * info.num_subcores


def _sc_token_pipeline(tab_hbm, idx_hbm, vec_hbm, out_hbm, idx_v, vec_v, rows_v, out_v,
                       sem_rows, sem_tok, sem_out, n_cores, tpw, compute_chunk):
    wid = lax.axis_index("s") * n_cores + lax.axis_index("c")
    base = wid * tpw

    def gather(s, c):
        return pltpu.make_async_copy(tab_hbm.at[idx_v.at[s, pl.ds(c * SC_ROWS, SC_ROWS)]],
                                     rows_v.at[c % 2], sem_rows.at[c % 2])

    def tok_fetch(s, tok):
        return (pltpu.make_async_copy(idx_hbm.at[tok], idx_v.at[s], sem_tok.at[0]),
                pltpu.make_async_copy(vec_hbm.at[tok], vec_v.at[s], sem_tok.at[1]))

    def out_copy(s, tok):
        return pltpu.make_async_copy(out_v.at[s], out_hbm.at[tok], sem_out.at[s])

    for d in tok_fetch(0, base):
        d.start()
    for d in tok_fetch(0, base):
        d.wait()
    gather(0, 0).start()

    def pair_body(tp, _):
        for s in range(2):
            t = tp * 2 + s
            tok = base + t
            has_next = t + 1 < tpw

            @pl.when(has_next)
            def _():
                for d in tok_fetch(1 - s, tok + 1):
                    d.start()

            @pl.when(t >= 2)
            def _():
                out_copy(s, tok - 2).wait()

            for c in range(SC_CHUNKS):
                if c + 1 < SC_CHUNKS:
                    gather(s, c + 1).start()
                else:
                    @pl.when(has_next)
                    def _():
                        for d in tok_fetch(1 - s, tok + 1):
                            d.wait()
                        gather(1 - s, 0).start()
                gather(s, c).wait()
                compute_chunk(s, c)
            out_copy(s, tok).start()
        return 0

    lax.fori_loop(0, tpw // 2, pair_body, 0)
    for s in range(2):
        out_copy(s, base + tpw - 2 + s).wait()


def _sc_scratch(vec_len, out_len):
    return [pltpu.VMEM((2, PEER_PICKS), jnp.int32),
            pltpu.VMEM((2, vec_len), F32),
            pltpu.VMEM((2, SC_ROWS, D_MODEL // 2), jnp.int32),
            pltpu.VMEM((2, out_len), F32),
            pltpu.SemaphoreType.DMA((2,)),
            pltpu.SemaphoreType.DMA((2,)),
            pltpu.SemaphoreType.DMA((2,))]


def _pack_table(tab):
    n, d = tab.shape
    bits = lax.bitcast_convert_type(tab.astype(BF16), jnp.uint16).astype(jnp.uint32).reshape(n, d // 32, 2, SC_LANES)
    words = bits[:, :, 0, :] | (bits[:, :, 1, :] << 16)
    return lax.bitcast_convert_type(words, jnp.int32).reshape(n, d // 2)


def _unpack_pairs(x):
    return list(plsc.unpack(x, format=plsc.PackFormat.INTERLEAVED))


def _sc_acc(v_packed, idx, w):
    T = w.shape[0]
    mesh, n_cores, n_workers = _sc_mesh()
    tpw = T // n_workers
    L = SC_LANES

    @functools.partial(
        pl.kernel, mesh=mesh, out_type=jax.ShapeDtypeStruct((T, D_MODEL), F32),
        scratch_types=_sc_scratch(PEER_PICKS, D_MODEL),
        compiler_params=pltpu.CompilerParams(needs_layout_passes=False), name="peer_sc_acc")
    def k(v_hbm, idx_hbm, w_hbm, out_hbm, idx_v, w_v, rows_v, out_v, sem_rows, sem_tok, sem_out):
        def compute_chunk(s, c):
            b = c % 2
            for dc in range(D_MODEL // SC_DCOLS):
                cols = [pl.ds(dc * SC_DCOLS + j * L, L) for j in range(SC_DVREGS)]
                if c == 0:
                    accs = tuple(jnp.zeros((L,), F32) for _ in cols)
                else:
                    accs = tuple(out_v[s, cs] for cs in cols)

                def row_body(r, accs, dc=dc):
                    wb = plsc.load_gather(w_v.at[s], [jnp.full((L,), c * SC_ROWS, jnp.int32) + r])
                    new = []
                    for j in range(SC_DVREGS // 2):
                        words = rows_v[b, r, pl.ds(dc * (SC_DCOLS // 2) + j * L, L)]
                        lo, hi = _unpack_pairs(plsc.bitcast(words, BF16))
                        new += [accs[2 * j] + wb * lo, accs[2 * j + 1] + wb * hi]
                    return tuple(new)

                accs = lax.fori_loop(0, SC_ROWS, row_body, accs)
                for a, cs in zip(accs, cols):
                    out_v[s, cs] = a

        _sc_token_pipeline(v_hbm, idx_hbm, w_hbm, out_hbm, idx_v, w_v, rows_v, out_v,
                           sem_rows, sem_tok, sem_out, n_cores, tpw, compute_chunk)

    return k(v_packed, idx, w)


DENSE_TM = 2048
DENSE_TN = 1024


def _dense_act_kernel(x_ref, u_ref, o_ref):
    o_ref[...] = _dot_nt(x_ref[...].astype(BF16), u_ref[...])


def _dense_act(x1, u_bf16):
    T, N = x1.shape[0], u_bf16.shape[0]
    return pl.pallas_call(
        _dense_act_kernel, grid=(T // DENSE_TM, N // DENSE_TN),
        in_specs=[pl.BlockSpec((DENSE_TM, D_MODEL), lambda i, j: (i, 0)),
                  pl.BlockSpec((DENSE_TN, D_MODEL), lambda i, j: (j, 0))],
        out_specs=pl.BlockSpec((DENSE_TM, DENSE_TN), lambda i, j: (i, j)),
        out_shape=jax.ShapeDtypeStruct((T, N), F32),
        compiler_params=pltpu.CompilerParams(dimension_semantics=("arbitrary", "arbitrary"),
                                             vmem_limit_bytes=VMEM_LIMIT),
        name="peer_dense_act",
    )(x1, u_bf16)


PICK_TG = SUBLANES
PICK_NQ = 4096


def _sc_pick(dense, idx):
    T, N = dense.shape
    mesh, n_cores, n_workers = _sc_mesh()
    gpw = T // PICK_TG // n_workers
    n_q = N // PICK_NQ
    L = SC_LANES

    @functools.partial(
        pl.kernel, mesh=mesh, out_type=jax.ShapeDtypeStruct((T, PEER_PICKS), F32),
        scratch_types=[pltpu.VMEM((2, PICK_TG, PEER_PICKS), jnp.int32), pltpu.VMEM((2, PICK_TG, PICK_NQ), F32),
                       pltpu.VMEM((2, PICK_TG, PEER_PICKS), F32),
                       pltpu.SemaphoreType.DMA((2,)), pltpu.SemaphoreType.DMA((2,)), pltpu.SemaphoreType.DMA((2,))],
        compiler_params=pltpu.CompilerParams(needs_layout_passes=False), name="peer_sc_pick")
    def k(d_hbm, idx_hbm, act_hbm, idx_v, buf, act_v, sem_idx, sem_buf, sem_out):
        g0 = (lax.axis_index("s") * n_cores + lax.axis_index("c")) * gpw

        def idx_fetch(gs, g):
            return pltpu.make_async_copy(idx_hbm.at[pl.ds(g * PICK_TG, PICK_TG)], idx_v.at[gs], sem_idx.at[gs])

        def blk_fetch(g, q):
            return pltpu.make_async_copy(d_hbm.at[pl.ds(g * PICK_TG, PICK_TG), pl.ds(q * PICK_NQ, PICK_NQ)],
                                         buf.at[q % 2], sem_buf.at[q % 2])

        def out_copy(gs, g):
            return pltpu.make_async_copy(act_v.at[gs], act_hbm.at[pl.ds(g * PICK_TG, PICK_TG)], sem_out.at[gs])

        idx_fetch(0, g0).start()
        blk_fetch(g0, 0).start()

        def pair_body(gp, _):
            for gs in range(2):
                gi = gp * 2 + gs
                g = g0 + gi
                has_next = gi + 1 < gpw

                @pl.when(has_next)
                def _():
                    idx_fetch(1 - gs, g + 1).start()

                idx_fetch(gs, g).wait()

                @pl.when(gi >= 2)
                def _():
                    out_copy(gs, g - 2).wait()

                for q in range(n_q):
                    if q + 1 < n_q:
                        blk_fetch(g, q + 1).start()
                    else:
                        @pl.when(has_next)
                        def _():
                            blk_fetch(g + 1, 0).start()
                    blk_fetch(g, q).wait()
                    for t in range(PICK_TG):
                        for v in range(PEER_PICKS // L):
                            picks = pl.ds(v * L, L)
                            local = idx_v[gs, t, picks] - q * PICK_NQ
                            inside = (local >= 0) & (local < PICK_NQ)
                            val = plsc.load_gather(buf, [jnp.full((L,), q % 2, jnp.int32), jnp.full((L,), t, jnp.int32),
                                                         jnp.where(inside, local, 0)])
                            act_v[gs, t, picks] = val if q == 0 else jnp.where(inside, val, act_v[gs, t, picks])
                out_copy(gs, g).start()
            return 0

        lax.fori_loop(0, gpw // 2, pair_body, 0)
        for gs in range(2):
            out_copy(gs, g0 + gpw - 2 + gs).wait()

    return k(dense, idx)


def _gelu_gate_kernel(a_ref, g_ref, o_ref):
    a = a_ref[...]
    o_ref[...] = 0.5 * a * (1.0 + lax.erf(a * (2.0 ** -0.5))) * g_ref[...]


def _ln2_kernel(h_ref, f_ref, g_ref, b_ref, full_ref, o_ref):
    del full_ref
    o_ref[...] = _layer_norm(ALPHA * h_ref[...] + f_ref[...], g_ref[...], b_ref[...])


MIX_TM = 512
_MIX_PARAMS = pltpu.CompilerParams(dimension_semantics=("arbitrary",))


def _mix_row(width):
    return pl.BlockSpec((MIX_TM, width), lambda i: (i, 0))


def _gelu_gate_call(act, gates):
    T = act.shape[0]
    return pl.pallas_call(
        _gelu_gate_kernel, grid=(T // MIX_TM,), in_specs=[_mix_row(PEER_PICKS)] * 2, out_specs=_mix_row(PEER_PICKS),
        out_shape=jax.ShapeDtypeStruct((T, PEER_PICKS), F32), compiler_params=_MIX_PARAMS, name="peer_gelu_gate",
    )(act, gates)


def _mix_finish(x1, idx, w, v_packed, ln_gain, ln_bias, result, row0):
    T = x1.shape[0]
    const = pl.BlockSpec((1, D_MODEL), lambda i: (0, 0))
    blk0 = row0 // MIX_TM
    ffn = _sc_acc(v_packed, idx, w)
    result = pl.pallas_call(
        _ln2_kernel, grid=(T // MIX_TM,),
        in_specs=[_mix_row(D_MODEL)] * 2 + [const, const, pl.BlockSpec(memory_space=pl.ANY)],
        out_specs=pl.BlockSpec((MIX_TM, D_MODEL), lambda i: (blk0 + i, 0)),
        out_shape=jax.ShapeDtypeStruct(result.shape, F32), input_output_aliases={4: 0},
        compiler_params=_MIX_PARAMS, name="peer_ln2",
    )(x1, ffn, ln_gain.reshape(1, D_MODEL), ln_bias.reshape(1, D_MODEL), result)
    return result, ffn


BATCH_CHUNKS = 16
SC_GATE_LAG = 1
SC_PASS_LAG = 3


def kernel(x, w_in, fox_f_bias, gla_gate_up, gla_gate_bias, gla_norm_gain, w_out_fox, w_out_gla, w_out,
           ln1_gain, ln1_bias, peer_w_query, peer_sub_keys, peer_expert_u, peer_expert_v, ln2_gain, ln2_bias):
    B, S, D = x.shape
    assert D == D_MODEL and S % max(ROUTE_TM, FOX_TQ) == 0
    assert DEPTH == 1 and w_in.shape[0] == 1, "the chunk pipeline below is written for the single-layer block"
    u_bf16, v_packed = peer_expert_u[0].astype(BF16), _pack_table(peer_expert_v[0])
    n_chunks = BATCH_CHUNKS if B % BATCH_CHUNKS == 0 else 1
    bc = B // n_chunks
    ffns, ws = [], []
    result = jnp.zeros((B * S, D), F32)
    for ci in range(n_chunks):
        h = x[ci * bc:(ci + 1) * bc].reshape(bc * S, D)
        fqkv, gqk, gv, gg, mg, small = _in_proj(h, w_in[0])
        c_t = _fox_gate(small, fox_f_bias[0], bc, S)
        attn = _fox_attn(fqkv, c_t, bc, S)
        og = _gla(gqk, gv, small, gg, gla_gate_up[0], gla_gate_bias[0], gla_norm_gain[0], bc, S)
        x1 = _merge(attn, og, mg, h, w_out_fox[0], w_out_gla[0], w_out[0], ln1_gain[0], ln1_bias[0])
        zero = jnp.zeros((SUBLANES, LANES), F32)
        after = ws[ci - SC_GATE_LAG][:SUBLANES, :LANES] if ci >= SC_GATE_LAG else zero
        after_b = ffns[ci - SC_PASS_LAG][:SUBLANES, :LANES] if ci >= SC_PASS_LAG else zero
        idx, gates = _route(x1, peer_w_query[0], peer_sub_keys[0], after, after_b)
        act = _sc_pick(_dense_act(x1, u_bf16), idx)
        w = _gelu_gate_call(act, gates)
        result, ffn = _mix_finish(x1, idx, w, v_packed, ln2_gain[0], ln2_bias[0], result, ci * bc * S)
        ffns.append(ffn)
        ws.append(w)
    return result.reshape(B, S, D)
```

```python
import functools
import math

import jax
import jax.numpy as jnp
from jax import lax
from jax.experimental import pallas as pl
from jax.experimental.pallas import tpu as pltpu
from jax.experimental.pallas import tpu_sc as plsc

F32 = jnp.float32
BF16 = jnp.bfloat16

D_MODEL = 1024
FOX_HEADS = 8
FOX_HEAD_DIM = 64
FOX_WIDTH = FOX_HEADS * FOX_HEAD_DIM
GLA_HEADS = 4
GLA_KEY_DIM = 128
GLA_VAL_DIM = 256
GLA_QK_WIDTH = GLA_HEADS * GLA_KEY_DIM
GLA_V_WIDTH = GLA_HEADS * GLA_VAL_DIM
GLA_GATE_RANK = 16
GLA_GATE_TEMP = 16.0
GLA_NORM_EPS = 1e-5
GLA_CHUNK = 64
PEER_HEADS = 8
PEER_N_KEYS = 128
PEER_HALF = 128
PEER_TOPK = 16
PEER_PICKS = PEER_HEADS * PEER_TOPK
DEPTH = 1
ALPHA = (2.0 * DEPTH) ** 0.25
LN_EPS = 1e-5

LANES = 128
SUBLANES = 8
VMEM_LIMIT = 52 * 1024 * 1024

IN_SPLIT_SIZES = (FOX_WIDTH, FOX_WIDTH, FOX_WIDTH, FOX_HEADS,
                  GLA_QK_WIDTH, GLA_QK_WIDTH, GLA_V_WIDTH, GLA_V_WIDTH, GLA_GATE_RANK,
                  D_MODEL, D_MODEL)
FF_COL = 0
GLR_COL = FOX_HEADS


def _dot(a, b, **kw):
    return jnp.dot(a, b, preferred_element_type=F32, **kw)


def _dot_nt(a, b):
    return lax.dot_general(a, b, (((1,), (1,)), ((), ())), preferred_element_type=F32)


def _dot_tn(a, b):
    return lax.dot_general(a, b, (((0,), (0,)), ((), ())), preferred_element_type=F32)


def _layer_norm(y, gain, bias):
    mu = jnp.mean(y, axis=-1, keepdims=True)
    yc = y - mu
    var = jnp.mean(yc * yc, axis=-1, keepdims=True)
    return yc * lax.rsqrt(var + LN_EPS) * gain + bias


def _in_proj_kernel(x_ref, wf_ref, wgqk_ref, wgv_ref, wgg_ref, wm_ref, ws_ref,
                    f_ref, gqk_ref, gv_ref, gg_ref, m_ref, s_ref):
    xb = x_ref[...].astype(BF16)
    f_ref[...] = _dot(xb, wf_ref[...]).astype(BF16)
    gqk_ref[...] = _dot(xb, wgqk_ref[...]).astype(BF16)
    gv_ref[...] = _dot(xb, wgv_ref[...]).astype(BF16)
    gg_ref[...] = _dot(xb, wgg_ref[...])
    m_ref[...] = _dot(xb, wm_ref[...])
    s_ref[...] = _dot(xb, ws_ref[...])


def _in_proj(x2, w_in):
    T = x2.shape[0]
    tm = 256
    pts = [0]
    for s in IN_SPLIT_SIZES:
        pts.append(pts[-1] + s)
    col = lambda i, j: w_in[:, pts[i]:pts[j]]
    wf = col(0, 3).astype(BF16)
    wgqk = col(4, 6).astype(BF16)
    wgv = col(6, 7).astype(BF16)
    wgg = col(7, 8).astype(BF16)
    wm = col(9, 11).astype(BF16)
    ws = jnp.concatenate([col(3, 4), col(8, 9)], axis=1)
    ws = jnp.pad(ws, ((0, 0), (0, LANES - ws.shape[1]))).astype(BF16)
    ws_list = [wf, wgqk, wgv, wgg, wm, ws]
    out_dtypes = [BF16, BF16, BF16, F32, F32, F32]
    const = lambda w: pl.BlockSpec(w.shape, lambda i: (0, 0))
    return pl.pallas_call(
        _in_proj_kernel,
        grid=(T // tm,),
        in_specs=[pl.BlockSpec((tm, D_MODEL), lambda i: (i, 0))] + [const(w) for w in ws_list],
        out_specs=[pl.BlockSpec((tm, w.shape[1]), lambda i: (i, 0)) for w in ws_list],
        out_shape=[jax.ShapeDtypeStruct((T, w.shape[1]), dt) for w, dt in zip(ws_list, out_dtypes)],
        compiler_params=pltpu.CompilerParams(dimension_semantics=("arbitrary",), vmem_limit_bytes=VMEM_LIMIT),
        name="in_proj",
    )(x2, *ws_list)


def _fox_gate_kernel(s_ref, bias_ref, c_ref):
    S = s_ref.shape[0]
    ff_t = s_ref[...].T[FF_COL:FF_COL + FOX_HEADS, :]
    log_f = jax.nn.log_sigmoid(ff_t + bias_ref[...])
    r = lax.broadcasted_iota(jnp.int32, (LANES, LANES), 0)
    c = lax.broadcasted_iota(jnp.int32, (LANES, LANES), 1)
    tri = (r <= c).astype(F32)
    carry = jnp.zeros((FOX_HEADS, 1), F32)
    for j in range(S // LANES):
        blk = log_f[:, j * LANES:(j + 1) * LANES]
        cs = _dot(blk, tri, precision=lax.Precision.HIGHEST) + carry
        c_ref[0, :, j * LANES:(j + 1) * LANES] = cs
        carry = cs[:, LANES - 1:LANES]


def _fox_gate(small, fox_f_bias, B, S):
    return pl.pallas_call(
        _fox_gate_kernel,
        grid=(B,),
        in_specs=[pl.BlockSpec((S, LANES), lambda b: (b, 0)),
                  pl.BlockSpec((FOX_HEADS, 1), lambda b: (0, 0))],
        out_specs=pl.BlockSpec((1, FOX_HEADS, S), lambda b: (b, 0, 0)),
        out_shape=jax.ShapeDtypeStruct((B, FOX_HEADS, S), F32),
        compiler_params=pltpu.CompilerParams(dimension_semantics=("arbitrary",)),
        name="fox_gate",
    )(small, fox_f_bias.reshape(FOX_HEADS, 1))


FOX_TQ = 256
FOX_TK = FOX_TQ
FOX_COLS = 256


def _fox_attn_kernel(q_ref, k_ref, v_ref, c_ref, o_ref):
    qi = pl.program_id(2)
    tq, tk, dh = FOX_TQ, FOX_TK, FOX_HEAD_DIM
    n_h = FOX_COLS // dh
    qs = [q_ref[:, hh * dh:(hh + 1) * dh] * (dh ** -0.5) for hh in range(n_h)]

    def step(j, carry, masked):
        ks = pl.ds(pl.multiple_of(j * tk, tk), tk)
        k2 = k_ref[ks, :]
        v2 = v_ref[ks, :]
        out = []
        for hh in range(n_h):
            m, l, acc = carry[hh]
            s = _dot_nt(qs[hh], k2[:, hh * dh:(hh + 1) * dh]) - c_ref[hh, :, ks]
            if masked:
                r = lax.broadcasted_iota(jnp.int32, (tq, tk), 0)
                c = lax.broadcasted_iota(jnp.int32, (tq, tk), 1)
                s = jnp.where(c <= r, s, -jnp.inf)
            m_new = jnp.maximum(m, jnp.max(s, axis=1, keepdims=True))
            p = jnp.exp(s - m_new)
            a = jnp.exp(m - m_new)
            l = a * l + jnp.sum(p, axis=1, keepdims=True)
            acc = a * acc + _dot(p.astype(BF16), v2[:, hh * dh:(hh + 1) * dh])
            out.append((m_new, l, acc))
        return tuple(out)

    init = tuple((jnp.full((tq, 1), -jnp.inf, F32), jnp.zeros((tq, 1), F32), jnp.zeros((tq, dh), F32))
                 for _ in range(n_h))
    carry = lax.fori_loop(0, qi, lambda j, c: step(j, c, False), init)
    carry = step(qi, carry, True)
    o_ref[...] = jnp.concatenate([acc / l for _, l, acc in carry], axis=1).astype(BF16)


def _fox_attn(fqkv, c_t, B, S):
    T = B * S
    nq = S // FOX_TQ
    n_hp = FOX_WIDTH // FOX_COLS
    return pl.pallas_call(
        _fox_attn_kernel,
        grid=(B, n_hp, nq),
        in_specs=[pl.BlockSpec((FOX_TQ, FOX_COLS), lambda b, h, i: (b * nq + i, h)),
                  pl.BlockSpec((S, FOX_COLS), lambda b, h, i: (b, n_hp + h)),
                  pl.BlockSpec((S, FOX_COLS), lambda b, h, i: (b, 2 * n_hp + h)),
                  pl.BlockSpec((FOX_COLS // FOX_HEAD_DIM, 1, S), lambda b, h, i: (b * n_hp + h, 0, 0))],
        out_specs=pl.BlockSpec((FOX_TQ, FOX_COLS), lambda b, h, i: (b * nq + i, h)),
        out_shape=jax.ShapeDtypeStruct((T, FOX_WIDTH), BF16),
        compiler_params=pltpu.CompilerParams(dimension_semantics=("arbitrary", "arbitrary", "arbitrary")),
        name="fox_attn",
    )(fqkv, fqkv, fqkv, c_t.reshape(B * FOX_HEADS, 1, S))


GLA_GROUP = 4


def _gla_kernel(q_ref, k_ref, v_ref, s_ref, up_ref, gb_ref, gg_ref, gain_ref, o_ref):
    S = q_ref.shape[0]
    C = GLA_CHUNK
    dk, dv = GLA_KEY_DIM, GLA_VAL_DIM
    r = lax.broadcasted_iota(jnp.int32, (C, C), 0)
    c = lax.broadcasted_iota(jnp.int32, (C, C), 1)
    tril = (r >= c).astype(F32)

    def body(ci, states):
        rows = pl.ds(pl.multiple_of(ci * C, C), C)
        z = _dot(s_ref[rows, :].astype(BF16), up_ref[...]) + gb_ref[...]
        la = jax.nn.log_sigmoid(z) * (1.0 / GLA_GATE_TEMP)
        cum = _dot(tril, la, precision=lax.Precision.HIGHEST)
        tot = cum[C - 1:C, :]
        kd_all = (k_ref[rows, :].astype(F32) * jnp.exp(tot - cum)).astype(BF16)
        decay = jnp.exp(tot)
        new_states = []
        for g, st_t in enumerate(states):
            kc, vc = slice(g * dk, (g + 1) * dk), slice(g * dv, (g + 1) * dv)
            st_t = st_t * decay[:, kc] + _dot_tn(v_ref[rows, vc], kd_all[:, kc])
            o = _dot_nt(q_ref[rows, kc], st_t.astype(BF16)) * (dk ** -0.5)
            o = o * lax.rsqrt(jnp.mean(o * o, axis=-1, keepdims=True) + GLA_NORM_EPS) * gain_ref[:, vc]
            o = o * jax.nn.silu(gg_ref[rows, vc])
            o_ref[rows, vc] = o.astype(BF16)
            new_states.append(st_t)
        return tuple(new_states)

    lax.fori_loop(0, S // C, body, tuple(jnp.zeros((dv, dk), F32) for _ in range(GLA_GROUP)), unroll=4)


def _gla(gqk, gv, small, gg, gla_gate_up, gla_gate_bias, gla_norm_gain, B, S):
    T = B * S
    up = jnp.zeros((LANES, GLA_QK_WIDTH), F32).at[GLR_COL:GLR_COL + GLA_GATE_RANK].set(gla_gate_up).astype(BF16)
    gb = gla_gate_bias.reshape(1, GLA_QK_WIDTH)
    gain = gla_norm_gain.reshape(1, GLA_V_WIDTH)
    n_groups = GLA_HEADS // GLA_GROUP
    kw, vw = GLA_GROUP * GLA_KEY_DIM, GLA_GROUP * GLA_VAL_DIM
    return pl.pallas_call(
        _gla_kernel,
        grid=(B, n_groups),
        in_specs=[pl.BlockSpec((S, kw), lambda b, h: (b, h)),
                  pl.BlockSpec((S, kw), lambda b, h: (b, n_groups + h)),
                  pl.BlockSpec((S, vw), lambda b, h: (b, h)),
                  pl.BlockSpec((S, LANES), lambda b, h: (b, 0)),
                  pl.BlockSpec((LANES, kw), lambda b, h: (0, h)),
                  pl.BlockSpec((1, kw), lambda b, h: (0, h)),
                  pl.BlockSpec((S, vw), lambda b, h: (b, h)),
                  pl.BlockSpec((1, vw), lambda b, h: (0, h))],
        out_specs=pl.BlockSpec((S, vw), lambda b, h: (b, h)),
        out_shape=jax.ShapeDtypeStruct((T, GLA_V_WIDTH), BF16),
        compiler_params=pltpu.CompilerParams(dimension_semantics=("arbitrary", "arbitrary")),
        name="gla",
    )(gqk, gqk, gv, small, up, gb, gg, gain)


def _merge_kernel(a_ref, og_ref, m_ref, x_ref, wf_ref, wg_ref, wo_ref, g_ref, b_ref, o_ref):
    y_fox = _dot(a_ref[...], wf_ref[...])
    y_gla = _dot(og_ref[...], wg_ref[...])
    merged = (jax.nn.sigmoid(m_ref[:, :D_MODEL]) * y_fox + jax.nn.sigmoid(m_ref[:, D_MODEL:]) * y_gla)
    mix = _dot(merged.astype(BF16), wo_ref[...])
    o_ref[...] = _layer_norm(ALPHA * x_ref[...] + mix, g_ref[...], b_ref[...])


def _merge(attn, og, mg, x2, w_out_fox, w_out_gla, w_out, ln_gain, ln_bias):
    T = x2.shape[0]
    tm = 512
    row = lambda w: pl.BlockSpec((tm, w), lambda i: (i, 0))
    const = lambda a: pl.BlockSpec(a.shape, lambda i: (0, 0))
    ws = [w_out_fox.astype(BF16), w_out_gla.astype(BF16), w_out.astype(BF16),
          ln_gain.reshape(1, D_MODEL), ln_bias.reshape(1, D_MODEL)]
    return pl.pallas_call(
        _merge_kernel,
        grid=(T // tm,),
        in_specs=[row(FOX_WIDTH), row(GLA_V_WIDTH), row(2 * D_MODEL), row(D_MODEL)] + [const(w) for w in ws],
        out_specs=row(D_MODEL),
        out_shape=jax.ShapeDtypeStruct((T, D_MODEL), F32),
        compiler_params=pltpu.CompilerParams(dimension_semantics=("arbitrary",), vmem_limit_bytes=VMEM_LIMIT),
        name="merge",
    )(attn, og, mg, x2, *ws)


ROUTE_TM = 256
CAND_BLOCKS = 10


def _sublane_all(x, op):
    for shift in (4, 2, 1):
        x = op(x, pltpu.roll(x, shift, axis=0))
    return x


def _take_max(s3, iota3, sentinel):
    m8 = _sublane_all(jnp.max(s3, axis=0), jnp.maximum)
    idx8 = _sublane_all(jnp.min(jnp.where(s3 == m8[None], iota3, sentinel), axis=0), jnp.minimum)
    return m8, idx8, iota3 == idx8[None]


def _row_iota(groups, n):
    shape = (groups, SUBLANES, n)
    return lax.broadcasted_iota(jnp.int32, shape, 0) * SUBLANES + lax.broadcasted_iota(jnp.int32, shape, 1)


def _route_kernel(x_ref, wq_ref, keys_ref, after_a_ref, after_b_ref, idx_ref, gate_ref,
                  q_scr, st_scr, it_scr, best_scr, pick_scr, gsel_scr):
    del after_a_ref, after_b_ref
    tm = ROUTE_TM
    K = PEER_TOPK
    q_scr[...] = _dot(x_ref[...].astype(BF16), wq_ref[...])
    key_iota = _row_iota(PEER_N_KEYS // SUBLANES, tm)

    def stage1(pair, _):
        hps = (2 * pair, 2 * pair + 1)
        ss = []
        for hp in hps:
            q = q_scr[:, pl.ds(pl.multiple_of(hp * PEER_HALF, PEER_HALF), PEER_HALF)].astype(BF16)
            ss.append(_dot_nt(keys_ref[hp], q).reshape(PEER_N_KEYS // SUBLANES, SUBLANES, tm))
        for i in range(K):
            for n, hp in enumerate(hps):
                m8, idx8, hit = _take_max(ss[n], key_iota, PEER_N_KEYS)
                st_scr[hp, i:i + 1, :] = m8[0:1]
                it_scr[hp, i:i + 1, :] = idx8[0:1]
                ss[n] = jnp.where(hit, -jnp.inf, ss[n])
        return 0

    lax.fori_loop(0, PEER_HEADS, stage1, 0)

    cand_iota = _row_iota(CAND_BLOCKS, tm)

    def stage2(h, _):
        s0, s1 = st_scr[2 * h], st_scr[2 * h + 1]
        i0, i1 = it_scr[2 * h], it_scr[2 * h + 1]
        lo, hi = slice(0, SUBLANES), slice(SUBLANES, 2 * SUBLANES)
        cs = [s0[0:1] + s1[lo], s0[0:1] + s1[hi]]
        ci = [i0[0:1] * PEER_N_KEYS + i1[lo], i0[0:1] * PEER_N_KEYS + i1[hi]]
        for a in range(1, SUBLANES):
            cs.append(s0[a:a + 1] + s1[lo])
            ci.append(i0[a:a + 1] * PEER_N_KEYS + i1[lo])
        cs.append(s0[hi] + s1[0:1])
        ci.append(i0[hi] * PEER_N_KEYS + i1[0:1])
        cand = jnp.stack(cs, axis=0)
        cidx = jnp.stack(ci, axis=0)
        for i in range(K):
            m8, _, hit = _take_max(cand, cand_iota, CAND_BLOCKS * SUBLANES)
            pick8 = _sublane_all(jnp.max(jnp.where(hit, cidx, -1), axis=0), jnp.maximum)
            best_scr[i:i + 1, :] = m8[0:1]
            pick_scr[pl.ds(h * K + i, 1), :] = pick8[0:1]
            cand = jnp.where(hit, -jnp.inf, cand)
        best = best_scr[...]
        e = jnp.exp(best - best[0:1])
        gsel_scr[pl.ds(pl.multiple_of(h * K, K), K), :] = e / jnp.sum(e, axis=0, keepdims=True)
        return 0

    lax.fori_loop(0, PEER_HEADS, stage2, 0)
    idx_ref[...] = pick_scr[...].T
    gate_ref[...] = gsel_scr[...].T


def _route(x1, peer_w_query, peer_sub_keys, after, after_b):
    T = x1.shape[0]
    tm = ROUTE_TM
    wq = peer_w_query.reshape(D_MODEL, 2 * PEER_HEADS * PEER_HALF).astype(BF16)
    keys = peer_sub_keys.reshape(2 * PEER_HEADS, PEER_N_KEYS, PEER_HALF).astype(BF16)
    picks = pl.BlockSpec((tm, PEER_PICKS), lambda i: (i, 0))
    return pl.pallas_call(
        _route_kernel,
        grid=(T // tm,),
        in_specs=[pl.BlockSpec((tm, D_MODEL), lambda i: (i, 0)),
                  pl.BlockSpec(wq.shape, lambda i: (0, 0)),
                  pl.BlockSpec(keys.shape, lambda i: (0, 0, 0)),
                  pl.BlockSpec((SUBLANES, LANES), lambda i: (0, 0)),
                  pl.BlockSpec((SUBLANES, LANES), lambda i: (0, 0))],
        out_specs=[picks, picks],
        out_shape=[jax.ShapeDtypeStruct((T, PEER_PICKS), jnp.int32),
                   jax.ShapeDtypeStruct((T, PEER_PICKS), F32)],
        scratch_shapes=[pltpu.VMEM((tm, 2 * PEER_HEADS * PEER_HALF), F32),
                        pltpu.VMEM((2 * PEER_HEADS, PEER_TOPK, tm), F32),
                        pltpu.VMEM((2 * PEER_HEADS, PEER_TOPK, tm), jnp.int32),
                        pltpu.VMEM((PEER_TOPK, tm), F32),
                        pltpu.VMEM((PEER_PICKS, tm), jnp.int32),
                        pltpu.VMEM((PEER_PICKS, tm), F32)],
        compiler_params=pltpu.CompilerParams(dimension_semantics=("arbitrary",), vmem_limit_bytes=VMEM_LIMIT),
        name="peer_route",
    )(x1, wq, keys, after, after_b)


SC_LANES = 16
SC_ROWS = 64
SC_CHUNKS = PEER_PICKS // SC_ROWS
SC_DCOLS = 256
SC_DVREGS = SC_DCOLS // SC_LANES


def _sc_mesh():
    info = plsc.get_sparse_core_info()
    mesh = plsc.VectorSubcoreMesh(core_axis_name="c", subcore_axis_name="s")
    return mesh, info.num_cores, info.num_cores * info.num_subcores


def _sc_token_pipeline(tab_hbm, idx_hbm, vec_hbm, out_hbm, idx_v, vec_v, rows_v, out_v,
                       sem_rows, sem_tok, sem_out, n_cores, tpw, compute_chunk):
    wid = lax.axis_index("s") * n_cores + lax.axis_index("c")
    base = wid * tpw

    def gather(s, c):
        return pltpu.make_async_copy(tab_hbm.at[idx_v.at[s, pl.ds(c * SC_ROWS, SC_ROWS)]],
                                     rows_v.at[c % 2], sem_rows.at[c % 2])

    def tok_fetch(s, tok):
        return (pltpu.make_async_copy(idx_hbm.at[tok], idx_v.at[s], sem_tok.at[0]),
                pltpu.make_async_copy(vec_hbm.at[tok], vec_v.at[s], sem_tok.at[1]))

    def out_copy(s, tok):
        return pltpu.make_async_copy(out_v.at[s], out_hbm.at[tok], sem_out.at[s])

    for d in tok_fetch(0, base):
        d.start()
    for d in tok_fetch(0, base):
        d.wait()
    gather(0, 0).start()

    def pair_body(tp, _):
        for s in range(2):
            t = tp * 2 + s
            tok = base + t
            has_next = t + 1 < tpw

            @pl.when(has_next)
            def _():
                for d in tok_fetch(1 - s, tok + 1):
                    d.start()

            @pl.when(t >= 2)
            def _():
                out_copy(s, tok - 2).wait()

            for c in range(SC_CHUNKS):
                if c + 1 < SC_CHUNKS:
                    gather(s, c + 1).start()
                else:
                    @pl.when(has_next)
                    def _():
                        for d in tok_fetch(1 - s, tok + 1):
                            d.wait()
                        gather(1 - s, 0).start()
                gather(s, c).wait()
                compute_chunk(s, c)
            out_copy(s, tok).start()
        return 0

    lax.fori_loop(0, tpw // 2, pair_body, 0)
    for s in range(2):
        out_copy(s, base + tpw - 2 + s).wait()


def _sc_scratch(vec_len, out_len):
    return [pltpu.VMEM((2, PEER_PICKS), jnp.int32),
            pltpu.VMEM((2, vec_len), F32),
            pltpu.VMEM((2, SC_ROWS, D_MODEL // 2), jnp.int32),
            pltpu.VMEM((2, out_len), F32),
            pltpu.SemaphoreType.DMA((2,)),
            pltpu.SemaphoreType.DMA((2,)),
            pltpu.SemaphoreType.DMA((2,))]


def _pack_table(tab):
    n, d = tab.shape
    bits = lax.bitcast_convert_type(tab.astype(BF16), jnp.uint16).astype(jnp.uint32).reshape(n, d // 32, 2, SC_LANES)
    words = bits[:, :, 0, :] | (bits[:, :, 1, :] << 16)
    return lax.bitcast_convert_type(words, jnp.int32).reshape(n, d // 2)


def _unpack_pairs(x):
    return list(plsc.unpack(x, format=plsc.PackFormat.INTERLEAVED))


def _sc_acc(v_packed, idx, w):
    T = w.shape[0]
    mesh, n_cores, n_workers = _sc_mesh()
    tpw = T // n_workers
    L = SC_LANES

    @functools.partial(
        pl.kernel, mesh=mesh, out_type=jax.ShapeDtypeStruct((T, D_MODEL), F32),
        scratch_types=_sc_scratch(PEER_PICKS, D_MODEL),
        compiler_params=pltpu.CompilerParams(needs_layout_passes=False), name="peer_sc_acc")
    def k(v_hbm, idx_hbm, w_hbm, out_hbm, idx_v, w_v, rows_v, out_v, sem_rows, sem_tok, sem_out):
        def compute_chunk(s, c):
            b = c % 2
            for dc in range(D_MODEL // SC_DCOLS):
                cols = [pl.ds(dc * SC_DCOLS + j * L, L) for j in range(SC_DVREGS)]
                if c == 0:
                    accs = tuple(jnp.zeros((L,), F32) for _ in cols)
                else:
                    accs = tuple(out_v[s, cs] for cs in cols)

                def row_body(r, accs, dc=dc):
                    wb = plsc.load_gather(w_v.at[s], [jnp.full((L,), c * SC_ROWS, jnp.int32) + r])
                    new = []
                    for j in range(SC_DVREGS // 2):
                        words = rows_v[b, r, pl.ds(dc * (SC_DCOLS // 2) + j * L, L)]
                        lo, hi = _unpack_pairs(plsc.bitcast(words, BF16))
                        new += [accs[2 * j] + wb * lo, accs[2 * j + 1] + wb * hi]
                    return tuple(new)

                accs = lax.fori_loop(0, SC_ROWS, row_body, accs)
                for a, cs in zip(accs, cols):
                    out_v[s, cs] = a

        _sc_token_pipeline(v_hbm, idx_hbm, w_hbm, out_hbm, idx_v, w_v, rows_v, out_v,
                           sem_rows, sem_tok, sem_out, n_cores, tpw, compute_chunk)

    return k(v_packed, idx, w)


DENSE_TM = 2048
DENSE_TN = 1024


def _dense_act_kernel(x_ref, u_ref, o_ref):
    o_ref[...] = _dot_nt(x_ref[...].astype(BF16), u_ref[...])


def _dense_act(x1, u_bf16):
    T, N = x1.shape[0], u_bf16.shape[0]
    return pl.pallas_call(
        _dense_act_kernel, grid=(T // DENSE_TM, N // DENSE_TN),
        in_specs=[pl.BlockSpec((DENSE_TM, D_MODEL), lambda i, j: (i, 0)),
                  pl.BlockSpec((DENSE_TN, D_MODEL), lambda i, j: (j, 0))],
        out_specs=pl.BlockSpec((DENSE_TM, DENSE_TN), lambda i, j: (i, j)),
        out_shape=jax.ShapeDtypeStruct((T, N), F32),
        compiler_params=pltpu.CompilerParams(dimension_semantics=("arbitrary", "arbitrary"),
                                             vmem_limit_bytes=VMEM_LIMIT),
        name="peer_dense_act",
    )(x1, u_bf16)


PICK_TG = SUBLANES
PICK_NQ = 4096


def _sc_pick(dense, idx):
    T, N = dense.shape
    mesh, n_cores, n_workers = _sc_mesh()
    gpw = T // PICK_TG // n_workers
    n_q = N // PICK_NQ
    L = SC_LANES

    @functools.partial(
        pl.kernel, mesh=mesh, out_type=jax.ShapeDtypeStruct((T, PEER_PICKS), F32),
        scratch_types=[pltpu.VMEM((2, PICK_TG, PEER_PICKS), jnp.int32), pltpu.VMEM((2, PICK_TG, PICK_NQ), F32),
                       pltpu.VMEM((2, PICK_TG, PEER_PICKS), F32),
                       pltpu.SemaphoreType.DMA((2,)), pltpu.SemaphoreType.DMA((2,)), pltpu.SemaphoreType.DMA((2,))],
        compiler_params=pltpu.CompilerParams(needs_layout_passes=False), name="peer_sc_pick")
    def k(d_hbm, idx_hbm, act_hbm, idx_v, buf, act_v, sem_idx, sem_buf, sem_out):
        g0 = (lax.axis_index("s") * n_cores + lax.axis_index("c")) * gpw

        def idx_fetch(gs, g):
            return pltpu.make_async_copy(idx_hbm.at[pl.ds(g * PICK_TG, PICK_TG)], idx_v.at[gs], sem_idx.at[gs])

        def blk_fetch(g, q):
            return pltpu.make_async_copy(d_hbm.at[pl.ds(g * PICK_TG, PICK_TG), pl.ds(q * PICK_NQ, PICK_NQ)],
                                         buf.at[q % 2], sem_buf.at[q % 2])

        def out_copy(gs, g):
            return pltpu.make_async_copy(act_v.at[gs], act_hbm.at[pl.ds(g * PICK_TG, PICK_TG)], sem_out.at[gs])

        idx_fetch(0, g0).start()
        blk_fetch(g0, 0).start()

        def pair_body(gp, _):
            for gs in range(2):
                gi = gp * 2 + gs
                g = g0 + gi
                has_next = gi + 1 < gpw

                @pl.when(has_next)
                def _():
                    idx_fetch(1 - gs, g + 1).start()

                idx_fetch(gs, g).wait()

                @pl.when(gi >= 2)
                def _():
                    out_copy(gs, g - 2).wait()

                for q in range(n_q):
                    if q + 1 < n_q:
                        blk_fetch(g, q + 1).start()
                    else:
                        @pl.when(has_next)
                        def _():
                            blk_fetch(g + 1, 0).start()
                    blk_fetch(g, q).wait()
                    for t in range(PICK_TG):
                        for v in range(PEER_PICKS // L):
                            picks = pl.ds(v * L, L)
                            local = idx_v[gs, t, picks] - q * PICK_NQ
                            inside = (local >= 0) & (local < PICK_NQ)
                            val = plsc.load_gather(buf, [jnp.full((L,), q % 2, jnp.int32), jnp.full((L,), t, jnp.int32),
                                                         jnp.where(inside, local, 0)])
                            act_v[gs, t, picks] = val if q == 0 else jnp.where(inside, val, act_v[gs, t, picks])
                out_copy(gs, g).start()
            return 0

        lax.fori_loop(0, gpw // 2, pair_body, 0)
        for gs in range(2):
            out_copy(gs, g0 + gpw - 2 + gs).wait()

    return k(dense, idx)


def _gelu_gate_kernel(a_ref, g_ref, o_ref):
    a = a_ref[...]
    o_ref[...] = 0.5 * a * (1.0 + lax.erf(a * (2.0 ** -0.5))) * g_ref[...]


def _ln2_kernel(h_ref, f_ref, g_ref, b_ref, full_ref, o_ref):
    del full_ref
    o_ref[...] = _layer_norm(ALPHA * h_ref[...] + f_ref[...], g_ref[...], b_ref[...])


MIX_TM = 512
_MIX_PARAMS = pltpu.CompilerParams(dimension_semantics=("arbitrary",))


def _mix_row(width):
    return pl.BlockSpec((MIX_TM, width), lambda i: (i, 0))


def _gelu_gate_call(act, gates):
    T = act.shape[0]
    return pl.pallas_call(
        _gelu_gate_kernel, grid=(T // MIX_TM,), in_specs=[_mix_row(PEER_PICKS)] * 2, out_specs=_mix_row(PEER_PICKS),
        out_shape=jax.ShapeDtypeStruct((T, PEER_PICKS), F32), compiler_params=_MIX_PARAMS, name="peer_gelu_gate",
    )(act, gates)


def _mix_finish(x1, idx, w, v_packed, ln_gain, ln_bias, result, row0):
    T = x1.shape[0]
    const = pl.BlockSpec((1, D_MODEL), lambda i: (0, 0))
    blk0 = row0 // MIX_TM
    ffn = _sc_acc(v_packed, idx, w)
    result = pl.pallas_call(
        _ln2_kernel, grid=(T // MIX_TM,),
        in_specs=[_mix_row(D_MODEL)] * 2 + [const, const, pl.BlockSpec(memory_space=pl.ANY)],
        out_specs=pl.BlockSpec((MIX_TM, D_MODEL), lambda i: (blk0 + i, 0)),
        out_shape=jax.ShapeDtypeStruct(result.shape, F32), input_output_aliases={4: 0},
        compiler_params=_MIX_PARAMS, name="peer_ln2",
    )(x1, ffn, ln_gain.reshape(1, D_MODEL), ln_bias.reshape(1, D_MODEL), result)
    return result, ffn


BATCH_CHUNKS = 16
SC_GATE_LAG = 1
SC_PASS_LAG = 3


def kernel(x, w_in, fox_f_bias, gla_gate_up, gla_gate_bias, gla_norm_gain, w_out_fox, w_out_gla, w_out,
           ln1_gain, ln1_bias, peer_w_query, peer_sub_keys, peer_expert_u, peer_expert_v, ln2_gain, ln2_bias):
    B, S, D = x.shape
    assert D == D_MODEL and S % max(ROUTE_TM, FOX_TQ) == 0
    assert DEPTH == 1 and w_in.shape[0] == 1, "the chunk pipeline below is written for the single-layer block"
    u_bf16, v_packed = peer_expert_u[0].astype(BF16), _pack_table(peer_expert_v[0])
    n_chunks = BATCH_CHUNKS if B % BATCH_CHUNKS == 0 else 1
    bc = B // n_chunks
    ffns, ws = [], []
    result = jnp.zeros((B * S, D), F32)
    for ci in range(n_chunks):
        h = x[ci * bc:(ci + 1) * bc].reshape(bc * S, D)
        fqkv, gqk, gv, gg, mg, small = _in_proj(h, w_in[0])
        c_t = _fox_gate(small, fox_f_bias[0], bc, S)
        attn = _fox_attn(fqkv, c_t, bc, S)
        og = _gla(gqk, gv, small, gg, gla_gate_up[0], gla_gate_bias[0], gla_norm_gain[0], bc, S)
        x1 = _merge(attn, og, mg, h, w_out_fox[0], w_out_gla[0], w_out[0], ln1_gain[0], ln1_bias[0])
        zero = jnp.zeros((SUBLANES, LANES), F32)
        after = ws[ci - SC_GATE_LAG][:SUBLANES, :LANES] if ci >= SC_GATE_LAG else zero
        after_b = ffns[ci - SC_PASS_LAG][:SUBLANES, :LANES] if ci >= SC_PASS_LAG else zero
        idx, gates = _route(x1, peer_w_query[0], peer_sub_keys[0], after, after_b)
        act = _sc_pick(_dense_act(x1, u_bf16), idx)
        w = _gelu_gate_call(act, gates)
        result, ffn = _mix_finish(x1, idx, w, v_packed, ln2_gain[0], ln2_bias[0], result, ci * bc * S)
        ffns.append(ffn)
        ws.append(w)
    return result.reshape(B, S, D)
```

```python
import functools
import math

import jax
import jax.numpy as jnp
from jax import lax
from jax.experimental import pallas as pl
from jax.experimental.pallas import tpu as pltpu
from jax.experimental.pallas import tpu_sc as plsc

F32 = jnp.float32
BF16 = jnp.bfloat16

D_MODEL = 1024
FOX_HEADS = 8
FOX_HEAD_DIM = 64
FOX_WIDTH = FOX_HEADS * FOX_HEAD_DIM
GLA_HEADS = 4
GLA_KEY_DIM = 128
GLA_VAL_DIM = 256
GLA_QK_WIDTH = GLA_HEADS * GLA_KEY_DIM
GLA_V_WIDTH = GLA_HEADS * GLA_VAL_DIM
GLA_GATE_RANK = 16
GLA_GATE_TEMP = 16.0
GLA_NORM_EPS = 1e-5
GLA_CHUNK = 64
PEER_HEADS = 8
PEER_N_KEYS = 128
PEER_HALF = 128
PEER_TOPK = 16
PEER_PICKS = PEER_HEADS * PEER_TOPK
DEPTH = 1
ALPHA = (2.0 * DEPTH) ** 0.25
LN_EPS = 1e-5

LANES = 128
SUBLANES = 8
VMEM_LIMIT = 52 * 1024 * 1024

IN_SPLIT_SIZES = (FOX_WIDTH, FOX_WIDTH, FOX_WIDTH, FOX_HEADS,
                  GLA_QK_WIDTH, GLA_QK_WIDTH, GLA_V_WIDTH, GLA_V_WIDTH, GLA_GATE_RANK,
                  D_MODEL, D_MODEL)
FF_COL = 0
GLR_COL = FOX_HEADS


def _dot(a, b, **kw):
    return jnp.dot(a, b, preferred_element_type=F32, **kw)


def _dot_nt(a, b):
    return lax.dot_general(a, b, (((1,), (1,)), ((), ())), preferred_element_type=F32)


def _dot_tn(a, b):
    return lax.dot_general(a, b, (((0,), (0,)), ((), ())), preferred_element_type=F32)


def _layer_norm(y, gain, bias):
    mu = jnp.mean(y, axis=-1, keepdims=True)
    yc = y - mu
    var = jnp.mean(yc * yc, axis=-1, keepdims=True)
    return yc * lax.rsqrt(var + LN_EPS) * gain + bias


def _in_proj_kernel(x_ref, wf_ref, wgqk_ref, wgv_ref, wgg_ref, wm_ref, ws_ref,
                    f_ref, gqk_ref, gv_ref, gg_ref, m_ref, s_ref):
    xb = x_ref[...].astype(BF16)
    f_ref[...] = _dot(xb, wf_ref[...]).astype(BF16)
    gqk_ref[...] = _dot(xb, wgqk_ref[...]).astype(BF16)
    gv_ref[...] = _dot(xb, wgv_ref[...]).astype(BF16)
    gg_ref[...] = _dot(xb, wgg_ref[...])
    m_ref[...] = _dot(xb, wm_ref[...])
    s_ref[...] = _dot(xb, ws_ref[...])


def _in_proj(x2, w_in):
    T = x2.shape[0]
    tm = 256
    pts = [0]
    for s in IN_SPLIT_SIZES:
        pts.append(pts[-1] + s)
    col = lambda i, j: w_in[:, pts[i]:pts[j]]
    wf = col(0, 3).astype(BF16)
    wgqk = col(4, 6).astype(BF16)
    wgv = col(6, 7).astype(BF16)
    wgg = col(7, 8).astype(BF16)
    wm = col(9, 11).astype(BF16)
    ws = jnp.concatenate([col(3, 4), col(8, 9)], axis=1)
    ws = jnp.pad(ws, ((0, 0), (0, LANES - ws.shape[1]))).astype(BF16)
    ws_list = [wf, wgqk, wgv, wgg, wm, ws]
    out_dtypes = [BF16, BF16, BF16, F32, F32, F32]
    const = lambda w: pl.BlockSpec(w.shape, lambda i: (0, 0))
    return pl.pallas_call(
        _in_proj_kernel,
        grid=(T // tm,),
        in_specs=[pl.BlockSpec((tm, D_MODEL), lambda i: (i, 0))] + [const(w) for w in ws_list],
        out_specs=[pl.BlockSpec((tm, w.shape[1]), lambda i: (i, 0)) for w in ws_list],
        out_shape=[jax.ShapeDtypeStruct((T, w.shape[1]), dt) for w, dt in zip(ws_list, out_dtypes)],
        compiler_params=pltpu.CompilerParams(dimension_semantics=("arbitrary",), vmem_limit_bytes=VMEM_LIMIT),
        name="in_proj",
    )(x2, *ws_list)


def _fox_gate_kernel(s_ref, bias_ref, c_ref):
    S = s_ref.shape[0]
    ff_t = s_ref[...].T[FF_COL:FF_COL + FOX_HEADS, :]
    log_f = jax.nn.log_sigmoid(ff_t + bias_ref[...])
    r = lax.broadcasted_iota(jnp.int32, (LANES, LANES), 0)
    c = lax.broadcasted_iota(jnp.int32, (LANES, LANES), 1)
    tri = (r <= c).astype(F32)
    carry = jnp.zeros((FOX_HEADS, 1), F32)
    for j in range(S // LANES):
        blk = log_f[:, j * LANES:(j + 1) * LANES]
        cs = _dot(blk, tri, precision=lax.Precision.HIGHEST) + carry
        c_ref[0, :, j * LANES:(j + 1) * LANES] = cs
        carry = cs[:, LANES - 1:LANES]


def _fox_gate(small, fox_f_bias, B, S):
    return pl.pallas_call(
        _fox_gate_kernel,
        grid=(B,),
        in_specs=[pl.BlockSpec((S, LANES), lambda b: (b, 0)),
                  pl.BlockSpec((FOX_HEADS, 1), lambda b: (0, 0))],
        out_specs=pl.BlockSpec((1, FOX_HEADS, S), lambda b: (b, 0, 0)),
        out_shape=jax.ShapeDtypeStruct((B, FOX_HEADS, S), F32),
        compiler_params=pltpu.CompilerParams(dimension_semantics=("arbitrary",)),
        name="fox_gate",
    )(small, fox_f_bias.reshape(FOX_HEADS, 1))


FOX_TQ = 256
FOX_TK = FOX_TQ
FOX_COLS = 256


def _fox_attn_kernel(q_ref, k_ref, v_ref, c_ref, o_ref):
    qi = pl.program_id(2)
    tq, tk, dh = FOX_TQ, FOX_TK, FOX_HEAD_DIM
    n_h = FOX_COLS // dh
    qs = [q_ref[:, hh * dh:(hh + 1) * dh] * (dh ** -0.5) for hh in range(n_h)]

    def step(j, carry, masked):
        ks = pl.ds(pl.multiple_of(j * tk, tk), tk)
        k2 = k_ref[ks, :]
        v2 = v_ref[ks, :]
        out = []
        for hh in range(n_h):
            m, l, acc = carry[hh]
            s = _dot_nt(qs[hh], k2[:, hh * dh:(hh + 1) * dh]) - c_ref[hh, :, ks]
            if masked:
                r = lax.broadcasted_iota(jnp.int32, (tq, tk), 0)
                c = lax.broadcasted_iota(jnp.int32, (tq, tk), 1)
                s = jnp.where(c <= r, s, -jnp.inf)
            m_new = jnp.maximum(m, jnp.max(s, axis=1, keepdims=True))
            p = jnp.exp(s - m_new)
            a = jnp.exp(m - m_new)
            l = a * l + jnp.sum(p, axis=1, keepdims=True)
            acc = a * acc + _dot(p.astype(BF16), v2[:, hh * dh:(hh + 1) * dh])
            out.append((m_new, l, acc))
        return tuple(out)

    init = tuple((jnp.full((tq, 1), -jnp.inf, F32), jnp.zeros((tq, 1), F32), jnp.zeros((tq, dh), F32))
                 for _ in range(n_h))
    carry = lax.fori_loop(0, qi, lambda j, c: step(j, c, False), init)
    carry = step(qi, carry, True)
    o_ref[...] = jnp.concatenate([acc / l for _, l, acc in carry], axis=1).astype(BF16)


def _fox_attn(fqkv, c_t, B, S):
    T = B * S
    nq = S // FOX_TQ
    n_hp = FOX_WIDTH // FOX_COLS
    return pl.pallas_call(
        _fox_attn_kernel,
        grid=(B, n_hp, nq),
        in_specs=[pl.BlockSpec((FOX_TQ, FOX_COLS), lambda b, h, i: (b * nq + i, h)),
                  pl.BlockSpec((S, FOX_COLS), lambda b, h, i: (b, n_hp + h)),
                  pl.BlockSpec((S, FOX_COLS), lambda b, h, i: (b, 2 * n_hp + h)),
                  pl.BlockSpec((FOX_COLS // FOX_HEAD_DIM, 1, S), lambda b, h, i: (b * n_hp + h, 0, 0))],
        out_specs=pl.BlockSpec((FOX_TQ, FOX_COLS), lambda b, h, i: (b * nq + i, h)),
        out_shape=jax.ShapeDtypeStruct((T, FOX_WIDTH), BF16),
        compiler_params=pltpu.CompilerParams(dimension_semantics=("arbitrary", "arbitrary", "arbitrary")),
        name="fox_attn",
    )(fqkv, fqkv, fqkv, c_t.reshape(B * FOX_HEADS, 1, S))


GLA_GROUP = 4


def _gla_kernel(q_ref, k_ref, v_ref, s_ref, up_ref, gb_ref, gg_ref, gain_ref, o_ref):
    S = q_ref.shape[0]
    C = GLA_CHUNK
    dk, dv = GLA_KEY_DIM, GLA_VAL_DIM
    r = lax.broadcasted_iota(jnp.int32, (C, C), 0)
    c = lax.broadcasted_iota(jnp.int32, (C, C), 1)
    tril = (r >= c).astype(F32)

    def body(ci, states):
        rows = pl.ds(pl.multiple_of(ci * C, C), C)
        z = _dot(s_ref[rows, :].astype(BF16), up_ref[...]) + gb_ref[...]
        la = jax.nn.log_sigmoid(z) * (1.0 / GLA_GATE_TEMP)
        cum = _dot(tril, la, precision=lax.Precision.HIGHEST)
        tot = cum[C - 1:C, :]
        kd_all = (k_ref[rows, :].astype(F32) * jnp.exp(tot - cum)).astype(BF16)
        decay = jnp.exp(tot)
        new_states = []
        for g, st_t in enumerate(states):
            kc, vc = slice(g * dk, (g + 1) * dk), slice(g * dv, (g + 1) * dv)
            st_t = st_t * decay[:, kc] + _dot_tn(v_ref[rows, vc], kd_all[:, kc])
            o = _dot_nt(q_ref[rows, kc], st_t.astype(BF16)) * (dk ** -0.5)
            o = o * lax.rsqrt(jnp.mean(o * o, axis=-1, keepdims=True) + GLA_NORM_EPS) * gain_ref[:, vc]
            o = o * jax.nn.silu(gg_ref[rows, vc])
            o_ref[rows, vc] = o.astype(BF16)
            new_states.append(st_t)
        return tuple(new_states)

    lax.fori_loop(0, S // C, body, tuple(jnp.zeros((dv, dk), F32) for _ in range(GLA_GROUP)), unroll=4)


def _gla(gqk, gv, small, gg, gla_gate_up, gla_gate_bias, gla_norm_gain, B, S):
    T = B * S
    up = jnp.zeros((LANES, GLA_QK_WIDTH), F32).at[GLR_COL:GLR_COL + GLA_GATE_RANK].set(gla_gate_up).astype(BF16)
    gb = gla_gate_bias.reshape(1, GLA_QK_WIDTH)
    gain = gla_norm_gain.reshape(1, GLA_V_WIDTH)
    n_groups = GLA_HEADS // GLA_GROUP
    kw, vw = GLA_GROUP * GLA_KEY_DIM, GLA_GROUP * GLA_VAL_DIM
    return pl.pallas_call(
        _gla_kernel,
        grid=(B, n_groups),
        in_specs=[pl.BlockSpec((S, kw), lambda b, h: (b, h)),
                  pl.BlockSpec((S, kw), lambda b, h: (b, n_groups + h)),
                  pl.BlockSpec((S, vw), lambda b, h: (b, h)),
                  pl.BlockSpec((S, LANES), lambda b, h: (b, 0)),
                  pl.BlockSpec((LANES, kw), lambda b, h: (0, h)),
                  pl.BlockSpec((1, kw), lambda b, h: (0, h)),
                  pl.BlockSpec((S, vw), lambda b, h: (b, h)),
                  pl.BlockSpec((1, vw), lambda b, h: (0, h))],
        out_specs=pl.BlockSpec((S, vw), lambda b, h: (b, h)),
        out_shape=jax.ShapeDtypeStruct((T, GLA_V_WIDTH), BF16),
        compiler_params=pltpu.CompilerParams(dimension_semantics=("arbitrary", "arbitrary")),
        name="gla",
    )(gqk, gqk, gv, small, up, gb, gg, gain)


def _merge_kernel(a_ref, og_ref, m_ref, x_ref, wf_ref, wg_ref, wo_ref, g_ref, b_ref, o_ref):
    y_fox = _dot(a_ref[...], wf_ref[...])
    y_gla = _dot(og_ref[...], wg_ref[...])
    merged = (jax.nn.sigmoid(m_ref[:, :D_MODEL]) * y_fox + jax.nn.sigmoid(m_ref[:, D_MODEL:]) * y_gla)
    mix = _dot(merged.astype(BF16), wo_ref[...])
    o_ref[...] = _layer_norm(ALPHA * x_ref[...] + mix, g_ref[...], b_ref[...])


def _merge(attn, og, mg, x2, w_out_fox, w_out_gla, w_out, ln_gain, ln_bias):
    T = x2.shape[0]
    tm = 512
    row = lambda w: pl.BlockSpec((tm, w), lambda i: (i, 0))
    const = lambda a: pl.BlockSpec(a.shape, lambda i: (0, 0))
    ws = [w_out_fox.astype(BF16), w_out_gla.astype(BF16), w_out.astype(BF16),
          ln_gain.reshape(1, D_MODEL), ln_bias.reshape(1, D_MODEL)]
    return pl.pallas_call(
        _merge_kernel,
        grid=(T // tm,),
        in_specs=[row(FOX_WIDTH), row(GLA_V_WIDTH), row(2 * D_MODEL), row(D_MODEL)] + [const(w) for w in ws],
        out_specs=row(D_MODEL),
        out_shape=jax.ShapeDtypeStruct((T, D_MODEL), F32),
        compiler_params=pltpu.CompilerParams(dimension_semantics=("arbitrary",), vmem_limit_bytes=VMEM_LIMIT),
        name="merge",
    )(attn, og, mg, x2, *ws)


ROUTE_TM = 256
CAND_BLOCKS = 10


def _sublane_all(x, op):
    for shift in (4, 2, 1):
        x = op(x, pltpu.roll(x, shift, axis=0))
    return x


def _take_max(s3, iota3, sentinel):
    m8 = _sublane_all(jnp.max(s3, axis=0), jnp.maximum)
    idx8 = _sublane_all(jnp.min(jnp.where(s3 == m8[None], iota3, sentinel), axis=0), jnp.minimum)
    return m8, idx8, iota3 == idx8[None]


def _row_iota(groups, n):
    shape = (groups, SUBLANES, n)
    return lax.broadcasted_iota(jnp.int32, shape, 0) * SUBLANES + lax.broadcasted_iota(jnp.int32, shape, 1)


def _route_kernel(x_ref, wq_ref, keys_ref, after_a_ref, after_b_ref, idx_ref, gate_ref,
                  q_scr, st_scr, it_scr, best_scr, pick_scr, gsel_scr):
    del after_a_ref, after_b_ref
    tm = ROUTE_TM
    K = PEER_TOPK
    q_scr[...] = _dot(x_ref[...].astype(BF16), wq_ref[...])
    key_iota = _row_iota(PEER_N_KEYS // SUBLANES, tm)

    def stage1(pair, _):
        hps = (2 * pair, 2 * pair + 1)
        ss = []
        for hp in hps:
            q = q_scr[:, pl.ds(pl.multiple_of(hp * PEER_HALF, PEER_HALF), PEER_HALF)].astype(BF16)
            ss.append(_dot_nt(keys_ref[hp], q).reshape(PEER_N_KEYS // SUBLANES, SUBLANES, tm))
        for i in range(K):
            for n, hp in enumerate(hps):
                m8, idx8, hit = _take_max(ss[n], key_iota, PEER_N_KEYS)
                st_scr[hp, i:i + 1, :] = m8[0:1]
                it_scr[hp, i:i + 1, :] = idx8[0:1]
                ss[n] = jnp.where(hit, -jnp.inf, ss[n])
        return 0

    lax.fori_loop(0, PEER_HEADS, stage1, 0)

    cand_iota = _row_iota(CAND_BLOCKS, tm)

    def stage2(h, _):
        s0, s1 = st_scr[2 * h], st_scr[2 * h + 1]
        i0, i1 = it_scr[2 * h], it_scr[2 * h + 1]
        lo, hi = slice(0, SUBLANES), slice(SUBLANES, 2 * SUBLANES)
        cs = [s0[0:1] + s1[lo], s0[0:1] + s1[hi]]
        ci = [i0[0:1] * PEER_N_KEYS + i1[lo], i0[0:1] * PEER_N_KEYS + i1[hi]]
        for a in range(1, SUBLANES):
            cs.append(s0[a:a + 1] + s1[lo])
            ci.append(i0[a:a + 1] * PEER_N_KEYS + i1[lo])
        cs.append(s0[hi] + s1[0:1])
        ci.append(i0[hi] * PEER_N_KEYS + i1[0:1])
        cand = jnp.stack(cs, axis=0)
        cidx = jnp.stack(ci, axis=0)
        for i in range(K):
            m8, _, hit = _take_max(cand, cand_iota, CAND_BLOCKS * SUBLANES)
            pick8 = _sublane_all(jnp.max(jnp.where(hit, cidx, -1), axis=0), jnp.maximum)
            best_scr[i:i + 1, :] = m8[0:1]
            pick_scr[pl.ds(h * K + i, 1), :] = pick8[0:1]
            cand = jnp.where(hit, -jnp.inf, cand)
        best = best_scr[...]
        e = jnp.exp(best - best[0:1])
        gsel_scr[pl.ds(pl.multiple_of(h * K, K), K), :] = e / jnp.sum(e, axis=0, keepdims=True)
        return 0

    lax.fori_loop(0, PEER_HEADS, stage2, 0, unroll=2)
    idx_ref[...] = pick_scr[...].T
    gate_ref[...] = gsel_scr[...].T


def _route(x1, peer_w_query, peer_sub_keys, after, after_b):
    T = x1.shape[0]
    tm = ROUTE_TM
    wq = peer_w_query.reshape(D_MODEL, 2 * PEER_HEADS * PEER_HALF).astype(BF16)
    keys = peer_sub_keys.reshape(2 * PEER_HEADS, PEER_N_KEYS, PEER_HALF).astype(BF16)
    picks = pl.BlockSpec((tm, PEER_PICKS), lambda i: (i, 0))
    return pl.pallas_call(
        _route_kernel,
        grid=(T // tm,),
        in_specs=[pl.BlockSpec((tm, D_MODEL), lambda i: (i, 0)),
                  pl.BlockSpec(wq.shape, lambda i: (0, 0)),
                  pl.BlockSpec(keys.shape, lambda i: (0, 0, 0)),
                  pl.BlockSpec((SUBLANES, LANES), lambda i: (0, 0)),
                  pl.BlockSpec((SUBLANES, LANES), lambda i: (0, 0))],
        out_specs=[picks, picks],
        out_shape=[jax.ShapeDtypeStruct((T, PEER_PICKS), jnp.int32),
                   jax.ShapeDtypeStruct((T, PEER_PICKS), F32)],
        scratch_shapes=[pltpu.VMEM((tm, 2 * PEER_HEADS * PEER_HALF), F32),
                        pltpu.VMEM((2 * PEER_HEADS, PEER_TOPK, tm), F32),
                        pltpu.VMEM((2 * PEER_HEADS, PEER_TOPK, tm), jnp.int32),
                        pltpu.VMEM((PEER_TOPK, tm), F32),
                        pltpu.VMEM((PEER_PICKS, tm), jnp.int32),
                        pltpu.VMEM((PEER_PICKS, tm), F32)],
        compiler_params=pltpu.CompilerParams(dimension_semantics=("arbitrary",), vmem_limit_bytes=VMEM_LIMIT),
        name="peer_route",
    )(x1, wq, keys, after, after_b)


SC_LANES = 16
SC_ROWS = 64
SC_CHUNKS = PEER_PICKS // SC_ROWS
SC_DCOLS = 256
SC_DVREGS = SC_DCOLS // SC_LANES


def _sc_mesh():
    info = plsc.get_sparse_core_info()
    mesh = plsc.VectorSubcoreMesh(core_axis_name="c", subcore_axis_name="s")
    return mesh, info.num_cores, info.num_cores * info.num_subcores


def _sc_token_pipeline(tab_hbm, idx_hbm, vec_hbm, out_hbm, idx_v, vec_v, rows_v, out_v,
                       sem_rows, sem_tok, sem_out, n_cores, tpw, compute_chunk):
    wid = lax.axis_index("s") * n_cores + lax.axis_index("c")
    base = wid * tpw

    def gather(s, c):
        return pltpu.make_async_copy(tab_hbm.at[idx_v.at[s, pl.ds(c * SC_ROWS, SC_ROWS)]],
                                     rows_v.at[c % 2], sem_rows.at[c % 2])

    def tok_fetch(s, tok):
        return (pltpu.make_async_copy(idx_hbm.at[tok], idx_v.at[s], sem_tok.at[0]),
                pltpu.make_async_copy(vec_hbm.at[tok], vec_v.at[s], sem_tok.at[1]))

    def out_copy(s, tok):
        return pltpu.make_async_copy(out_v.at[s], out_hbm.at[tok], sem_out.at[s])

    for d in tok_fetch(0, base):
        d.start()
    for d in tok_fetch(0, base):
        d.wait()
    gather(0, 0).start()

    def pair_body(tp, _):
        for s in range(2):
            t = tp * 2 + s
            tok = base + t
            has_next = t + 1 < tpw

            @pl.when(has_next)
            def _():
                for d in tok_fetch(1 - s, tok + 1):
                    d.start()

            @pl.when(t >= 2)
            def _():
                out_copy(s, tok - 2).wait()

            for c in range(SC_CHUNKS):
                if c + 1 < SC_CHUNKS:
                    gather(s, c + 1).start()
                else:
                    @pl.when(has_next)
                    def _():
                        for d in tok_fetch(1 - s, tok + 1):
                            d.wait()
                        gather(1 - s, 0).start()
                gather(s, c).wait()
                compute_chunk(s, c)
            out_copy(s, tok).start()
        return 0

    lax.fori_loop(0, tpw // 2, pair_body, 0)
    for s in range(2):
        out_copy(s, base + tpw - 2 + s).wait()


def _sc_scratch(vec_len, out_len):
    return [pltpu.VMEM((2, PEER_PICKS), jnp.int32),
            pltpu.VMEM((2, vec_len), F32),
            pltpu.VMEM((2, SC_ROWS, D_MODEL // 2), jnp.int32),
            pltpu.VMEM((2, out_len), F32),
            pltpu.SemaphoreType.DMA((2,)),
            pltpu.SemaphoreType.DMA((2,)),
            pltpu.SemaphoreType.DMA((2,))]


def _pack_table(tab):
    n, d = tab.shape
    bits = lax.bitcast_convert_type(tab.astype(BF16), jnp.uint16).astype(jnp.uint32).reshape(n, d // 32, 2, SC_LANES)
    words = bits[:, :, 0, :] | (bits[:, :, 1, :] << 16)
    return lax.bitcast_convert_type(words, jnp.int32).reshape(n, d // 2)


def _unpack_pairs(x):
    return list(plsc.unpack(x, format=plsc.PackFormat.INTERLEAVED))


def _sc_acc(v_packed, idx, w):
    T = w.shape[0]
    mesh, n_cores, n_workers = _sc_mesh()
    tpw = T // n_workers
    L = SC_LANES

    @functools.partial(
        pl.kernel, mesh=mesh, out_type=jax.ShapeDtypeStruct((T, D_MODEL), F32),
        scratch_types=_sc_scratch(PEER_PICKS, D_MODEL),
        compiler_params=pltpu.CompilerParams(needs_layout_passes=False), name="peer_sc_acc")
    def k(v_hbm, idx_hbm, w_hbm, out_hbm, idx_v, w_v, rows_v, out_v, sem_rows, sem_tok, sem_out):
        def compute_chunk(s, c):
            b = c % 2
            for dc in range(D_MODEL // SC_DCOLS):
                cols = [pl.ds(dc * SC_DCOLS + j * L, L) for j in range(SC_DVREGS)]
                if c == 0:
                    accs = tuple(jnp.zeros((L,), F32) for _ in cols)
                else:
                    accs = tuple(out_v[s, cs] for cs in cols)

                def row_body(r, accs, dc=dc):
                    wb = plsc.load_gather(w_v.at[s], [jnp.full((L,), c * SC_ROWS, jnp.int32) + r])
                    new = []
                    for j in range(SC_DVREGS // 2):
                        words = rows_v[b, r, pl.ds(dc * (SC_DCOLS // 2) + j * L, L)]
                        lo, hi = _unpack_pairs(plsc.bitcast(words, BF16))
                        new += [accs[2 * j] + wb * lo, accs[2 * j + 1] + wb * hi]
                    return tuple(new)

                accs = lax.fori_loop(0, SC_ROWS, row_body, accs)
                for a, cs in zip(accs, cols):
                    out_v[s, cs] = a

        _sc_token_pipeline(v_hbm, idx_hbm, w_hbm, out_hbm, idx_v, w_v, rows_v, out_v,
                           sem_rows, sem_tok, sem_out, n_cores, tpw, compute_chunk)

    return k(v_packed, idx, w)


DENSE_TM = 2048
DENSE_TN = 1024


def _dense_act_kernel(x_ref, u_ref, o_ref):
    o_ref[...] = _dot_nt(x_ref[...].astype(BF16), u_ref[...])


def _dense_act(x1, u_bf16):
    T, N = x1.shape[0], u_bf16.shape[0]
    return pl.pallas_call(
        _dense_act_kernel, grid=(T // DENSE_TM, N // DENSE_TN),
        in_specs=[pl.BlockSpec((DENSE_TM, D_MODEL), lambda i, j: (i, 0)),
                  pl.BlockSpec((DENSE_TN, D_MODEL), lambda i, j: (j, 0))],
        out_specs=pl.BlockSpec((DENSE_TM, DENSE_TN), lambda i, j: (i, j)),
        out_shape=jax.ShapeDtypeStruct((T, N), F32),
        compiler_params=pltpu.CompilerParams(dimension_semantics=("arbitrary", "arbitrary"),
                                             vmem_limit_bytes=VMEM_LIMIT),
        name="peer_dense_act",
    )(x1, u_bf16)


PICK_TG = SUBLANES
PICK_NQ = 4096


def _sc_pick(dense, idx):
    T, N = dense.shape
    mesh, n_cores, n_workers = _sc_mesh()
    gpw = T // PICK_TG // n_workers
    n_q = N // PICK_NQ
    L = SC_LANES

    @functools.partial(
        pl.kernel, mesh=mesh, out_type=jax.ShapeDtypeStruct((T, PEER_PICKS), F32),
        scratch_types=[pltpu.VMEM((2, PICK_TG, PEER_PICKS), jnp.int32), pltpu.VMEM((2, PICK_TG, PICK_NQ), F32),
                       pltpu.VMEM((2, PICK_TG, PEER_PICKS), F32),
                       pltpu.SemaphoreType.DMA((2,)), pltpu.SemaphoreType.DMA((2,)), pltpu.SemaphoreType.DMA((2,))],
        compiler_params=pltpu.CompilerParams(needs_layout_passes=False), name="peer_sc_pick")
    def k(d_hbm, idx_hbm, act_hbm, idx_v, buf, act_v, sem_idx, sem_buf, sem_out):
        g0 = (lax.axis_index("s") * n_cores + lax.axis_index("c")) * gpw

        def idx_fetch(gs, g):
            return pltpu.make_async_copy(idx_hbm.at[pl.ds(g * PICK_TG, PICK_TG)], idx_v.at[gs], sem_idx.at[gs])

        def blk_fetch(g, q):
            return pltpu.make_async_copy(d_hbm.at[pl.ds(g * PICK_TG, PICK_TG), pl.ds(q * PICK_NQ, PICK_NQ)],
                                         buf.at[q % 2], sem_buf.at[q % 2])

        def out_copy(gs, g):
            return pltpu.make_async_copy(act_v.at[gs], act_hbm.at[pl.ds(g * PICK_TG, PICK_TG)], sem_out.at[gs])

        idx_fetch(0, g0).start()
        blk_fetch(g0, 0).start()

        def pair_body(gp, _):
            for gs in range(2):
                gi = gp * 2 + gs
                g = g0 + gi
                has_next = gi + 1 < gpw

                @pl.when(has_next)
                def _():
                    idx_fetch(1 - gs, g + 1).start()

                idx_fetch(gs, g).wait()

                @pl.when(gi >= 2)
                def _():
                    out_copy(gs, g - 2).wait()

                for q in range(n_q):
                    if q + 1 < n_q:
                        blk_fetch(g, q + 1).start()
                    else:
                        @pl.when(has_next)
                        def _():
                            blk_fetch(g + 1, 0).start()
                    blk_fetch(g, q).wait()
                    for t in range(PICK_TG):
                        for v in range(PEER_PICKS // L):
                            picks = pl.ds(v * L, L)
                            local = idx_v[gs, t, picks] - q * PICK_NQ
                            inside = (local >= 0) & (local < PICK_NQ)
                            val = plsc.load_gather(buf, [jnp.full((L,), q % 2, jnp.int32), jnp.full((L,), t, jnp.int32),
                                                         jnp.where(inside, local, 0)])
                            act_v[gs, t, picks] = val if q == 0 else jnp.where(inside, val, act_v[gs, t, picks])
                out_copy(gs, g).start()
            return 0

        lax.fori_loop(0, gpw // 2, pair_body, 0)
        for gs in range(2):
            out_copy(gs, g0 + gpw - 2 + gs).wait()

    return k(dense, idx)


def _gelu_gate_kernel(a_ref, g_ref, o_ref):
    a = a_ref[...]
    o_ref[...] = 0.5 * a * (1.0 + lax.erf(a * (2.0 ** -0.5))) * g_ref[...]


def _ln2_kernel(h_ref, f_ref, g_ref, b_ref, full_ref, o_ref):
    del full_ref
    o_ref[...] = _layer_norm(ALPHA * h_ref[...] + f_ref[...], g_ref[...], b_ref[...])


MIX_TM = 512
_MIX_PARAMS = pltpu.CompilerParams(dimension_semantics=("arbitrary",))


def _mix_row(width):
    return pl.BlockSpec((MIX_TM, width), lambda i: (i, 0))


def _gelu_gate_call(act, gates):
    T = act.shape[0]
    return pl.pallas_call(
        _gelu_gate_kernel, grid=(T // MIX_TM,), in_specs=[_mix_row(PEER_PICKS)] * 2, out_specs=_mix_row(PEER_PICKS),
        out_shape=jax.ShapeDtypeStruct((T, PEER_PICKS), F32), compiler_params=_MIX_PARAMS, name="peer_gelu_gate",
    )(act, gates)


def _mix_finish(x1, idx, w, v_packed, ln_gain, ln_bias, result, row0):
    T = x1.shape[0]
    const = pl.BlockSpec((1, D_MODEL), lambda i: (0, 0))
    blk0 = row0 // MIX_TM
    ffn = _sc_acc(v_packed, idx, w)
    result = pl.pallas_call(
        _ln2_kernel, grid=(T // MIX_TM,),
        in_specs=[_mix_row(D_MODEL)] * 2 + [const, const, pl.BlockSpec(memory_space=pl.ANY)],
        out_specs=pl.BlockSpec((MIX_TM, D_MODEL), lambda i: (blk0 + i, 0)),
        out_shape=jax.ShapeDtypeStruct(result.shape, F32), input_output_aliases={4: 0},
        compiler_params=_MIX_PARAMS, name="peer_ln2",
    )(x1, ffn, ln_gain.reshape(1, D_MODEL), ln_bias.reshape(1, D_MODEL), result)
    return result, ffn


BATCH_CHUNKS = 16
SC_GATE_LAG = 1
SC_PASS_LAG = 3


def kernel(x, w_in, fox_f_bias, gla_gate_up, gla_gate_bias, gla_norm_gain, w_out_fox, w_out_gla, w_out,
           ln1_gain, ln1_bias, peer_w_query, peer_sub_keys, peer_expert_u, peer_expert_v, ln2_gain, ln2_bias):
    B, S, D = x.shape
    assert D == D_MODEL and S % max(ROUTE_TM, FOX_TQ) == 0
    assert DEPTH == 1 and w_in.shape[0] == 1, "the chunk pipeline below is written for the single-layer block"
    u_bf16, v_packed = peer_expert_u[0].astype(BF16), _pack_table(peer_expert_v[0])
    n_chunks = BATCH_CHUNKS if B % BATCH_CHUNKS == 0 else 1
    bc = B // n_chunks
    ffns, ws = [], []
    result = jnp.zeros((B * S, D), F32)
    for ci in range(n_chunks):
        h = x[ci * bc:(ci + 1) * bc].reshape(bc * S, D)
        fqkv, gqk, gv, gg, mg, small = _in_proj(h, w_in[0])
        c_t = _fox_gate(small, fox_f_bias[0], bc, S)
        attn = _fox_attn(fqkv, c_t, bc, S)
        og = _gla(gqk, gv, small, gg, gla_gate_up[0], gla_gate_bias[0], gla_norm_gain[0], bc, S)
        x1 = _merge(attn, og, mg, h, w_out_fox[0], w_out_gla[0], w_out[0], ln1_gain[0], ln1_bias[0])
        zero = jnp.zeros((SUBLANES, LANES), F32)
        after = ws[ci - SC_GATE_LAG][:SUBLANES, :LANES] if ci >= SC_GATE_LAG else zero
        after_b = ffns[ci - SC_PASS_LAG][:SUBLANES, :LANES] if ci >= SC_PASS_LAG else zero
        idx, gates = _route(x1, peer_w_query[0], peer_sub_keys[0], after, after_b)
        act = _sc_pick(_dense_act(x1, u_bf16), idx)
        w = _gelu_gate_call(act, gates)
        result, ffn = _mix_finish(x1, idx, w, v_packed, ln2_gain[0], ln2_bias[0], result, ci * bc * S)
        ffns.append(ffn)
        ws.append(w)
    return result.reshape(B, S, D)
```

```python
import functools
import math

import jax
import jax.numpy as jnp
from jax import lax
from jax.experimental import pallas as pl
from jax.experimental.pallas import tpu as pltpu
from jax.experimental.pallas import tpu_sc as plsc

F32 = jnp.float32
BF16 = jnp.bfloat16

D_MODEL = 1024
FOX_HEADS = 8
FOX_HEAD_DIM = 64
FOX_WIDTH = FOX_HEADS * FOX_HEAD_DIM
GLA_HEADS = 4
GLA_KEY_DIM = 128
GLA_VAL_DIM = 256
GLA_QK_WIDTH = GLA_HEADS * GLA_KEY_DIM
GLA_V_WIDTH = GLA_HEADS * GLA_VAL_DIM
GLA_GATE_RANK = 16
GLA_GATE_TEMP = 16.0
GLA_NORM_EPS = 1e-5
GLA_CHUNK = 64
PEER_HEADS = 8
PEER_N_KEYS = 128
PEER_HALF = 128
PEER_TOPK = 16
PEER_PICKS = PEER_HEADS * PEER_TOPK
DEPTH = 1
ALPHA = (2.0 * DEPTH) ** 0.25
LN_EPS = 1e-5

LANES = 128
SUBLANES = 8
VMEM_LIMIT = 52 * 1024 * 1024

IN_SPLIT_SIZES = (FOX_WIDTH, FOX_WIDTH, FOX_WIDTH, FOX_HEADS,
                  GLA_QK_WIDTH, GLA_QK_WIDTH, GLA_V_WIDTH, GLA_V_WIDTH, GLA_GATE_RANK,
                  D_MODEL, D_MODEL)
FF_COL = 0
GLR_COL = FOX_HEADS


def _dot(a, b, **kw):
    return jnp.dot(a, b, preferred_element_type=F32, **kw)


def _dot_nt(a, b):
    return lax.dot_general(a, b, (((1,), (1,)), ((), ())), preferred_element_type=F32)


def _dot_tn(a, b):
    return lax.dot_general(a, b, (((0,), (0,)), ((), ())), preferred_element_type=F32)


def _layer_norm(y, gain, bias):
    mu = jnp.mean(y, axis=-1, keepdims=True)
    yc = y - mu
    var = jnp.mean(yc * yc, axis=-1, keepdims=True)
    return yc * lax.rsqrt(var + LN_EPS) * gain + bias


def _in_proj_kernel(x_ref, wf_ref, wgqk_ref, wgv_ref, wgg_ref, wm_ref, ws_ref,
                    f_ref, gqk_ref, gv_ref, gg_ref, m_ref, s_ref):
    xb = x_ref[...].astype(BF16)
    f_ref[...] = _dot(xb, wf_ref[...]).astype(BF16)
    gqk_ref[...] = _dot(xb, wgqk_ref[...]).astype(BF16)
    gv_ref[...] = _dot(xb, wgv_ref[...]).astype(BF16)
    gg_ref[...] = _dot(xb, wgg_ref[...])
    m_ref[...] = _dot(xb, wm_ref[...])
    s_ref[...] = _dot(xb, ws_ref[...])


def _in_proj(x2, w_in):
    T = x2.shape[0]
    tm = 256
    pts = [0]
    for s in IN_SPLIT_SIZES:
        pts.append(pts[-1] + s)
    col = lambda i, j: w_in[:, pts[i]:pts[j]]
    wf = col(0, 3).astype(BF16)
    wgqk = col(4, 6).astype(BF16)
    wgv = col(6, 7).astype(BF16)
    wgg = col(7, 8).astype(BF16)
    wm = col(9, 11).astype(BF16)
    ws = jnp.concatenate([col(3, 4), col(8, 9)], axis=1)
    ws = jnp.pad(ws, ((0, 0), (0, LANES - ws.shape[1]))).astype(BF16)
    ws_list = [wf, wgqk, wgv, wgg, wm, ws]
    out_dtypes = [BF16, BF16, BF16, F32, F32, F32]
    const = lambda w: pl.BlockSpec(w.shape, lambda i: (0, 0))
    return pl.pallas_call(
        _in_proj_kernel,
        grid=(T // tm,),
        in_specs=[pl.BlockSpec((tm, D_MODEL), lambda i: (i, 0))] + [const(w) for w in ws_list],
        out_specs=[pl.BlockSpec((tm, w.shape[1]), lambda i: (i, 0)) for w in ws_list],
        out_shape=[jax.ShapeDtypeStruct((T, w.shape[1]), dt) for w, dt in zip(ws_list, out_dtypes)],
        compiler_params=pltpu.CompilerParams(dimension_semantics=("arbitrary",), vmem_limit_bytes=VMEM_LIMIT),
        name="in_proj",
    )(x2, *ws_list)


def _fox_gate_kernel(s_ref, bias_ref, c_ref):
    S = s_ref.shape[0]
    ff_t = s_ref[...].T[FF_COL:FF_COL + FOX_HEADS, :]
    log_f = jax.nn.log_sigmoid(ff_t + bias_ref[...])
    r = lax.broadcasted_iota(jnp.int32, (LANES, LANES), 0)
    c = lax.broadcasted_iota(jnp.int32, (LANES, LANES), 1)
    tri = (r <= c).astype(F32)
    carry = jnp.zeros((FOX_HEADS, 1), F32)
    for j in range(S // LANES):
        blk = log_f[:, j * LANES:(j + 1) * LANES]
        cs = _dot(blk, tri, precision=lax.Precision.HIGHEST) + carry
        c_ref[0, :, j * LANES:(j + 1) * LANES] = cs
        carry = cs[:, LANES - 1:LANES]


def _fox_gate(small, fox_f_bias, B, S):
    return pl.pallas_call(
        _fox_gate_kernel,
        grid=(B,),
        in_specs=[pl.BlockSpec((S, LANES), lambda b: (b, 0)),
                  pl.BlockSpec((FOX_HEADS, 1), lambda b: (0, 0))],
        out_specs=pl.BlockSpec((1, FOX_HEADS, S), lambda b: (b, 0, 0)),
        out_shape=jax.ShapeDtypeStruct((B, FOX_HEADS, S), F32),
        compiler_params=pltpu.CompilerParams(dimension_semantics=("arbitrary",)),
        name="fox_gate",
    )(small, fox_f_bias.reshape(FOX_HEADS, 1))


FOX_TQ = 256
FOX_TK = FOX_TQ
FOX_COLS = 256


def _fox_attn_kernel(q_ref, k_ref, v_ref, c_ref, o_ref):
    qi = pl.program_id(2)
    tq, tk, dh = FOX_TQ, FOX_TK, FOX_HEAD_DIM
    n_h = FOX_COLS // dh
    qs = [q_ref[:, hh * dh:(hh + 1) * dh] * (dh ** -0.5) for hh in range(n_h)]

    def step(j, carry, masked):
        ks = pl.ds(pl.multiple_of(j * tk, tk), tk)
        k2 = k_ref[ks, :]
        v2 = v_ref[ks, :]
        out = []
        for hh in range(n_h):
            m, l, acc = carry[hh]
            s = _dot_nt(qs[hh], k2[:, hh * dh:(hh + 1) * dh]) - c_ref[hh, :, ks]
            if masked:
                r = lax.broadcasted_iota(jnp.int32, (tq, tk), 0)
                c = lax.broadcasted_iota(jnp.int32, (tq, tk), 1)
                s = jnp.where(c <= r, s, -jnp.inf)
            m_new = jnp.maximum(m, jnp.max(s, axis=1, keepdims=True))
            p = jnp.exp(s - m_new)
            a = jnp.exp(m - m_new)
            l = a * l + jnp.sum(p, axis=1, keepdims=True)
            acc = a * acc + _dot(p.astype(BF16), v2[:, hh * dh:(hh + 1) * dh])
            out.append((m_new, l, acc))
        return tuple(out)

    init = tuple((jnp.full((tq, 1), -jnp.inf, F32), jnp.zeros((tq, 1), F32), jnp.zeros((tq, dh), F32))
                 for _ in range(n_h))
    carry = lax.fori_loop(0, qi, lambda j, c: step(j, c, False), init)
    carry = step(qi, carry, True)
    o_ref[...] = jnp.concatenate([acc / l for _, l, acc in carry], axis=1).astype(BF16)


def _fox_attn(fqkv, c_t, B, S):
    T = B * S
    nq = S // FOX_TQ
    n_hp = FOX_WIDTH // FOX_COLS
    return pl.pallas_call(
        _fox_attn_kernel,
        grid=(B, n_hp, nq),
        in_specs=[pl.BlockSpec((FOX_TQ, FOX_COLS), lambda b, h, i: (b * nq + i, h)),
                  pl.BlockSpec((S, FOX_COLS), lambda b, h, i: (b, n_hp + h)),
                  pl.BlockSpec((S, FOX_COLS), lambda b, h, i: (b, 2 * n_hp + h)),
                  pl.BlockSpec((FOX_COLS // FOX_HEAD_DIM, 1, S), lambda b, h, i: (b * n_hp + h, 0, 0))],
        out_specs=pl.BlockSpec((FOX_TQ, FOX_COLS), lambda b, h, i: (b * nq + i, h)),
        out_shape=jax.ShapeDtypeStruct((T, FOX_WIDTH), BF16),
        compiler_params=pltpu.CompilerParams(dimension_semantics=("arbitrary", "arbitrary", "arbitrary")),
        name="fox_attn",
    )(fqkv, fqkv, fqkv, c_t.reshape(B * FOX_HEADS, 1, S))


GLA_GROUP = 4


def _gla_kernel(q_ref, k_ref, v_ref, s_ref, up_ref, gb_ref, gg_ref, gain_ref, o_ref):
    S = q_ref.shape[0]
    C = GLA_CHUNK
    dk, dv = GLA_KEY_DIM, GLA_VAL_DIM
    r = lax.broadcasted_iota(jnp.int32, (C, C), 0)
    c = lax.broadcasted_iota(jnp.int32, (C, C), 1)
    tril = (r >= c).astype(F32)

    def body(ci, states):
        rows = pl.ds(pl.multiple_of(ci * C, C), C)
        z = _dot(s_ref[rows, :].astype(BF16), up_ref[...]) + gb_ref[...]
        la = jax.nn.log_sigmoid(z) * (1.0 / GLA_GATE_TEMP)
        cum = _dot(tril, la, precision=lax.Precision.HIGHEST)
        tot = cum[C - 1:C, :]
        kd_all = (k_ref[rows, :].astype(F32) * jnp.exp(tot - cum)).astype(BF16)
        decay = jnp.exp(tot)
        new_states = []
        for g, st_t in enumerate(states):
            kc, vc = slice(g * dk, (g + 1) * dk), slice(g * dv, (g + 1) * dv)
            st_t = st_t * decay[:, kc] + _dot_tn(v_ref[rows, vc], kd_all[:, kc])
            o = _dot_nt(q_ref[rows, kc], st_t.astype(BF16)) * (dk ** -0.5)
            o = o * lax.rsqrt(jnp.mean(o * o, axis=-1, keepdims=True) + GLA_NORM_EPS) * gain_ref[:, vc]
            o = o * jax.nn.silu(gg_ref[rows, vc])
            o_ref[rows, vc] = o.astype(BF16)
            new_states.append(st_t)
        return tuple(new_states)

    lax.fori_loop(0, S // C, body, tuple(jnp.zeros((dv, dk), F32) for _ in range(GLA_GROUP)), unroll=4)


def _gla(gqk, gv, small, gg, gla_gate_up, gla_gate_bias, gla_norm_gain, B, S):
    T = B * S
    up = jnp.zeros((LANES, GLA_QK_WIDTH), F32).at[GLR_COL:GLR_COL + GLA_GATE_RANK].set(gla_gate_up).astype(BF16)
    gb = gla_gate_bias.reshape(1, GLA_QK_WIDTH)
    gain = gla_norm_gain.reshape(1, GLA_V_WIDTH)
    n_groups = GLA_HEADS // GLA_GROUP
    kw, vw = GLA_GROUP * GLA_KEY_DIM, GLA_GROUP * GLA_VAL_DIM
    return pl.pallas_call(
        _gla_kernel,
        grid=(B, n_groups),
        in_specs=[pl.BlockSpec((S, kw), lambda b, h: (b, h)),
                  pl.BlockSpec((S, kw), lambda b, h: (b, n_groups + h)),
                  pl.BlockSpec((S, vw), lambda b, h: (b, h)),
                  pl.BlockSpec((S, LANES), lambda b, h: (b, 0)),
                  pl.BlockSpec((LANES, kw), lambda b, h: (0, h)),
                  pl.BlockSpec((1, kw), lambda b, h: (0, h)),
                  pl.BlockSpec((S, vw), lambda b, h: (b, h)),
                  pl.BlockSpec((1, vw), lambda b, h: (0, h))],
        out_specs=pl.BlockSpec((S, vw), lambda b, h: (b, h)),
        out_shape=jax.ShapeDtypeStruct((T, GLA_V_WIDTH), BF16),
        compiler_params=pltpu.CompilerParams(dimension_semantics=("arbitrary", "arbitrary")),
        name="gla",
    )(gqk, gqk, gv, small, up, gb, gg, gain)


def _merge_kernel(a_ref, og_ref, m_ref, x_ref, wf_ref, wg_ref, wo_ref, g_ref, b_ref, o_ref, ob_ref):
    y_fox = _dot(a_ref[...], wf_ref[...])
    y_gla = _dot(og_ref[...], wg_ref[...])
    merged = (jax.nn.sigmoid(m_ref[:, :D_MODEL]) * y_fox + jax.nn.sigmoid(m_ref[:, D_MODEL:]) * y_gla)
    mix = _dot(merged.astype(BF16), wo_ref[...])
    x1 = _layer_norm(ALPHA * x_ref[...] + mix, g_ref[...], b_ref[...])
    o_ref[...] = x1
    ob_ref[...] = x1.astype(BF16)


def _merge(attn, og, mg, x2, w_out_fox, w_out_gla, w_out, ln_gain, ln_bias):
    T = x2.shape[0]
    tm = 512
    row = lambda w: pl.BlockSpec((tm, w), lambda i: (i, 0))
    const = lambda a: pl.BlockSpec(a.shape, lambda i: (0, 0))
    ws = [w_out_fox.astype(BF16), w_out_gla.astype(BF16), w_out.astype(BF16),
          ln_gain.reshape(1, D_MODEL), ln_bias.reshape(1, D_MODEL)]
    return pl.pallas_call(
        _merge_kernel,
        grid=(T // tm,),
        in_specs=[row(FOX_WIDTH), row(GLA_V_WIDTH), row(2 * D_MODEL), row(D_MODEL)] + [const(w) for w in ws],
        out_specs=[row(D_MODEL), row(D_MODEL)],
        out_shape=[jax.ShapeDtypeStruct((T, D_MODEL), F32), jax.ShapeDtypeStruct((T, D_MODEL), BF16)],
        compiler_params=pltpu.CompilerParams(dimension_semantics=("arbitrary",), vmem_limit_bytes=VMEM_LIMIT),
        name="merge",
    )(attn, og, mg, x2, *ws)


ROUTE_TM = 256
CAND_BLOCKS = 10


def _sublane_all(x, op):
    for shift in (4, 2, 1):
        x = op(x, pltpu.roll(x, shift, axis=0))
    return x


def _take_max(s3, iota3, sentinel):
    m8 = _sublane_all(jnp.max(s3, axis=0), jnp.maximum)
    idx8 = _sublane_all(jnp.min(jnp.where(s3 == m8[None], iota3, sentinel), axis=0), jnp.minimum)
    return m8, idx8, iota3 == idx8[None]


def _row_iota(groups, n):
    shape = (groups, SUBLANES, n)
    return lax.broadcasted_iota(jnp.int32, shape, 0) * SUBLANES + lax.broadcasted_iota(jnp.int32, shape, 1)


def _route_kernel(x_ref, wq_ref, keys_ref, after_a_ref, after_b_ref, idx_ref, gate_ref,
                  q_scr, st_scr, it_scr, best_scr, pick_scr, gsel_scr):
    del after_a_ref, after_b_ref
    tm = ROUTE_TM
    K = PEER_TOPK
    q_scr[...] = _dot(x_ref[...], wq_ref[...])
    key_iota = _row_iota(PEER_N_KEYS // SUBLANES, tm)

    def stage1(pair, _):
        hps = (2 * pair, 2 * pair + 1)
        ss = []
        for hp in hps:
            q = q_scr[:, pl.ds(pl.multiple_of(hp * PEER_HALF, PEER_HALF), PEER_HALF)].astype(BF16)
            ss.append(_dot_nt(keys_ref[hp], q).reshape(PEER_N_KEYS // SUBLANES, SUBLANES, tm))
        for i in range(K):
            for n, hp in enumerate(hps):
                m8, idx8, hit = _take_max(ss[n], key_iota, PEER_N_KEYS)
                st_scr[hp, i:i + 1, :] = m8[0:1]
                it_scr[hp, i:i + 1, :] = idx8[0:1]
                ss[n] = jnp.where(hit, -jnp.inf, ss[n])
        return 0

    lax.fori_loop(0, PEER_HEADS, stage1, 0)

    cand_iota = _row_iota(CAND_BLOCKS, tm)

    def stage2(h, _):
        s0, s1 = st_scr[2 * h], st_scr[2 * h + 1]
        i0, i1 = it_scr[2 * h], it_scr[2 * h + 1]
        lo, hi = slice(0, SUBLANES), slice(SUBLANES, 2 * SUBLANES)
        cs = [s0[0:1] + s1[lo], s0[0:1] + s1[hi]]
        ci = [i0[0:1] * PEER_N_KEYS + i1[lo], i0[0:1] * PEER_N_KEYS + i1[hi]]
        for a in range(1, SUBLANES):
            cs.append(s0[a:a + 1] + s1[lo])
            ci.append(i0[a:a + 1] * PEER_N_KEYS + i1[lo])
        cs.append(s0[hi] + s1[0:1])
        ci.append(i0[hi] * PEER_N_KEYS + i1[0:1])
        cand = jnp.stack(cs, axis=0)
        cidx = jnp.stack(ci, axis=0)
        for i in range(K):
            m8, _, hit = _take_max(cand, cand_iota, CAND_BLOCKS * SUBLANES)
            pick8 = _sublane_all(jnp.max(jnp.where(hit, cidx, -1), axis=0), jnp.maximum)
            best_scr[i:i + 1, :] = m8[0:1]
            pick_scr[pl.ds(h * K + i, 1), :] = pick8[0:1]
            cand = jnp.where(hit, -jnp.inf, cand)
        best = best_scr[...]
        e = jnp.exp(best - best[0:1])
        gsel_scr[pl.ds(pl.multiple_of(h * K, K), K), :] = e / jnp.sum(e, axis=0, keepdims=True)
        return 0

    lax.fori_loop(0, PEER_HEADS, stage2, 0, unroll=2)
    idx_ref[...] = pick_scr[...].T
    gate_ref[...] = gsel_scr[...].T


def _route(x1, peer_w_query, peer_sub_keys, after, after_b):
    T = x1.shape[0]
    tm = ROUTE_TM
    wq = peer_w_query.reshape(D_MODEL, 2 * PEER_HEADS * PEER_HALF).astype(BF16)
    keys = peer_sub_keys.reshape(2 * PEER_HEADS, PEER_N_KEYS, PEER_HALF).astype(BF16)
    picks = pl.BlockSpec((tm, PEER_PICKS), lambda i: (i, 0))
    return pl.pallas_call(
        _route_kernel,
        grid=(T // tm,),
        in_specs=[pl.BlockSpec((tm, D_MODEL), lambda i: (i, 0)),
                  pl.BlockSpec(wq.shape, lambda i: (0, 0)),
                  pl.BlockSpec(keys.shape, lambda i: (0, 0, 0)),
                  pl.BlockSpec((SUBLANES, LANES), lambda i: (0, 0)),
                  pl.BlockSpec((SUBLANES, LANES), lambda i: (0, 0))],
        out_specs=[picks, picks],
        out_shape=[jax.ShapeDtypeStruct((T, PEER_PICKS), jnp.int32),
                   jax.ShapeDtypeStruct((T, PEER_PICKS), F32)],
        scratch_shapes=[pltpu.VMEM((tm, 2 * PEER_HEADS * PEER_HALF), F32),
                        pltpu.VMEM((2 * PEER_HEADS, PEER_TOPK, tm), F32),
                        pltpu.VMEM((2 * PEER_HEADS, PEER_TOPK, tm), jnp.int32),
                        pltpu.VMEM((PEER_TOPK, tm), F32),
                        pltpu.VMEM((PEER_PICKS, tm), jnp.int32),
                        pltpu.VMEM((PEER_PICKS, tm), F32)],
        compiler_params=pltpu.CompilerParams(dimension_semantics=("arbitrary",), vmem_limit_bytes=VMEM_LIMIT),
        name="peer_route",
    )(x1, wq, keys, after, after_b)


SC_LANES = 16
SC_ROWS = 64
SC_CHUNKS = PEER_PICKS // SC_ROWS
SC_DCOLS = 256
SC_DVREGS = SC_DCOLS // SC_LANES


def _sc_mesh():
    info = plsc.get_sparse_core_info()
    mesh = plsc.VectorSubcoreMesh(core_axis_name="c", subcore_axis_name="s")
    return mesh, info.num_cores, info.num_cores * info.num_subcores


def _sc_token_pipeline(tab_hbm, idx_hbm, vec_hbm, out_hbm, idx_v, vec_v, rows_v, out_v,
                       sem_rows, sem_tok, sem_out, n_cores, tpw, compute_chunk):
    wid = lax.axis_index("s") * n_cores + lax.axis_index("c")
    base = wid * tpw

    def gather(s, c):
        return pltpu.make_async_copy(tab_hbm.at[idx_v.at[s, pl.ds(c * SC_ROWS, SC_ROWS)]],
                                     rows_v.at[c % 2], sem_rows.at[c % 2])

    def tok_fetch(s, tok):
        return (pltpu.make_async_copy(idx_hbm.at[tok], idx_v.at[s], sem_tok.at[0]),
                pltpu.make_async_copy(vec_hbm.at[tok], vec_v.at[s], sem_tok.at[1]))

    def out_copy(s, tok):
        return pltpu.make_async_copy(out_v.at[s], out_hbm.at[tok], sem_out.at[s])

    for d in tok_fetch(0, base):
        d.start()
    for d in tok_fetch(0, base):
        d.wait()
    gather(0, 0).start()

    def pair_body(tp, _):
        for s in range(2):
            t = tp * 2 + s
            tok = base + t
            has_next = t + 1 < tpw

            @pl.when(has_next)
            def _():
                for d in tok_fetch(1 - s, tok + 1):
                    d.start()

            @pl.when(t >= 2)
            def _():
                out_copy(s, tok - 2).wait()

            for c in range(SC_CHUNKS):
                if c + 1 < SC_CHUNKS:
                    gather(s, c + 1).start()
                else:
                    @pl.when(has_next)
                    def _():
                        for d in tok_fetch(1 - s, tok + 1):
                            d.wait()
                        gather(1 - s, 0).start()
                gather(s, c).wait()
                compute_chunk(s, c)
            out_copy(s, tok).start()
        return 0

    lax.fori_loop(0, tpw // 2, pair_body, 0)
    for s in range(2):
        out_copy(s, base + tpw - 2 + s).wait()


def _sc_scratch(vec_len, out_len):
    return [pltpu.VMEM((2, PEER_PICKS), jnp.int32),
            pltpu.VMEM((2, vec_len), F32),
            pltpu.VMEM((2, SC_ROWS, D_MODEL // 2), jnp.int32),
            pltpu.VMEM((2, out_len), F32),
            pltpu.SemaphoreType.DMA((2,)),
            pltpu.SemaphoreType.DMA((2,)),
            pltpu.SemaphoreType.DMA((2,))]


def _pack_table(tab):
    n, d = tab.shape
    bits = lax.bitcast_convert_type(tab.astype(BF16), jnp.uint16).astype(jnp.uint32).reshape(n, d // 32, 2, SC_LANES)
    words = bits[:, :, 0, :] | (bits[:, :, 1, :] << 16)
    return lax.bitcast_convert_type(words, jnp.int32).reshape(n, d // 2)


def _unpack_pairs(x):
    return list(plsc.unpack(x, format=plsc.PackFormat.INTERLEAVED))


def _sc_acc(v_packed, idx, w):
    T = w.shape[0]
    mesh, n_cores, n_workers = _sc_mesh()
    tpw = T // n_workers
    L = SC_LANES

    @functools.partial(
        pl.kernel, mesh=mesh, out_type=jax.ShapeDtypeStruct((T, D_MODEL), F32),
        scratch_types=_sc_scratch(PEER_PICKS, D_MODEL),
        compiler_params=pltpu.CompilerParams(needs_layout_passes=False), name="peer_sc_acc")
    def k(v_hbm, idx_hbm, w_hbm, out_hbm, idx_v, w_v, rows_v, out_v, sem_rows, sem_tok, sem_out):
        def compute_chunk(s, c):
            b = c % 2
            for dc in range(D_MODEL // SC_DCOLS):
                cols = [pl.ds(dc * SC_DCOLS + j * L, L) for j in range(SC_DVREGS)]
                if c == 0:
                    accs = tuple(jnp.zeros((L,), F32) for _ in cols)
                else:
                    accs = tuple(out_v[s, cs] for cs in cols)

                def row_body(r, accs, dc=dc):
                    wb = plsc.load_gather(w_v.at[s], [jnp.full((L,), c * SC_ROWS, jnp.int32) + r])
                    new = []
                    for j in range(SC_DVREGS // 2):
                        words = rows_v[b, r, pl.ds(dc * (SC_DCOLS // 2) + j * L, L)]
                        lo, hi = _unpack_pairs(plsc.bitcast(words, BF16))
                        new += [accs[2 * j] + wb * lo, accs[2 * j + 1] + wb * hi]
                    return tuple(new)

                accs = lax.fori_loop(0, SC_ROWS, row_body, accs)
                for a, cs in zip(accs, cols):
                    out_v[s, cs] = a

        _sc_token_pipeline(v_hbm, idx_hbm, w_hbm, out_hbm, idx_v, w_v, rows_v, out_v,
                           sem_rows, sem_tok, sem_out, n_cores, tpw, compute_chunk)

    return k(v_packed, idx, w)


DENSE_TM = 2048
DENSE_TN = 1024


def _dense_act_kernel(x_ref, u_ref, o_ref):
    o_ref[...] = _dot_nt(x_ref[...], u_ref[...])


def _dense_act(x1, u_bf16):
    T, N = x1.shape[0], u_bf16.shape[0]
    return pl.pallas_call(
        _dense_act_kernel, grid=(T // DENSE_TM, N // DENSE_TN),
        in_specs=[pl.BlockSpec((DENSE_TM, D_MODEL), lambda i, j: (i, 0)),
                  pl.BlockSpec((DENSE_TN, D_MODEL), lambda i, j: (j, 0))],
        out_specs=pl.BlockSpec((DENSE_TM, DENSE_TN), lambda i, j: (i, j)),
        out_shape=jax.ShapeDtypeStruct((T, N), F32),
        compiler_params=pltpu.CompilerParams(dimension_semantics=("arbitrary", "arbitrary"),
                                             vmem_limit_bytes=VMEM_LIMIT),
        name="peer_dense_act",
    )(x1, u_bf16)


PICK_TG = SUBLANES
PICK_NQ = 4096


def _sc_pick(dense, idx):
    T, N = dense.shape
    mesh, n_cores, n_workers = _sc_mesh()
    gpw = T // PICK_TG // n_workers
    n_q = N // PICK_NQ
    L = SC_LANES

    @functools.partial(
        pl.kernel, mesh=mesh, out_type=jax.ShapeDtypeStruct((T, PEER_PICKS), F32),
        scratch_types=[pltpu.VMEM((2, PICK_TG, PEER_PICKS), jnp.int32), pltpu.VMEM((2, PICK_TG, PICK_NQ), F32),
                       pltpu.VMEM((2, PICK_TG, PEER_PICKS), F32),
                       pltpu.SemaphoreType.DMA((2,)), pltpu.SemaphoreType.DMA((2,)), pltpu.SemaphoreType.DMA((2,))],
        compiler_params=pltpu.CompilerParams(needs_layout_passes=False), name="peer_sc_pick")
    def k(d_hbm, idx_hbm, act_hbm, idx_v, buf, act_v, sem_idx, sem_buf, sem_out):
        g0 = (lax.axis_index("s") * n_cores + lax.axis_index("c")) * gpw

        def idx_fetch(gs, g):
            return pltpu.make_async_copy(idx_hbm.at[pl.ds(g * PICK_TG, PICK_TG)], idx_v.at[gs], sem_idx.at[gs])

        def blk_fetch(g, q):
            return pltpu.make_async_copy(d_hbm.at[pl.ds(g * PICK_TG, PICK_TG), pl.ds(q * PICK_NQ, PICK_NQ)],
                                         buf.at[q % 2], sem_buf.at[q % 2])

        def out_copy(gs, g):
            return pltpu.make_async_copy(act_v.at[gs], act_hbm.at[pl.ds(g * PICK_TG, PICK_TG)], sem_out.at[gs])

        idx_fetch(0, g0).start()
        blk_fetch(g0, 0).start()

        def pair_body(gp, _):
            for gs in range(2):
                gi = gp * 2 + gs
                g = g0 + gi
                has_next = gi + 1 < gpw

                @pl.when(has_next)
                def _():
                    idx_fetch(1 - gs, g + 1).start()

                idx_fetch(gs, g).wait()

                @pl.when(gi >= 2)
                def _():
                    out_copy(gs, g - 2).wait()

                for q in range(n_q):
                    if q + 1 < n_q:
                        blk_fetch(g, q + 1).start()
                    else:
                        @pl.when(has_next)
                        def _():
                            blk_fetch(g + 1, 0).start()
                    blk_fetch(g, q).wait()
                    for t in range(PICK_TG):
                        for v in range(PEER_PICKS // L):
                            picks = pl.ds(v * L, L)
                            local = idx_v[gs, t, picks] - q * PICK_NQ
                            inside = (local >= 0) & (local < PICK_NQ)
                            val = plsc.load_gather(buf, [jnp.full((L,), q % 2, jnp.int32), jnp.full((L,), t, jnp.int32),
                                                         jnp.where(inside, local, 0)])
                            act_v[gs, t, picks] = val if q == 0 else jnp.where(inside, val, act_v[gs, t, picks])
                out_copy(gs, g).start()
            return 0

        lax.fori_loop(0, gpw // 2, pair_body, 0)
        for gs in range(2):
            out_copy(gs, g0 + gpw - 2 + gs).wait()

    return k(dense, idx)


def _gelu_gate_kernel(a_ref, g_ref, o_ref):
    a = a_ref[...]
    o_ref[...] = 0.5 * a * (1.0 + lax.erf(a * (2.0 ** -0.5))) * g_ref[...]


def _ln2_kernel(h_ref, f_ref, g_ref, b_ref, full_ref, o_ref):
    del full_ref
    o_ref[...] = _layer_norm(ALPHA * h_ref[...] + f_ref[...], g_ref[...], b_ref[...])


MIX_TM = 512
_MIX_PARAMS = pltpu.CompilerParams(dimension_semantics=("arbitrary",))


def _mix_row(width):
    return pl.BlockSpec((MIX_TM, width), lambda i: (i, 0))


def _gelu_gate_call(act, gates):
    T = act.shape[0]
    return pl.pallas_call(
        _gelu_gate_kernel, grid=(T // MIX_TM,), in_specs=[_mix_row(PEER_PICKS)] * 2, out_specs=_mix_row(PEER_PICKS),
        out_shape=jax.ShapeDtypeStruct((T, PEER_PICKS), F32), compiler_params=_MIX_PARAMS, name="peer_gelu_gate",
    )(act, gates)


def _mix_finish(x1, idx, w, v_packed, ln_gain, ln_bias, result, row0):
    T = x1.shape[0]
    const = pl.BlockSpec((1, D_MODEL), lambda i: (0, 0))
    blk0 = row0 // MIX_TM
    ffn = _sc_acc(v_packed, idx, w)
    result = pl.pallas_call(
        _ln2_kernel, grid=(T // MIX_TM,),
        in_specs=[_mix_row(D_MODEL)] * 2 + [const, const, pl.BlockSpec(memory_space=pl.ANY)],
        out_specs=pl.BlockSpec((MIX_TM, D_MODEL), lambda i: (blk0 + i, 0)),
        out_shape=jax.ShapeDtypeStruct(result.shape, F32), input_output_aliases={4: 0},
        compiler_params=_MIX_PARAMS, name="peer_ln2",
    )(x1, ffn, ln_gain.reshape(1, D_MODEL), ln_bias.reshape(1, D_MODEL), result)
    return result, ffn


BATCH_CHUNKS = 16
SC_GATE_LAG = 1
SC_PASS_LAG = 3


def kernel(x, w_in, fox_f_bias, gla_gate_up, gla_gate_bias, gla_norm_gain, w_out_fox, w_out_gla, w_out,
           ln1_gain, ln1_bias, peer_w_query, peer_sub_keys, peer_expert_u, peer_expert_v, ln2_gain, ln2_bias):
    B, S, D = x.shape
    assert D == D_MODEL and S % max(ROUTE_TM, FOX_TQ) == 0
    assert DEPTH == 1 and w_in.shape[0] == 1, "the chunk pipeline below is written for the single-layer block"
    u_bf16, v_packed = peer_expert_u[0].astype(BF16), _pack_table(peer_expert_v[0])
    n_chunks = BATCH_CHUNKS if B % BATCH_CHUNKS == 0 else 1
    bc = B // n_chunks
    ffns, ws = [], []
    result = jnp.zeros((B * S, D), F32)
    for ci in range(n_chunks):
        h = x[ci * bc:(ci + 1) * bc].reshape(bc * S, D)
        fqkv, gqk, gv, gg, mg, small = _in_proj(h, w_in[0])
        c_t = _fox_gate(small, fox_f_bias[0], bc, S)
        attn = _fox_attn(fqkv, c_t, bc, S)
        og = _gla(gqk, gv, small, gg, gla_gate_up[0], gla_gate_bias[0], gla_norm_gain[0], bc, S)
        x1, x1b = _merge(attn, og, mg, h, w_out_fox[0], w_out_gla[0], w_out[0], ln1_gain[0], ln1_bias[0])
        zero = jnp.zeros((SUBLANES, LANES), F32)
        after = ws[ci - SC_GATE_LAG][:SUBLANES, :LANES] if ci >= SC_GATE_LAG else zero
        after_b = ffns[ci - SC_PASS_LAG][:SUBLANES, :LANES] if ci >= SC_PASS_LAG else zero
        idx, gates = _route(x1b, peer_w_query[0], peer_sub_keys[0], after, after_b)
        act = _sc_pick(_dense_act(x1b, u_bf16), idx)
        w = _gelu_gate_call(act, gates)
        result, ffn = _mix_finish(x1, idx, w, v_packed, ln2_gain[0], ln2_bias[0], result, ci * bc * S)
        ffns.append(ffn)
        ws.append(w)
    return result.reshape(B, S, D)
```

```python
import functools
import math

import jax
import jax.numpy as jnp
from jax import lax
from jax.experimental import pallas as pl
from jax.experimental.pallas import tpu as pltpu
from jax.experimental.pallas import tpu_sc as plsc

F32 = jnp.float32
BF16 = jnp.bfloat16

D_MODEL = 1024
FOX_HEADS = 8
FOX_HEAD_DIM = 64
FOX_WIDTH = FOX_HEADS * FOX_HEAD_DIM
GLA_HEADS = 4
GLA_KEY_DIM = 128
GLA_VAL_DIM = 256
GLA_QK_WIDTH = GLA_HEADS * GLA_KEY_DIM
GLA_V_WIDTH = GLA_HEADS * GLA_VAL_DIM
GLA_GATE_RANK = 16
GLA_GATE_TEMP = 16.0
GLA_NORM_EPS = 1e-5
GLA_CHUNK = 64
PEER_HEADS = 8
PEER_N_KEYS = 128
PEER_HALF = 128
PEER_TOPK = 16
PEER_PICKS = PEER_HEADS * PEER_TOPK
DEPTH = 1
ALPHA = (2.0 * DEPTH) ** 0.25
LN_EPS = 1e-5

LANES = 128
SUBLANES = 8
VMEM_LIMIT = 52 * 1024 * 1024

IN_SPLIT_SIZES = (FOX_WIDTH, FOX_WIDTH, FOX_WIDTH, FOX_HEADS,
                  GLA_QK_WIDTH, GLA_QK_WIDTH, GLA_V_WIDTH, GLA_V_WIDTH, GLA_GATE_RANK,
                  D_MODEL, D_MODEL)
FF_COL = 0
GLR_COL = FOX_HEADS


def _dot(a, b, **kw):
    return jnp.dot(a, b, preferred_element_type=F32, **kw)


def _dot_nt(a, b):
    return lax.dot_general(a, b, (((1,), (1,)), ((), ())), preferred_element_type=F32)


def _dot_tn(a, b):
    return lax.dot_general(a, b, (((0,), (0,)), ((), ())), preferred_element_type=F32)


def _layer_norm(y, gain, bias):
    mu = jnp.mean(y, axis=-1, keepdims=True)
    yc = y - mu
    var = jnp.mean(yc * yc, axis=-1, keepdims=True)
    return yc * lax.rsqrt(var + LN_EPS) * gain + bias


def _in_proj_kernel(x_ref, wf_ref, wgqk_ref, wgv_ref, wgg_ref, wm_ref, ws_ref,
                    f_ref, gqk_ref, gv_ref, gg_ref, m_ref, s_ref):
    xb = x_ref[...].astype(BF16)
    f_ref[...] = _dot(xb, wf_ref[...]).astype(BF16)
    gqk_ref[...] = _dot(xb, wgqk_ref[...]).astype(BF16)
    gv_ref[...] = _dot(xb, wgv_ref[...]).astype(BF16)
    gg_ref[...] = _dot(xb, wgg_ref[...])
    m_ref[...] = _dot(xb, wm_ref[...])
    s_ref[...] = _dot(xb, ws_ref[...])


def _in_proj(x2, w_in):
    T = x2.shape[0]
    tm = 512
    pts = [0]
    for s in IN_SPLIT_SIZES:
        pts.append(pts[-1] + s)
    col = lambda i, j: w_in[:, pts[i]:pts[j]]
    wf = col(0, 3).astype(BF16)
    wgqk = col(4, 6).astype(BF16)
    wgv = col(6, 7).astype(BF16)
    wgg = col(7, 8).astype(BF16)
    wm = col(9, 11).astype(BF16)
    ws = jnp.concatenate([col(3, 4), col(8, 9)], axis=1)
    ws = jnp.pad(ws, ((0, 0), (0, LANES - ws.shape[1]))).astype(BF16)
    ws_list = [wf, wgqk, wgv, wgg, wm, ws]
    out_dtypes = [BF16, BF16, BF16, F32, F32, F32]
    const = lambda w: pl.BlockSpec(w.shape, lambda i: (0, 0), pipeline_mode=pl.Buffered(1))
    return pl.pallas_call(
        _in_proj_kernel,
        grid=(T // tm,),
        in_specs=[pl.BlockSpec((tm, D_MODEL), lambda i: (i, 0))] + [const(w) for w in ws_list],
        out_specs=[pl.BlockSpec((tm, w.shape[1]), lambda i: (i, 0)) for w in ws_list],
        out_shape=[jax.ShapeDtypeStruct((T, w.shape[1]), dt) for w, dt in zip(ws_list, out_dtypes)],
        compiler_params=pltpu.CompilerParams(dimension_semantics=("arbitrary",), vmem_limit_bytes=VMEM_LIMIT),
        name="in_proj",
    )(x2, *ws_list)


def _fox_gate_kernel(s_ref, bias_ref, c_ref):
    S = s_ref.shape[0]
    ff_t = s_ref[...].T[FF_COL:FF_COL + FOX_HEADS, :]
    log_f = jax.nn.log_sigmoid(ff_t + bias_ref[...])
    r = lax.broadcasted_iota(jnp.int32, (LANES, LANES), 0)
    c = lax.broadcasted_iota(jnp.int32, (LANES, LANES), 1)
    tri = (r <= c).astype(F32)
    carry = jnp.zeros((FOX_HEADS, 1), F32)
    for j in range(S // LANES):
        blk = log_f[:, j * LANES:(j + 1) * LANES]
        cs = _dot(blk, tri, precision=lax.Precision.HIGHEST) + carry
        c_ref[0, :, j * LANES:(j + 1) * LANES] = cs
        carry = cs[:, LANES - 1:LANES]


def _fox_gate(small, fox_f_bias, B, S):
    return pl.pallas_call(
        _fox_gate_kernel,
        grid=(B,),
        in_specs=[pl.BlockSpec((S, LANES), lambda b: (b, 0)),
                  pl.BlockSpec((FOX_HEADS, 1), lambda b: (0, 0))],
        out_specs=pl.BlockSpec((1, FOX_HEADS, S), lambda b: (b, 0, 0)),
        out_shape=jax.ShapeDtypeStruct((B, FOX_HEADS, S), F32),
        compiler_params=pltpu.CompilerParams(dimension_semantics=("arbitrary",)),
        name="fox_gate",
    )(small, fox_f_bias.reshape(FOX_HEADS, 1))


FOX_TQ = 256
FOX_TK = FOX_TQ
FOX_COLS = 256


def _fox_attn_kernel(q_ref, k_ref, v_ref, c_ref, o_ref):
    qi = pl.program_id(2)
    tq, tk, dh = FOX_TQ, FOX_TK, FOX_HEAD_DIM
    n_h = FOX_COLS // dh
    qs = [q_ref[:, hh * dh:(hh + 1) * dh] * (dh ** -0.5) for hh in range(n_h)]

    def step(j, carry, masked):
        ks = pl.ds(pl.multiple_of(j * tk, tk), tk)
        k2 = k_ref[ks, :]
        v2 = v_ref[ks, :]
        out = []
        for hh in range(n_h):
            m, l, acc = carry[hh]
            s = _dot_nt(qs[hh], k2[:, hh * dh:(hh + 1) * dh]) - c_ref[hh, :, ks]
            if masked:
                r = lax.broadcasted_iota(jnp.int32, (tq, tk), 0)
                c = lax.broadcasted_iota(jnp.int32, (tq, tk), 1)
                s = jnp.where(c <= r, s, -jnp.inf)
            m_new = jnp.maximum(m, jnp.max(s, axis=1, keepdims=True))
            p = jnp.exp(s - m_new)
            a = jnp.exp(m - m_new)
            l = a * l + jnp.sum(p, axis=1, keepdims=True)
            acc = a * acc + _dot(p.astype(BF16), v2[:, hh * dh:(hh + 1) * dh])
            out.append((m_new, l, acc))
        return tuple(out)

    init = tuple((jnp.full((tq, 1), -jnp.inf, F32), jnp.zeros((tq, 1), F32), jnp.zeros((tq, dh), F32))
                 for _ in range(n_h))
    carry = lax.fori_loop(0, qi, lambda j, c: step(j, c, False), init)
    carry = step(qi, carry, True)
    o_ref[...] = jnp.concatenate([acc / l for _, l, acc in carry], axis=1).astype(BF16)


def _fox_attn(fqkv, c_t, B, S):
    T = B * S
    nq = S // FOX_TQ
    n_hp = FOX_WIDTH // FOX_COLS
    return pl.pallas_call(
        _fox_attn_kernel,
        grid=(B, n_hp, nq),
        in_specs=[pl.BlockSpec((FOX_TQ, FOX_COLS), lambda b, h, i: (b * nq + i, h)),
                  pl.BlockSpec((S, FOX_COLS), lambda b, h, i: (b, n_hp + h)),
                  pl.BlockSpec((S, FOX_COLS), lambda b, h, i: (b, 2 * n_hp + h)),
                  pl.BlockSpec((FOX_COLS // FOX_HEAD_DIM, 1, S), lambda b, h, i: (b * n_hp + h, 0, 0))],
        out_specs=pl.BlockSpec((FOX_TQ, FOX_COLS), lambda b, h, i: (b * nq + i, h)),
        out_shape=jax.ShapeDtypeStruct((T, FOX_WIDTH), BF16),
        compiler_params=pltpu.CompilerParams(dimension_semantics=("arbitrary", "arbitrary", "arbitrary")),
        name="fox_attn",
    )(fqkv, fqkv, fqkv, c_t.reshape(B * FOX_HEADS, 1, S))


GLA_GROUP = 4


def _gla_kernel(q_ref, k_ref, v_ref, s_ref, up_ref, gb_ref, gg_ref, gain_ref, o_ref):
    S = q_ref.shape[0]
    C = GLA_CHUNK
    dk, dv = GLA_KEY_DIM, GLA_VAL_DIM
    r = lax.broadcasted_iota(jnp.int32, (C, C), 0)
    c = lax.broadcasted_iota(jnp.int32, (C, C), 1)
    tril = (r >= c).astype(F32)

    def body(ci, states):
        rows = pl.ds(pl.multiple_of(ci * C, C), C)
        z = _dot(s_ref[rows, :].astype(BF16), up_ref[...]) + gb_ref[...]
        la = jax.nn.log_sigmoid(z) * (1.0 / GLA_GATE_TEMP)
        cum = _dot(tril, la, precision=lax.Precision.HIGHEST)
        tot = cum[C - 1:C, :]
        kd_all = (k_ref[rows, :].astype(F32) * jnp.exp(tot - cum)).astype(BF16)
        decay = jnp.exp(tot)
        new_states = []
        for g, st_t in enumerate(states):
            kc, vc = slice(g * dk, (g + 1) * dk), slice(g * dv, (g + 1) * dv)
            st_t = st_t * decay[:, kc] + _dot_tn(v_ref[rows, vc], kd_all[:, kc])
            o = _dot_nt(q_ref[rows, kc], st_t.astype(BF16)) * (dk ** -0.5)
            o = o * lax.rsqrt(jnp.mean(o * o, axis=-1, keepdims=True) + GLA_NORM_EPS) * gain_ref[:, vc]
            o = o * jax.nn.silu(gg_ref[rows, vc])
            o_ref[rows, vc] = o.astype(BF16)
            new_states.append(st_t)
        return tuple(new_states)

    lax.fori_loop(0, S // C, body, tuple(jnp.zeros((dv, dk), F32) for _ in range(GLA_GROUP)), unroll=4)


def _gla(gqk, gv, small, gg, gla_gate_up, gla_gate_bias, gla_norm_gain, B, S):
    T = B * S
    up = jnp.zeros((LANES, GLA_QK_WIDTH), F32).at[GLR_COL:GLR_COL + GLA_GATE_RANK].set(gla_gate_up).astype(BF16)
    gb = gla_gate_bias.reshape(1, GLA_QK_WIDTH)
    gain = gla_norm_gain.reshape(1, GLA_V_WIDTH)
    n_groups = GLA_HEADS // GLA_GROUP
    kw, vw = GLA_GROUP * GLA_KEY_DIM, GLA_GROUP * GLA_VAL_DIM
    return pl.pallas_call(
        _gla_kernel,
        grid=(B, n_groups),
        in_specs=[pl.BlockSpec((S, kw), lambda b, h: (b, h)),
                  pl.BlockSpec((S, kw), lambda b, h: (b, n_groups + h)),
                  pl.BlockSpec((S, vw), lambda b, h: (b, h)),
                  pl.BlockSpec((S, LANES), lambda b, h: (b, 0)),
                  pl.BlockSpec((LANES, kw), lambda b, h: (0, h)),
                  pl.BlockSpec((1, kw), lambda b, h: (0, h)),
                  pl.BlockSpec((S, vw), lambda b, h: (b, h)),
                  pl.BlockSpec((1, vw), lambda b, h: (0, h))],
        out_specs=pl.BlockSpec((S, vw), lambda b, h: (b, h)),
        out_shape=jax.ShapeDtypeStruct((T, GLA_V_WIDTH), BF16),
        compiler_params=pltpu.CompilerParams(dimension_semantics=("arbitrary", "arbitrary")),
        name="gla",
    )(gqk, gqk, gv, small, up, gb, gg, gain)


def _merge_kernel(a_ref, og_ref, m_ref, x_ref, wf_ref, wg_ref, wo_ref, g_ref, b_ref, o_ref, ob_ref):
    y_fox = _dot(a_ref[...], wf_ref[...])
    y_gla = _dot(og_ref[...], wg_ref[...])
    merged = (jax.nn.sigmoid(m_ref[:, :D_MODEL]) * y_fox + jax.nn.sigmoid(m_ref[:, D_MODEL:]) * y_gla)
    mix = _dot(merged.astype(BF16), wo_ref[...])
    x1 = _layer_norm(ALPHA * x_ref[...] + mix, g_ref[...], b_ref[...])
    o_ref[...] = x1
    ob_ref[...] = x1.astype(BF16)


def _merge(attn, og, mg, x2, w_out_fox, w_out_gla, w_out, ln_gain, ln_bias):
    T = x2.shape[0]
    tm = 512
    row = lambda w: pl.BlockSpec((tm, w), lambda i: (i, 0))
    const = lambda a: pl.BlockSpec(a.shape, lambda i: (0, 0))
    ws = [w_out_fox.astype(BF16), w_out_gla.astype(BF16), w_out.astype(BF16),
          ln_gain.reshape(1, D_MODEL), ln_bias.reshape(1, D_MODEL)]
    return pl.pallas_call(
        _merge_kernel,
        grid=(T // tm,),
        in_specs=[row(FOX_WIDTH), row(GLA_V_WIDTH), row(2 * D_MODEL), row(D_MODEL)] + [const(w) for w in ws],
        out_specs=[row(D_MODEL), row(D_MODEL)],
        out_shape=[jax.ShapeDtypeStruct((T, D_MODEL), F32), jax.ShapeDtypeStruct((T, D_MODEL), BF16)],
        compiler_params=pltpu.CompilerParams(dimension_semantics=("arbitrary",), vmem_limit_bytes=VMEM_LIMIT),
        name="merge",
    )(attn, og, mg, x2, *ws)


ROUTE_TM = 256
CAND_BLOCKS = 10


def _sublane_all(x, op):
    for shift in (4, 2, 1):
        x = op(x, pltpu.roll(x, shift, axis=0))
    return x


def _take_max(s3, iota3, sentinel):
    m8 = _sublane_all(jnp.max(s3, axis=0), jnp.maximum)
    idx8 = _sublane_all(jnp.min(jnp.where(s3 == m8[None], iota3, sentinel), axis=0), jnp.minimum)
    return m8, idx8, iota3 == idx8[None]


def _row_iota(groups, n):
    shape = (groups, SUBLANES, n)
    return lax.broadcasted_iota(jnp.int32, shape, 0) * SUBLANES + lax.broadcasted_iota(jnp.int32, shape, 1)


def _route_kernel(x_ref, wq_ref, keys_ref, after_a_ref, after_b_ref, idx_ref, gate_ref,
                  q_scr, st_scr, it_scr, best_scr, pick_scr, gsel_scr):
    del after_a_ref, after_b_ref
    tm = ROUTE_TM
    K = PEER_TOPK
    q_scr[...] = _dot(x_ref[...], wq_ref[...])
    key_iota = _row_iota(PEER_N_KEYS // SUBLANES, tm)

    def stage1(pair, _):
        hps = (2 * pair, 2 * pair + 1)
        ss = []
        for hp in hps:
            q = q_scr[:, pl.ds(pl.multiple_of(hp * PEER_HALF, PEER_HALF), PEER_HALF)].astype(BF16)
            ss.append(_dot_nt(keys_ref[hp], q).reshape(PEER_N_KEYS // SUBLANES, SUBLANES, tm))
        for i in range(K):
            for n, hp in enumerate(hps):
                m8, idx8, hit = _take_max(ss[n], key_iota, PEER_N_KEYS)
                st_scr[hp, i:i + 1, :] = m8[0:1]
                it_scr[hp, i:i + 1, :] = idx8[0:1]
                ss[n] = jnp.where(hit, -jnp.inf, ss[n])
        return 0

    lax.fori_loop(0, PEER_HEADS, stage1, 0)

    cand_iota = _row_iota(CAND_BLOCKS, tm)

    def stage2(h, _):
        s0, s1 = st_scr[2 * h], st_scr[2 * h + 1]
        i0, i1 = it_scr[2 * h], it_scr[2 * h + 1]
        lo, hi = slice(0, SUBLANES), slice(SUBLANES, 2 * SUBLANES)
        cs = [s0[0:1] + s1[lo], s0[0:1] + s1[hi]]
        ci = [i0[0:1] * PEER_N_KEYS + i1[lo], i0[0:1] * PEER_N_KEYS + i1[hi]]
        for a in range(1, SUBLANES):
            cs.append(s0[a:a + 1] + s1[lo])
            ci.append(i0[a:a + 1] * PEER_N_KEYS + i1[lo])
        cs.append(s0[hi] + s1[0:1])
        ci.append(i0[hi] * PEER_N_KEYS + i1[0:1])
        cand = jnp.stack(cs, axis=0)
        cidx = jnp.stack(ci, axis=0)
        for i in range(K):
            m8, _, hit = _take_max(cand, cand_iota, CAND_BLOCKS * SUBLANES)
            pick8 = _sublane_all(jnp.max(jnp.where(hit, cidx, -1), axis=0), jnp.maximum)
            best_scr[i:i + 1, :] = m8[0:1]
            pick_scr[pl.ds(h * K + i, 1), :] = pick8[0:1]
            cand = jnp.where(hit, -jnp.inf, cand)
        best = best_scr[...]
        e = jnp.exp(best - best[0:1])
        gsel_scr[pl.ds(pl.multiple_of(h * K, K), K), :] = e / jnp.sum(e, axis=0, keepdims=True)
        return 0

    lax.fori_loop(0, PEER_HEADS, stage2, 0, unroll=2)
    idx_ref[...] = pick_scr[...].T
    gate_ref[...] = gsel_scr[...].T


def _route(x1, peer_w_query, peer_sub_keys, after, after_b):
    T = x1.shape[0]
    tm = ROUTE_TM
    wq = peer_w_query.reshape(D_MODEL, 2 * PEER_HEADS * PEER_HALF).astype(BF16)
    keys = peer_sub_keys.reshape(2 * PEER_HEADS, PEER_N_KEYS, PEER_HALF).astype(BF16)
    picks = pl.BlockSpec((tm, PEER_PICKS), lambda i: (i, 0))
    return pl.pallas_call(
        _route_kernel,
        grid=(T // tm,),
        in_specs=[pl.BlockSpec((tm, D_MODEL), lambda i: (i, 0)),
                  pl.BlockSpec(wq.shape, lambda i: (0, 0)),
                  pl.BlockSpec(keys.shape, lambda i: (0, 0, 0)),
                  pl.BlockSpec((SUBLANES, LANES), lambda i: (0, 0)),
                  pl.BlockSpec((SUBLANES, LANES), lambda i: (0, 0))],
        out_specs=[picks, picks],
        out_shape=[jax.ShapeDtypeStruct((T, PEER_PICKS), jnp.int32),
                   jax.ShapeDtypeStruct((T, PEER_PICKS), F32)],
        scratch_shapes=[pltpu.VMEM((tm, 2 * PEER_HEADS * PEER_HALF), F32),
                        pltpu.VMEM((2 * PEER_HEADS, PEER_TOPK, tm), F32),
                        pltpu.VMEM((2 * PEER_HEADS, PEER_TOPK, tm), jnp.int32),
                        pltpu.VMEM((PEER_TOPK, tm), F32),
                        pltpu.VMEM((PEER_PICKS, tm), jnp.int32),
                        pltpu.VMEM((PEER_PICKS, tm), F32)],
        compiler_params=pltpu.CompilerParams(dimension_semantics=("arbitrary",), vmem_limit_bytes=VMEM_LIMIT),
        name="peer_route",
    )(x1, wq, keys, after, after_b)


SC_LANES = 16
SC_ROWS = 64
SC_CHUNKS = PEER_PICKS // SC_ROWS
SC_DCOLS = 256
SC_DVREGS = SC_DCOLS // SC_LANES


def _sc_mesh():
    info = plsc.get_sparse_core_info()
    mesh = plsc.VectorSubcoreMesh(core_axis_name="c", subcore_axis_name="s")
    return mesh, info.num_cores, info.num_cores * info.num_subcores


def _sc_token_pipeline(tab_hbm, idx_hbm, vec_hbm, out_hbm, idx_v, vec_v, rows_v, out_v,
                       sem_rows, sem_tok, sem_out, n_cores, tpw, compute_chunk):
    wid = lax.axis_index("s") * n_cores + lax.axis_index("c")
    base = wid * tpw

    def gather(s, c):
        return pltpu.make_async_copy(tab_hbm.at[idx_v.at[s, pl.ds(c * SC_ROWS, SC_ROWS)]],
                                     rows_v.at[c % 2], sem_rows.at[c % 2])

    def tok_fetch(s, tok):
        return (pltpu.make_async_copy(idx_hbm.at[tok], idx_v.at[s], sem_tok.at[0]),
                pltpu.make_async_copy(vec_hbm.at[tok], vec_v.at[s], sem_tok.at[1]))

    def out_copy(s, tok):
        return pltpu.make_async_copy(out_v.at[s], out_hbm.at[tok], sem_out.at[s])

    for d in tok_fetch(0, base):
        d.start()
    for d in tok_fetch(0, base):
        d.wait()
    gather(0, 0).start()

    def pair_body(tp, _):
        for s in range(2):
            t = tp * 2 + s
            tok = base + t
            has_next = t + 1 < tpw

            @pl.when(has_next)
            def _():
                for d in tok_fetch(1 - s, tok + 1):
                    d.start()

            @pl.when(t >= 2)
            def _():
                out_copy(s, tok - 2).wait()

            for c in range(SC_CHUNKS):
                if c + 1 < SC_CHUNKS:
                    gather(s, c + 1).start()
                else:
                    @pl.when(has_next)
                    def _():
                        for d in tok_fetch(1 - s, tok + 1):
                            d.wait()
                        gather(1 - s, 0).start()
                gather(s, c).wait()
                compute_chunk(s, c)
            out_copy(s, tok).start()
        return 0

    lax.fori_loop(0, tpw // 2, pair_body, 0)
    for s in range(2):
        out_copy(s, base + tpw - 2 + s).wait()


def _sc_scratch(vec_len, out_len):
    return [pltpu.VMEM((2, PEER_PICKS), jnp.int32),
            pltpu.VMEM((2, vec_len), F32),
            pltpu.VMEM((2, SC_ROWS, D_MODEL // 2), jnp.int32),
            pltpu.VMEM((2, out_len), F32),
            pltpu.SemaphoreType.DMA((2,)),
            pltpu.SemaphoreType.DMA((2,)),
            pltpu.SemaphoreType.DMA((2,))]


def _pack_table(tab):
    n, d = tab.shape
    bits = lax.bitcast_convert_type(tab.astype(BF16), jnp.uint16).astype(jnp.uint32).reshape(n, d // 32, 2, SC_LANES)
    words = bits[:, :, 0, :] | (bits[:, :, 1, :] << 16)
    return lax.bitcast_convert_type(words, jnp.int32).reshape(n, d // 2)


def _unpack_pairs(x):
    return list(plsc.unpack(x, format=plsc.PackFormat.INTERLEAVED))


def _sc_acc(v_packed, idx, w):
    T = w.shape[0]
    mesh, n_cores, n_workers = _sc_mesh()
    tpw = T // n_workers
    L = SC_LANES

    @functools.partial(
        pl.kernel, mesh=mesh, out_type=jax.ShapeDtypeStruct((T, D_MODEL), F32),
        scratch_types=_sc_scratch(PEER_PICKS, D_MODEL),
        compiler_params=pltpu.CompilerParams(needs_layout_passes=False), name="peer_sc_acc")
    def k(v_hbm, idx_hbm, w_hbm, out_hbm, idx_v, w_v, rows_v, out_v, sem_rows, sem_tok, sem_out):
        def compute_chunk(s, c):
            b = c % 2
            for dc in range(D_MODEL // SC_DCOLS):
                cols = [pl.ds(dc * SC_DCOLS + j * L, L) for j in range(SC_DVREGS)]
                if c == 0:
                    accs = tuple(jnp.zeros((L,), F32) for _ in cols)
                else:
                    accs = tuple(out_v[s, cs] for cs in cols)

                def row_body(r, accs, dc=dc):
                    wb = plsc.load_gather(w_v.at[s], [jnp.full((L,), c * SC_ROWS, jnp.int32) + r])
                    new = []
                    for j in range(SC_DVREGS // 2):
                        words = rows_v[b, r, pl.ds(dc * (SC_DCOLS // 2) + j * L, L)]
                        lo, hi = _unpack_pairs(plsc.bitcast(words, BF16))
                        new += [accs[2 * j] + wb * lo, accs[2 * j + 1] + wb * hi]
                    return tuple(new)

                accs = lax.fori_loop(0, SC_ROWS, row_body, accs)
                for a, cs in zip(accs, cols):
                    out_v[s, cs] = a

        _sc_token_pipeline(v_hbm, idx_hbm, w_hbm, out_hbm, idx_v, w_v, rows_v, out_v,
                           sem_rows, sem_tok, sem_out, n_cores, tpw, compute_chunk)

    return k(v_packed, idx, w)


DENSE_TM = 2048
DENSE_TN = 1024


def _dense_act_kernel(x_ref, u_ref, o_ref):
    o_ref[...] = _dot_nt(x_ref[...], u_ref[...])


def _dense_act(x1, u_bf16):
    T, N = x1.shape[0], u_bf16.shape[0]
    return pl.pallas_call(
        _dense_act_kernel, grid=(T // DENSE_TM, N // DENSE_TN),
        in_specs=[pl.BlockSpec((DENSE_TM, D_MODEL), lambda i, j: (i, 0)),
                  pl.BlockSpec((DENSE_TN, D_MODEL), lambda i, j: (j, 0))],
        out_specs=pl.BlockSpec((DENSE_TM, DENSE_TN), lambda i, j: (i, j)),
        out_shape=jax.ShapeDtypeStruct((T, N), F32),
        compiler_params=pltpu.CompilerParams(dimension_semantics=("arbitrary", "arbitrary"),
                                             vmem_limit_bytes=VMEM_LIMIT),
        name="peer_dense_act",
    )(x1, u_bf16)


PICK_TG = SUBLANES
PICK_NQ = 4096


def _sc_pick(dense, idx):
    T, N = dense.shape
    mesh, n_cores, n_workers = _sc_mesh()
    gpw = T // PICK_TG // n_workers
    n_q = N // PICK_NQ
    L = SC_LANES

    @functools.partial(
        pl.kernel, mesh=mesh, out_type=jax.ShapeDtypeStruct((T, PEER_PICKS), F32),
        scratch_types=[pltpu.VMEM((2, PICK_TG, PEER_PICKS), jnp.int32), pltpu.VMEM((2, PICK_TG, PICK_NQ), F32),
                       pltpu.VMEM((2, PICK_TG, PEER_PICKS), F32),
                       pltpu.SemaphoreType.DMA((2,)), pltpu.SemaphoreType.DMA((2,)), pltpu.SemaphoreType.DMA((2,))],
        compiler_params=pltpu.CompilerParams(needs_layout_passes=False), name="peer_sc_pick")
    def k(d_hbm, idx_hbm, act_hbm, idx_v, buf, act_v, sem_idx, sem_buf, sem_out):
        g0 = (lax.axis_index("s") * n_cores + lax.axis_index("c")) * gpw

        def idx_fetch(gs, g):
            return pltpu.make_async_copy(idx_hbm.at[pl.ds(g * PICK_TG, PICK_TG)], idx_v.at[gs], sem_idx.at[gs])

        def blk_fetch(g, q):
            return pltpu.make_async_copy(d_hbm.at[pl.ds(g * PICK_TG, PICK_TG), pl.ds(q * PICK_NQ, PICK_NQ)],
                                         buf.at[q % 2], sem_buf.at[q % 2])

        def out_copy(gs, g):
            return pltpu.make_async_copy(act_v.at[gs], act_hbm.at[pl.ds(g * PICK_TG, PICK_TG)], sem_out.at[gs])

        idx_fetch(0, g0).start()
        blk_fetch(g0, 0).start()

        def pair_body(gp, _):
            for gs in range(2):
                gi = gp * 2 + gs
                g = g0 + gi
                has_next = gi + 1 < gpw

                @pl.when(has_next)
                def _():
                    idx_fetch(1 - gs, g + 1).start()

                idx_fetch(gs, g).wait()

                @pl.when(gi >= 2)
                def _():
                    out_copy(gs, g - 2).wait()

                for q in range(n_q):
                    if q + 1 < n_q:
                        blk_fetch(g, q + 1).start()
                    else:
                        @pl.when(has_next)
                        def _():
                            blk_fetch(g + 1, 0).start()
                    blk_fetch(g, q).wait()
                    for t in range(PICK_TG):
                        for v in range(PEER_PICKS // L):
                            picks = pl.ds(v * L, L)
                            local = idx_v[gs, t, picks] - q * PICK_NQ
                            inside = (local >= 0) & (local < PICK_NQ)
                            val = plsc.load_gather(buf, [jnp.full((L,), q % 2, jnp.int32), jnp.full((L,), t, jnp.int32),
                                                         jnp.where(inside, local, 0)])
                            act_v[gs, t, picks] = val if q == 0 else jnp.where(inside, val, act_v[gs, t, picks])
                out_copy(gs, g).start()
            return 0

        lax.fori_loop(0, gpw // 2, pair_body, 0)
        for gs in range(2):
            out_copy(gs, g0 + gpw - 2 + gs).wait()

    return k(dense, idx)


def _gelu_gate_kernel(a_ref, g_ref, o_ref):
    a = a_ref[...]
    o_ref[...] = 0.5 * a * (1.0 + lax.erf(a * (2.0 ** -0.5))) * g_ref[...]


def _ln2_kernel(h_ref, f_ref, g_ref, b_ref, full_ref, o_ref):
    del full_ref
    o_ref[...] = _layer_norm(ALPHA * h_ref[...] + f_ref[...], g_ref[...], b_ref[...])


MIX_TM = 512
_MIX_PARAMS = pltpu.CompilerParams(dimension_semantics=("arbitrary",))


def _mix_row(width):
    return pl.BlockSpec((MIX_TM, width), lambda i: (i, 0))


def _gelu_gate_call(act, gates):
    T = act.shape[0]
    return pl.pallas_call(
        _gelu_gate_kernel, grid=(T // MIX_TM,), in_specs=[_mix_row(PEER_PICKS)] * 2, out_specs=_mix_row(PEER_PICKS),
        out_shape=jax.ShapeDtypeStruct((T, PEER_PICKS), F32), compiler_params=_MIX_PARAMS, name="peer_gelu_gate",
    )(act, gates)


def _mix_finish(x1, idx, w, v_packed, ln_gain, ln_bias, result, row0):
    T = x1.shape[0]
    const = pl.BlockSpec((1, D_MODEL), lambda i: (0, 0))
    blk0 = row0 // MIX_TM
    ffn = _sc_acc(v_packed, idx, w)
    result = pl.pallas_call(
        _ln2_kernel, grid=(T // MIX_TM,),
        in_specs=[_mix_row(D_MODEL)] * 2 + [const, const, pl.BlockSpec(memory_space=pl.ANY)],
        out_specs=pl.BlockSpec((MIX_TM, D_MODEL), lambda i: (blk0 + i, 0)),
        out_shape=jax.ShapeDtypeStruct(result.shape, F32), input_output_aliases={4: 0},
        compiler_params=_MIX_PARAMS, name="peer_ln2",
    )(x1, ffn, ln_gain.reshape(1, D_MODEL), ln_bias.reshape(1, D_MODEL), result)
    return result, ffn


BATCH_CHUNKS = 16
SC_GATE_LAG = 1
SC_PASS_LAG = 3


def kernel(x, w_in, fox_f_bias, gla_gate_up, gla_gate_bias, gla_norm_gain, w_out_fox, w_out_gla, w_out,
           ln1_gain, ln1_bias, peer_w_query, peer_sub_keys, peer_expert_u, peer_expert_v, ln2_gain, ln2_bias):
    B, S, D = x.shape
    assert D == D_MODEL and S % max(ROUTE_TM, FOX_TQ) == 0
    assert DEPTH == 1 and w_in.shape[0] == 1, "the chunk pipeline below is written for the single-layer block"
    u_bf16, v_packed = peer_expert_u[0].astype(BF16), _pack_table(peer_expert_v[0])
    n_chunks = BATCH_CHUNKS if B % BATCH_CHUNKS == 0 else 1
    bc = B // n_chunks
    ffns, ws = [], []
    result = jnp.zeros((B * S, D), F32)
    for ci in range(n_chunks):
        h = x[ci * bc:(ci + 1) * bc].reshape(bc * S, D)
        fqkv, gqk, gv, gg, mg, small = _in_proj(h, w_in[0])
        c_t = _fox_gate(small, fox_f_bias[0], bc, S)
        attn = _fox_attn(fqkv, c_t, bc, S)
        og = _gla(gqk, gv, small, gg, gla_gate_up[0], gla_gate_bias[0], gla_norm_gain[0], bc, S)
        x1, x1b = _merge(attn, og, mg, h, w_out_fox[0], w_out_gla[0], w_out[0], ln1_gain[0], ln1_bias[0])
        zero = jnp.zeros((SUBLANES, LANES), F32)
        after = ws[ci - SC_GATE_LAG][:SUBLANES, :LANES] if ci >= SC_GATE_LAG else zero
        after_b = ffns[ci - SC_PASS_LAG][:SUBLANES, :LANES] if ci >= SC_PASS_LAG else zero
        idx, gates = _route(x1b, peer_w_query[0], peer_sub_keys[0], after, after_b)
        act = _sc_pick(_dense_act(x1b, u_bf16), idx)
        w = _gelu_gate_call(act, gates)
        result, ffn = _mix_finish(x1, idx, w, v_packed, ln2_gain[0], ln2_bias[0], result, ci * bc * S)
        ffns.append(ffn)
        ws.append(w)
    return result.reshape(B, S, D)
```
